```python
import math
import jax, jax.numpy as jnp
from jax import lax
import numpy as np

D_MODEL = 1024
BATCH = 4
SEQ = 4096
DEPTH = 1

HEAD_DIM = 64
FOX_HEADS = 8
NSA_HEADS = 8
NSA_KV_GROUPS = 2
NSA_REP = NSA_HEADS // NSA_KV_GROUPS
CMP_BLOCK = 32
CMP_STRIDE = 16
CMP_HIDDEN = 256
SLC_BLOCK = 64
SLC_TOPK = 16
WINDOW = 512
Q_BLOCK = 128
REL_BUCKETS = 32
REL_MAX_DIST = 128
D_FF = 4 * D_MODEL
RMS_EPS = 1e-6
NEG_INF = -1e30
FORCED_SCORE = 1e4
SCALE = HEAD_DIM ** -0.5

FOX_W = FOX_HEADS * HEAD_DIM
NSA_W = NSA_HEADS * HEAD_DIM
KV_W = NSA_KV_GROUPS * HEAD_DIM
IN_SPLITS = (FOX_W, FOX_W, FOX_W, FOX_HEADS,
             NSA_W, KV_W, KV_W, KV_W, KV_W, KV_W, KV_W, 3 * NSA_HEADS,
             D_MODEL, D_MODEL)
IN_WIDTH = sum(IN_SPLITS)

kernel_name = "hybrid_fox_nsa_gated_block"


def rms_norm(x, g):
    xf = x.astype(jnp.float32)
    y = xf * lax.rsqrt(jnp.mean(xf * xf, axis=-1, keepdims=True) + RMS_EPS)
    return (y * g.astype(jnp.float32)).astype(x.dtype)


def masked_softmax(s, mask):
    p = jax.nn.softmax(jnp.where(mask, s, NEG_INF), axis=-1)
    return jnp.where(mask, p, 0.0)


def rel_bucket(dist):
    n = jnp.maximum(dist, 0)
    exact = REL_BUCKETS // 2
    nf = jnp.maximum(n, exact).astype(jnp.float32)
    log_b = exact + (jnp.log(nf / exact) / math.log(REL_MAX_DIST / exact)
                     * (REL_BUCKETS - exact)).astype(jnp.int32)
    return jnp.where(n < exact, n, jnp.minimum(log_b, REL_BUCKETS - 1))


def fox_attention(q, k, v, f_logit, b_f):
    B, T, H, DH = q.shape
    nb = T // Q_BLOCK
    log_f = jax.nn.log_sigmoid(f_logit.astype(jnp.float32) + b_f.astype(jnp.float32))
    c = jnp.cumsum(log_f, axis=1).transpose(0, 2, 1)
    qb = q.reshape(B, nb, Q_BLOCK, H, DH).transpose(1, 0, 3, 2, 4)
    cqb = c.reshape(B, H, nb, Q_BLOCK).transpose(2, 0, 1, 3)
    kpos = jnp.arange(T)

    def block(args):
        i, qi, ci = args
        qpos = i * Q_BLOCK + jnp.arange(Q_BLOCK)
        s = jnp.einsum('bhqd,bkhd->bhqk', qi, k).astype(jnp.float32) * SCALE
        s = s + ci[..., None] - c[:, :, None, :]
        p = masked_softmax(s, kpos[None, :] <= qpos[:, None])
        return jnp.einsum('bhqk,bkhd->bqhd', p.astype(v.dtype), v)

    o = lax.map(block, (jnp.arange(nb), qb, cqb))
    return o.transpose(1, 0, 2, 3, 4).reshape(B, T, H * DH)


def compress_blocks(kv, pe, w1, w2):
    B, T, G, DH = kv.shape
    nc = (T - CMP_BLOCK) // CMP_STRIDE + 1
    idx = np.arange(nc)[:, None] * CMP_STRIDE + np.arange(CMP_BLOCK)[None, :]
    blk = kv[:, idx] + pe[:, None, :]
    blk = blk.transpose(0, 3, 1, 2, 4).reshape(B, G, nc, CMP_BLOCK * DH)
    return jax.nn.silu(blk @ w1) @ w2


def nsa_attention(q, k_cmp, v_cmp, k_slc, v_slc, k_win, v_win, gate_logit, rel_table,
                  pe_k, w1_k, w2_k, pe_v, w1_v, w2_v):
    B, T, H, DH = q.shape
    G, R = NSA_KV_GROUPS, NSA_REP
    nb = T // Q_BLOCK
    nc = (T - CMP_BLOCK) // CMP_STRIDE + 1
    ns = T // SLC_BLOCK
    n_sel = min(SLC_TOPK, ns)

    kc = compress_blocks(k_cmp, pe_k, w1_k, w2_k)
    vc = compress_blocks(v_cmp, pe_v, w1_v, w2_v)
    cmp_start = np.arange(nc) * CMP_STRIDE
    cmp_end = jnp.asarray(cmp_start + CMP_BLOCK - 1, dtype=jnp.int32)
    slc_start = np.arange(ns) * SLC_BLOCK
    overlap = jnp.asarray(((cmp_start[:, None] < slc_start[None, :] + SLC_BLOCK)
                           & (cmp_start[:, None] + CMP_BLOCK > slc_start[None, :])).astype(np.float32))

    tbl_heads = rel_table.T
    tbl_grp = tbl_heads.reshape(G, R, REL_BUCKETS)
    g_idx = jnp.arange(G)[None, :, None, None, None]
    r_idx = jnp.arange(R)[None, None, :, None, None]

    ks = k_slc.transpose(0, 2, 1, 3).reshape(B, G, ns, SLC_BLOCK, DH)
    vs = v_slc.transpose(0, 2, 1, 3).reshape(B, G, ns, SLC_BLOCK, DH)
    pad = ((0, 0), (0, 0), (WINDOW, 0), (0, 0))
    kwp = jnp.pad(k_win.transpose(0, 2, 1, 3), pad)
    vwp = jnp.pad(v_win.transpose(0, 2, 1, 3), pad)
    win_dist = jnp.asarray(np.arange(Q_BLOCK)[:, None] + WINDOW - np.arange(Q_BLOCK + WINDOW)[None, :],
                           dtype=jnp.int32)
    bias_win = tbl_heads[:, rel_bucket(win_dist)].reshape(G, R, Q_BLOCK, Q_BLOCK + WINDOW)

    qb = q.reshape(B, nb, Q_BLOCK, G, R, DH).transpose(1, 0, 3, 4, 2, 5)
    b_idx = jnp.arange(B)[:, None, None, None]
    gk_idx = jnp.arange(G)[None, :, None, None]
    j = jnp.arange(ns)

    def block(args):
        i, qi = args
        qpos = i * Q_BLOCK + jnp.arange(Q_BLOCK)
        s = jnp.einsum('bgrqd,bgcd->bgrqc', qi, kc).astype(jnp.float32) * SCALE
        bias = tbl_heads[:, rel_bucket(qpos[:, None] - cmp_end[None, :])].reshape(G, R, Q_BLOCK, nc)
        p_cmp = masked_softmax(s + bias, cmp_end[None, :] <= qpos[:, None])
        o_cmp = jnp.einsum('bgrqc,bgcd->bgrqd', p_cmp.astype(vc.dtype), vc)
        imp = jnp.einsum('bgrqc,cs->bgqs', p_cmp, overlap)
        cur = qpos // SLC_BLOCK
        forced = (j[None, :] == 0) | (j[None, :] == cur[:, None]) | (j[None, :] == cur[:, None] - 1)
        valid = j[None, :] * SLC_BLOCK <= qpos[:, None]
        imp = jnp.where(forced, FORCED_SCORE, jnp.where(valid, imp, -1.0))
        _, sel = lax.top_k(imp, n_sel)
        k_sel = ks[b_idx, gk_idx, sel].reshape(B, G, Q_BLOCK, n_sel * SLC_BLOCK, DH)
        v_sel = vs[b_idx, gk_idx, sel].reshape(B, G, Q_BLOCK, n_sel * SLC_BLOCK, DH)
        pos = (sel[..., None] * SLC_BLOCK + jnp.arange(SLC_BLOCK)).reshape(B, G, Q_BLOCK, n_sel * SLC_BLOCK)
        s = jnp.einsum('bgrqd,bgqkd->bgrqk', qi, k_sel).astype(jnp.float32) * SCALE
        bias = tbl_grp[g_idx, r_idx, rel_bucket(qpos[:, None] - pos)[:, :, None]]
        p = masked_softmax(s + bias, (pos <= qpos[:, None])[:, :, None])
        o_slc = jnp.einsum('bgrqk,bgqkd->bgrqd', p.astype(v_sel.dtype), v_sel)
        kw = lax.dynamic_slice_in_dim(kwp, i * Q_BLOCK, Q_BLOCK + WINDOW, axis=2)
        vw = lax.dynamic_slice_in_dim(vwp, i * Q_BLOCK, Q_BLOCK + WINDOW, axis=2)
        kpos = i * Q_BLOCK - WINDOW + jnp.arange(Q_BLOCK + WINDOW)
        wmask = ((kpos[None, :] >= 0) & (kpos[None, :] <= qpos[:, None])
                 & (qpos[:, None] - kpos[None, :] < WINDOW))
        s = jnp.einsum('bgrqd,bgkd->bgrqk', qi, kw).astype(jnp.float32) * SCALE + bias_win
        p = masked_softmax(s, wmask)
        o_win = jnp.einsum('bgrqk,bgkd->bgrqd', p.astype(vw.dtype), vw)
        return jnp.stack([o_cmp, o_slc, o_win], axis=-2)

    o = lax.map(block, (jnp.arange(nb), qb))
    o = o.transpose(1, 0, 4, 2, 3, 5, 6).reshape(B, T, H, 3, DH)
    gates = jax.nn.sigmoid(gate_logit)[..., None]
    return jnp.sum(gates * o, axis=3).reshape(B, T, H * DH)


def setup_inputs(seed: int = 0) -> dict:
    key = jax.random.key(seed)
    ks = jax.random.split(key, 20)

    def w(k, shape, fan_in):
        return jax.random.normal(k, shape, jnp.float32) * fan_in ** -0.5

    def gain(k, shape):
        return 1.0 + 0.05 * jax.random.normal(k, shape, jnp.float32)

    nrm = lambda k, shape: jax.random.normal(k, shape, jnp.float32)
    L = DEPTH
    return {
        "x": nrm(ks[0], (BATCH, SEQ, D_MODEL)),
        "rel_bias_table": 0.5 * nrm(ks[1], (REL_BUCKETS, NSA_HEADS)),
        "g_attn": gain(ks[2], (L, D_MODEL)),
        "w_in": w(ks[3], (L, D_MODEL, IN_WIDTH), D_MODEL),
        "b_forget": 3.0 + 0.5 * nrm(ks[4], (L, FOX_HEADS)),
        "cmp_pe_k": 0.5 * nrm(ks[5], (L, CMP_BLOCK, HEAD_DIM)),
        "cmp_w1_k": w(ks[6], (L, CMP_BLOCK * HEAD_DIM, CMP_HIDDEN), CMP_BLOCK * HEAD_DIM),
        "cmp_w2_k": w(ks[7], (L, CMP_HIDDEN, HEAD_DIM), CMP_HIDDEN),
        "cmp_pe_v": 0.5 * nrm(ks[8], (L, CMP_BLOCK, HEAD_DIM)),
        "cmp_w1_v": w(ks[9], (L, CMP_BLOCK * HEAD_DIM, CMP_HIDDEN), CMP_BLOCK * HEAD_DIM),
        "cmp_w2_v": w(ks[10], (L, CMP_HIDDEN, HEAD_DIM), CMP_HIDDEN),
        "w_br_fox": w(ks[11], (L, FOX_W, D_MODEL), FOX_W),
        "w_br_nsa": w(ks[12], (L, NSA_W, D_MODEL), NSA_W),
        "w_out": w(ks[13], (L, D_MODEL, D_MODEL), D_MODEL),
        "g_mlp": gain(ks[14], (L, D_MODEL)),
        "w_ff1": w(ks[15], (L, D_MODEL, D_FF), D_MODEL),
        "w_ff2": w(ks[16], (L, D_FF, D_MODEL), D_FF),
        "g_final": gain(ks[17], (D_MODEL,)),
    }


def reference(x, rel_bias_table, g_attn, w_in, b_forget, cmp_pe_k, cmp_w1_k, cmp_w2_k,
              cmp_pe_v, cmp_w1_v, cmp_w2_v, w_br_fox, w_br_nsa, w_out, g_mlp, w_ff1, w_ff2,
              g_final):
    B, T, _ = x.shape
    offsets = np.cumsum(IN_SPLITS)[:-1].tolist()
    for l in range(DEPTH):
        h = rms_norm(x, g_attn[l])
        z = h @ w_in[l]
        (fq, fk, fv, f_logit, nq, kc, vc, ksl, vsl, kwn, vwn, n_gate, g_a, g_b) = jnp.split(z, offsets, axis=-1)
        fox_h = lambda t: t.reshape(B, T, FOX_HEADS, HEAD_DIM)
        kv_h = lambda t: t.reshape(B, T, NSA_KV_GROUPS, HEAD_DIM)
        o_fox = fox_attention(fox_h(fq), fox_h(fk), fox_h(fv), f_logit, b_forget[l])
        o_nsa = nsa_attention(nq.reshape(B, T, NSA_HEADS, HEAD_DIM), kv_h(kc), kv_h(vc), kv_h(ksl), kv_h(vsl),
                              kv_h(kwn), kv_h(vwn), n_gate.reshape(B, T, NSA_HEADS, 3), rel_bias_table,
                              cmp_pe_k[l], cmp_w1_k[l], cmp_w2_k[l], cmp_pe_v[l], cmp_w1_v[l], cmp_w2_v[l])
        merged = jax.nn.sigmoid(g_a) * (o_fox @ w_br_fox[l]) + jax.nn.sigmoid(g_b) * (o_nsa @ w_br_nsa[l])
        x = x + merged @ w_out[l]
        h = rms_norm(x, g_mlp[l])
        x = x + jnp.square(jax.nn.relu(h @ w_ff1[l])) @ w_ff2[l]
    return rms_norm(x, g_final)
```

```python
import functools
import math

import numpy as np
import jax
import jax.numpy as jnp
from jax import lax
from jax.experimental import pallas as pl
from jax.experimental.pallas import tpu as pltpu

F32 = jnp.float32
BF16 = jnp.bfloat16

D_MODEL = 1024
HEAD_DIM = 64
FOX_HEADS = 8
NSA_HEADS = 8
NSA_GROUPS = 2
NSA_REP = NSA_HEADS // NSA_GROUPS
CMP_BLOCK = 32
CMP_STRIDE = 16
CMP_HIDDEN = 256
SLC_BLOCK = 64
SLC_TOPK = 16
WINDOW = 512
REL_BUCKETS = 32
REL_MAX_DIST = 128
D_FF = 4 * D_MODEL
RMS_EPS = 1e-6
NEG_INF = -1e30
FORCED_SCORE = 1e4
SCALE = HEAD_DIM ** -0.5

FOX_W = FOX_HEADS * HEAD_DIM
NSA_W = NSA_HEADS * HEAD_DIM
KV_W = NSA_GROUPS * HEAD_DIM
LANES = 128
NSA_TILE = 256
REL_FAR = 113
CMP_BAND = 32

VMEM_LIMIT = 56 * 1024 * 1024

_NT = (((1,), (1,)), ((), ()))


def _dot(a, b):
    return jnp.dot(a, b, preferred_element_type=F32)


def _dot_nt(a, b):
    return lax.dot_general(a, b, _NT, preferred_element_type=F32)


def _split3(x):
    hi = x.astype(BF16)
    r1 = x - hi.astype(F32)
    mid = r1.astype(BF16)
    lo = (r1 - mid.astype(F32)).astype(BF16)
    return hi, mid, lo


def _rms(x, g):
    return x * lax.rsqrt(jnp.mean(x * x, axis=-1, keepdims=True) + RMS_EPS) * g


def _rel_bucket_np(n):
    exact = REL_BUCKETS // 2
    nf = np.maximum(n, exact).astype(np.float64)
    log_b = exact + (np.log(nf / exact) / math.log(REL_MAX_DIST / exact) * (REL_BUCKETS - exact)).astype(np.int64)
    return np.where(n < exact, n, np.minimum(log_b, REL_BUCKETS - 1))


_SEG_FQ, _SEG_FK, _SEG_NQ, _SEG_NK, _SEG_FV, _SEG_NV, _SEG_CKV, _SEG_MISC = (
    0, 512, 1024, 1536, 1792, 2304, 2560, 2816)
_W_ALL = 2944


def _inproj_kernel(x_ref, g_ref, w_ref, fq_ref, fk_ref, nq_ref, nk_ref, fv_ref, nv_ref, ckv_ref, misc_ref):
    h = _rms(x_ref[...], g_ref[...]).astype(BF16)

    def seg(a, n):
        return _dot(h, w_ref[:, a:a + n])

    def heads(ref, a, n_heads):
        r = seg(a, n_heads * HEAD_DIM)
        for j in range(n_heads):
            ref[j] = r[:, j * HEAD_DIM:(j + 1) * HEAD_DIM].astype(BF16)

    heads(fq_ref, _SEG_FQ, FOX_HEADS)
    heads(fk_ref, _SEG_FK, FOX_HEADS)
    heads(nq_ref, _SEG_NQ, NSA_HEADS)
    heads(nk_ref, _SEG_NK, 2 * NSA_GROUPS)
    fv_ref[...] = seg(_SEG_FV, FOX_W).astype(BF16)
    nv_ref[...] = seg(_SEG_NV, 2 * KV_W).astype(BF16)
    ckv_ref[...] = seg(_SEG_CKV, 2 * KV_W).astype(BF16)
    misc_ref[...] = seg(_SEG_MISC, LANES)


def _in_proj(x2, g, w_all, tm):
    M = x2.shape[0]
    row = lambda i: (i, 0)
    hrow = lambda i: (0, i, 0)
    return pl.pallas_call(
        _inproj_kernel,
        grid=(M // tm,),
        in_specs=[pl.BlockSpec((tm, D_MODEL), row),
                  pl.BlockSpec((1, D_MODEL), lambda i: (0, 0)),
                  pl.BlockSpec((D_MODEL, _W_ALL), lambda i: (0, 0))],
        out_specs=[pl.BlockSpec((FOX_HEADS, tm, HEAD_DIM), hrow),
                   pl.BlockSpec((FOX_HEADS, tm, HEAD_DIM), hrow),
                   pl.BlockSpec((NSA_HEADS, tm, HEAD_DIM), hrow),
                   pl.BlockSpec((2 * NSA_GROUPS, tm, HEAD_DIM), hrow),
                   pl.BlockSpec((tm, FOX_W), row),
                   pl.BlockSpec((tm, 2 * KV_W), row),
                   pl.BlockSpec((tm, 2 * KV_W), row),
                   pl.BlockSpec((tm, LANES), row)],
        out_shape=[jax.ShapeDtypeStruct((FOX_HEADS, M, HEAD_DIM), BF16),
                   jax.ShapeDtypeStruct((FOX_HEADS, M, HEAD_DIM), BF16),
                   jax.ShapeDtypeStruct((NSA_HEADS, M, HEAD_DIM), BF16),
                   jax.ShapeDtypeStruct((2 * NSA_GROUPS, M, HEAD_DIM), BF16),
                   jax.ShapeDtypeStruct((M, FOX_W), BF16),
                   jax.ShapeDtypeStruct((M, 2 * KV_W), BF16),
                   jax.ShapeDtypeStruct((M, 2 * KV_W), BF16),
                   jax.ShapeDtypeStruct((M, LANES), F32)],
        compiler_params=pltpu.CompilerParams(dimension_semantics=("parallel",), vmem_limit_bytes=VMEM_LIMIT),
        name="in_proj",
    )(x2, g, w_all)


_SCAN_BLK = 256


def _decay_kernel(misc_ref, bf_ref, crow_ref, ccol_sc, *, T):
    r = lax.broadcasted_iota(jnp.int32, (_SCAN_BLK, _SCAN_BLK), 0)
    c = lax.broadcasted_iota(jnp.int32, (_SCAN_BLK, _SCAN_BLK), 1)
    tri = jnp.where(r >= c, 1.0, 0.0).astype(BF16)

    def blk(n, carry):
        s0 = pl.multiple_of(n * _SCAN_BLK, _SCAN_BLK)
        x = misc_ref[pl.ds(s0, _SCAN_BLK), :] + bf_ref[...]
        lf = jnp.minimum(x, 0.0) - jnp.log1p(jnp.exp(-jnp.abs(x)))
        hi, mid, lo = _split3(lf)
        cs = _dot(tri, hi) + _dot(tri, mid) + _dot(tri, lo) + carry
        ccol_sc[pl.ds(s0, _SCAN_BLK), :] = cs
        return cs[_SCAN_BLK - 1:_SCAN_BLK, :]

    lax.fori_loop(0, T // _SCAN_BLK, blk, jnp.zeros((1, LANES), F32))
    crow_ref[0] = ccol_sc[...].T[0:FOX_HEADS, :]


def _fox_decay(misc, bf_pad, B, T):
    return pl.pallas_call(
        functools.partial(_decay_kernel, T=T),
        grid=(B,),
        in_specs=[pl.BlockSpec((T, LANES), lambda b: (b, 0)),
                  pl.BlockSpec((1, LANES), lambda b: (0, 0))],
        out_specs=pl.BlockSpec((1, FOX_HEADS, T), lambda b: (b, 0, 0)),
        out_shape=jax.ShapeDtypeStruct((B, FOX_HEADS, T), F32),
        scratch_shapes=[pltpu.VMEM((T, LANES), F32)],
        compiler_params=pltpu.CompilerParams(dimension_semantics=("parallel",), vmem_limit_bytes=VMEM_LIMIT),
        name="fox_decay",
    )(misc, bf_pad)


def _fox_kernel(q_ref, k_ref, v_ref, c_ref, o_ref, *, tq, tk):
    hp = pl.program_id(1)
    i = pl.program_id(2)
    t0 = i * tq
    rows = t0 + lax.broadcasted_iota(jnp.int32, (tq, 1), 0)
    lane = lax.broadcasted_iota(jnp.int32, (1, LANES), 1)
    n_full = (i * tq) // tk
    n_diag = tq // tk
    outs = []
    for hh in range(2):
        q = q_ref[hh]
        h = 2 * hp + hh

        def chunk(j, carry, masked, hh=hh, q=q, h=h):
            m, l, acc = carry
            ks = pl.multiple_of(j * tk, tk)
            k = k_ref[hh, pl.ds(ks, tk), :]
            v = v_ref[pl.ds(ks, tk), :]
            s = _dot_nt(q, k) - c_ref[0, pl.ds(h, 1), pl.ds(ks, tk)]
            if masked:
                cols = ks + lax.broadcasted_iota(jnp.int32, (1, tk), 1)
                s = jnp.where(cols <= rows, s, NEG_INF)
            m_new = jnp.maximum(m, jnp.max(s, axis=-1, keepdims=True))
            p = jnp.exp(s - m_new)
            alpha = jnp.exp(m - m_new)
            l = alpha * l + jnp.sum(p, axis=-1, keepdims=True)
            acc = alpha * acc + _dot(p.astype(BF16), v)
            return m_new, l, acc

        carry = (jnp.full((tq, 1), NEG_INF, F32), jnp.zeros((tq, 1), F32), jnp.zeros((tq, LANES), F32))
        carry = lax.fori_loop(0, n_full, functools.partial(chunk, masked=False), carry)
        for d in range(n_diag):
            carry = chunk(n_full + d, carry, True)
        m, l, acc = carry
        outs.append(acc * (1.0 / l))
    o_ref[...] = jnp.where(lane < HEAD_DIM, outs[0], outs[1]).astype(BF16)


def _fox_attention(fq, fk, fv, crow, B, T, tq, tk):
    nT = T // tq
    return pl.pallas_call(
        functools.partial(_fox_kernel, tq=tq, tk=tk),
        grid=(B, FOX_HEADS // 2, nT),
        in_specs=[pl.BlockSpec((2, tq, HEAD_DIM), lambda b, hp, i: (hp, b * nT + i, 0)),
                  pl.BlockSpec((2, T, HEAD_DIM), lambda b, hp, i: (hp, b, 0)),
                  pl.BlockSpec((T, LANES), lambda b, hp, i: (b, hp)),
                  pl.BlockSpec((1, FOX_HEADS, T), lambda b, hp, i: (b, 0, 0))],
        out_specs=pl.BlockSpec((tq, LANES), lambda b, hp, i: (b * nT + i, hp)),
        out_shape=jax.ShapeDtypeStruct((B * T, FOX_W), BF16),
        compiler_params=pltpu.CompilerParams(dimension_semantics=("parallel", "parallel", "arbitrary"),
                                             vmem_limit_bytes=VMEM_LIMIT),
        name="fox_attn",
    )(fq, fk, fv, crow)


def _compress_kernel(r_ref, w1_ref, pe_ref, w2k_ref, w2v_ref, kc_ref, vc_ref, *, NR):
    half = CMP_STRIDE * HEAD_DIM
    vacc = jnp.zeros((NR, LANES), F32)
    for idx in range(2 * NSA_GROUPS):
        kind, g = divmod(idx, NSA_GROUPS)
        rm = r_ref[0, idx]
        a = _dot(rm, w1_ref[kind, 0:half, :])
        bm = _dot(rm, w1_ref[kind, half:2 * half, :])
        pe_term = _dot(pe_ref[kind], w1_ref[kind])[0:1, :]
        pre = a + pltpu.roll(bm, NR - 1, 0) + pe_term
        hid = (pre * jax.nn.sigmoid(pre)).astype(BF16)
        if kind == 0:
            kc_ref[0, g] = _dot(hid, w2k_ref[...]).astype(BF16)
        else:
            vacc = vacc + _dot(hid, w2v_ref[g])
    vc_ref[0] = vacc.astype(BF16)


def _compress(r4, w1, pe, w2k, w2v_pad, B, NR):
    return pl.pallas_call(
        functools.partial(_compress_kernel, NR=NR),
        grid=(B,),
        in_specs=[pl.BlockSpec((1, 2 * NSA_GROUPS, NR, CMP_STRIDE * HEAD_DIM), lambda b: (b, 0, 0, 0)),
                  pl.BlockSpec((2, CMP_BLOCK * HEAD_DIM, CMP_HIDDEN), lambda b: (0, 0, 0)),
                  pl.BlockSpec((2, 16, CMP_BLOCK * HEAD_DIM), lambda b: (0, 0, 0)),
                  pl.BlockSpec((CMP_HIDDEN, HEAD_DIM), lambda b: (0, 0)),
                  pl.BlockSpec((NSA_GROUPS, CMP_HIDDEN, LANES), lambda b: (0, 0, 0))],
        out_specs=[pl.BlockSpec((1, NSA_GROUPS, NR, HEAD_DIM), lambda b: (b, 0, 0, 0)),
                   pl.BlockSpec((1, NR, LANES), lambda b: (b, 0, 0))],
        out_shape=[jax.ShapeDtypeStruct((B, NSA_GROUPS, NR, HEAD_DIM), BF16),
                   jax.ShapeDtypeStruct((B, NR, LANES), BF16)],
        compiler_params=pltpu.CompilerParams(dimension_semantics=("parallel",), vmem_limit_bytes=VMEM_LIMIT),
        name="compress",
    )(r4, w1, pe, w2k, w2v_pad)


def _cmp_kernel(q_ref, kc_ref, vc_ref, bt_ref, ocmp_ref, sel_ref, bias_sc, val_sc, *, tq, NR):
    i = pl.program_id(0)
    b = pl.program_id(1)
    t0 = i * tq
    n_slc = LANES // 2

    @pl.when(b == 0)
    def _():
        f = lax.broadcasted_iota(jnp.int32, (LANES, NR), 0)
        c = lax.broadcasted_iota(jnp.int32, (LANES, NR), 1)
        place = (((f % CMP_BAND) == (c - t0 // CMP_STRIDE + CMP_BAND // 2)) & (f < 3 * CMP_BAND))
        place = jnp.where(place, 1.0, 0.0).astype(BF16)
        for h in range(NSA_HEADS):
            bias_sc[h] = _dot(bt_ref[h], place)

    t = t0 + lax.broadcasted_iota(jnp.int32, (tq, 1), 0)
    c = lax.broadcasted_iota(jnp.int32, (1, NR), 1)
    cmask = (c * CMP_STRIDE + (CMP_BLOCK - 1)) <= t
    lane = lax.broadcasted_iota(jnp.int32, (1, LANES), 1)

    jj = lax.broadcasted_iota(jnp.int32, (LANES, NR), 0)
    cc = lax.broadcasted_iota(jnp.int32, (LANES, NR), 1)
    ov = ((cc * CMP_STRIDE < jj * SLC_BLOCK + SLC_BLOCK) & (cc * CMP_STRIDE + CMP_BLOCK > jj * SLC_BLOCK)
          & (jj < n_slc) & (cc < NR - 1))
    ov = jnp.where(ov, 1.0, 0.0).astype(BF16)

    jrow = lax.broadcasted_iota(jnp.int32, (n_slc, tq), 0)
    tt = t0 + lax.broadcasted_iota(jnp.int32, (n_slc, tq), 1)
    cur = tt // SLC_BLOCK
    forced = (jrow == 0) | (jrow == cur) | (jrow == cur - 1)
    valid = jrow * SLC_BLOCK <= tt
    jcol = lax.broadcasted_iota(jnp.int32, (n_slc, 1), 0)

    outs = []
    for g in range(NSA_GROUPS):
        qs = q_ref[g * NSA_REP:(g + 1) * NSA_REP].reshape(NSA_REP * tq, HEAD_DIM)
        s = _dot_nt(qs, kc_ref[0, g]).reshape(NSA_REP, tq, NR) + bias_sc[g * NSA_REP:(g + 1) * NSA_REP]
        s = jnp.where(cmask, s, NEG_INF)
        m = jnp.max(s, axis=-1, keepdims=True)
        e = jnp.where(cmask, jnp.exp(s - m), 0.0)
        l = jnp.sum(e, axis=-1, keepdims=True)
        p = e * (1.0 / jnp.where(l > 0.0, l, 1.0))
        o = _dot(p.reshape(NSA_REP * tq, NR).astype(BF16), vc_ref[0])
        outs.append(o.reshape(NSA_REP, tq, LANES))

        hi, mid, lo = _split3(jnp.sum(p, axis=0))
        imp = (_dot_nt(ov, hi) + _dot_nt(ov, mid) + _dot_nt(ov, lo))[0:n_slc, :]
        val_sc[...] = jnp.where(forced, FORCED_SCORE, jnp.where(valid, imp, -1.0))
        val = val_sc[...]
        rank = jnp.zeros((n_slc, tq), F32)
        for k in range(n_slc):
            vk = val_sc[k:k + 1, :]
            tie = jnp.where(jcol > k, 1.0, 0.0)
            rank = rank + jnp.where(vk > val, 1.0, jnp.where(vk == val, tie, 0.0))
        selneg = jnp.where(rank < float(SLC_TOPK), 0.0, NEG_INF)
        selneg = jnp.concatenate([selneg, jnp.zeros((LANES - n_slc, tq), F32)], axis=0)
        sel_ref[g] = selneg.T.astype(BF16)

    for r in range(NSA_REP):
        ocmp_ref[:, r * LANES:(r + 1) * LANES] = jnp.where(lane < HEAD_DIM, outs[0][r], outs[1][r])


def _nsa_cmp(nq, kc, vc, bt, B, T, NR):
    tq = NSA_TILE
    nT = T // tq
    M = B * T
    return pl.pallas_call(
        functools.partial(_cmp_kernel, tq=tq, NR=NR),
        grid=(nT, B),
        in_specs=[pl.BlockSpec((NSA_HEADS, tq, HEAD_DIM), lambda i, b: (0, b * nT + i, 0)),
                  pl.BlockSpec((1, NSA_GROUPS, NR, HEAD_DIM), lambda i, b: (b, 0, 0, 0)),
                  pl.BlockSpec((1, NR, LANES), lambda i, b: (b, 0, 0)),
                  pl.BlockSpec((NSA_HEADS, tq, LANES), lambda i, b: (0, 0, 0))],
        out_specs=[pl.BlockSpec((tq, NSA_REP * LANES), lambda i, b: (b * nT + i, 0)),
                   pl.BlockSpec((NSA_GROUPS, tq, LANES), lambda i, b: (0, b * nT + i, 0))],
        out_shape=[jax.ShapeDtypeStruct((M, NSA_REP * LANES), F32),
                   jax.ShapeDtypeStruct((NSA_GROUPS, M, LANES), BF16)],
        scratch_shapes=[pltpu.VMEM((NSA_HEADS, tq, NR), F32),
                        pltpu.VMEM((LANES // 2, tq), F32)],
        compiler_params=pltpu.CompilerParams(dimension_semantics=("arbitrary", "arbitrary"),
                                             vmem_limit_bytes=VMEM_LIMIT),
        name="nsa_cmp",
    )(nq, kc, vc, bt)


def _nsa_kernel(q_ref, k_ref, v_ref, sel_ref, ocmp_ref, misc_ref, bias_ref, o_ref, *, tq):
    i = pl.program_id(1)
    tk = tq
    rq = NSA_REP * tq
    gates = jax.nn.sigmoid(misc_ref[...])
    lane = lax.broadcasted_iota(jnp.int32, (1, LANES), 1)
    rows = lax.broadcasted_iota(jnp.int32, (tq, tk), 0)
    cols = lax.broadcasted_iota(jnp.int32, (tq, tk), 1)
    causal = cols <= rows
    ej = lax.broadcasted_iota(jnp.int32, (LANES, tk), 0)
    ec = lax.broadcasted_iota(jnp.int32, (LANES, tk), 1)

    def gate_col(h, branch):
        c0 = FOX_HEADS + 3 * h + branch
        return gates[:, c0:c0 + 1]

    def update(carry, qs, k, v, bias):
        m, l, acc = carry
        s = _dot_nt(qs, k).reshape(NSA_REP, tq, tk) + bias
        m_new = jnp.maximum(m, jnp.max(s, axis=-1, keepdims=True))
        p = jnp.exp(s - m_new)
        alpha = jnp.exp(m - m_new)
        l = alpha * l + jnp.sum(p, axis=-1, keepdims=True)
        pv = _dot(p.reshape(rq, tk).astype(BF16), v).reshape(NSA_REP, tq, LANES)
        return m_new, l, alpha * acc + pv

    def init():
        return (jnp.full((NSA_REP, tq, 1), NEG_INF, F32), jnp.zeros((NSA_REP, tq, 1), F32),
                jnp.zeros((NSA_REP, tq, LANES), F32))

    def kv(kidx, vcol, j):
        ks = pl.multiple_of(j * tk, tk)
        return k_ref[kidx, pl.ds(ks, tk), :], v_ref[pl.ds(ks, tk), vcol * LANES:(vcol + 1) * LANES]

    combs = []
    for g in range(NSA_GROUPS):
        qs = q_ref[g * NSA_REP:(g + 1) * NSA_REP].reshape(rq, HEAD_DIM)
        selg = sel_ref[g]
        hs = slice(g * NSA_REP, (g + 1) * NSA_REP)

        def sel_bias(j, selg=selg):
            expand = jnp.where((ec + j * tk) // SLC_BLOCK == ej, 1.0, 0.0).astype(BF16)
            return _dot(selg, expand)

        def far(j, carry, qs=qs, g=g, sel_bias=sel_bias):
            k, v = kv(g, 0, j)
            return update(carry, qs, k, v, sel_bias(j))

        cs = lax.fori_loop(0, jnp.maximum(i - 1, 0), far, init())
        jp = jnp.maximum(i - 1, 0)
        k, v = kv(g, 0, jp)
        prev_bias = jnp.where(i >= 1, sel_bias(jp) + bias_ref[hs, 1], NEG_INF)
        cs = update(cs, qs, k, v, prev_bias)
        k, v = kv(g, 0, i)
        cs = update(cs, qs, k, v, jnp.where(causal, sel_bias(i) + bias_ref[hs, 0], NEG_INF))

        cw = init()
        jf = jnp.maximum(i - 2, 0)
        k, v = kv(NSA_GROUPS + g, 1, jf)
        cw = update(cw, qs, k, v, jnp.where(cols > rows, jnp.where(i >= 2, 0.0, NEG_INF), NEG_INF))
        k, v = kv(NSA_GROUPS + g, 1, jp)
        cw = update(cw, qs, k, v, jnp.where(i >= 1, bias_ref[hs, 1], NEG_INF))
        k, v = kv(NSA_GROUPS + g, 1, i)
        cw = update(cw, qs, k, v, jnp.where(causal, bias_ref[hs, 0], NEG_INF))

        comb = []
        for r in range(NSA_REP):
            h = g * NSA_REP + r
            comb.append(cs[2][r] * (gate_col(h, 1) / cs[1][r]) + cw[2][r] * (gate_col(h, 2) / cw[1][r]))
        combs.append(comb)

    for r in range(NSA_REP):
        gc = jnp.where(lane < HEAD_DIM, gate_col(r, 0), gate_col(NSA_REP + r, 0))
        o = jnp.where(lane < HEAD_DIM, combs[0][r], combs[1][r]) + gc * ocmp_ref[:, r * LANES:(r + 1) * LANES]
        o_ref[:, r * LANES:(r + 1) * LANES] = o.astype(BF16)


def _nsa_attention(nq, nk, nv, sel, ocmp, misc, bias_t, B, T):
    tq = NSA_TILE
    assert WINDOW == 2 * tq
    nT = T // tq
    M = B * T
    return pl.pallas_call(
        functools.partial(_nsa_kernel, tq=tq),
        grid=(B, nT),
        in_specs=[pl.BlockSpec((NSA_HEADS, tq, HEAD_DIM), lambda b, i: (0, b * nT + i, 0)),
                  pl.BlockSpec((2 * NSA_GROUPS, T, HEAD_DIM), lambda b, i: (0, b, 0)),
                  pl.BlockSpec((T, 2 * KV_W), lambda b, i: (b, 0)),
                  pl.BlockSpec((NSA_GROUPS, tq, LANES), lambda b, i: (0, b * nT + i, 0)),
                  pl.BlockSpec((tq, NSA_REP * LANES), lambda b, i: (b * nT + i, 0)),
                  pl.BlockSpec((tq, LANES), lambda b, i: (b * nT + i, 0)),
                  pl.BlockSpec((NSA_HEADS, 2, tq, tq), lambda b, i: (0, 0, 0, 0))],
        out_specs=pl.BlockSpec((tq, NSA_REP * LANES), lambda b, i: (b * nT + i, 0)),
        out_shape=jax.ShapeDtypeStruct((M, NSA_W), BF16),
        compiler_params=pltpu.CompilerParams(dimension_semantics=("parallel", "arbitrary"),
                                             vmem_limit_bytes=VMEM_LIMIT),
        name="nsa_attn",
    )(nq, nk, nv, sel, ocmp, misc, bias_t)


def _merge_kernel(x_ref, g_ref, of_ref, on_ref, wga_ref, wgb_ref, wbf_ref, wbn_ref, wo_ref, x1_ref):
    x = x_ref[...]
    h = _rms(x, g_ref[...]).astype(BF16)
    ga = jax.nn.sigmoid(_dot(h, wga_ref[...]))
    gb = jax.nn.sigmoid(_dot(h, wgb_ref[...]))
    merged = ga * _dot(of_ref[...], wbf_ref[...]) + gb * _dot(on_ref[...], wbn_ref[...])
    x1_ref[...] = x + _dot(merged.astype(BF16), wo_ref[...])


def _merge(x2, g, o_fox, o_nsa, wga, wgb, wbf, wbn, wo, tm):
    M = x2.shape[0]
    row = lambda i: (i, 0)
    full = lambda i: (0, 0)
    return pl.pallas_call(
        _merge_kernel,
        grid=(M // tm,),
        in_specs=[pl.BlockSpec((tm, D_MODEL), row),
                  pl.BlockSpec((1, D_MODEL), full),
                  pl.BlockSpec((tm, FOX_W), row),
                  pl.BlockSpec((tm, NSA_W), row),
                  pl.BlockSpec((D_MODEL, D_MODEL), full),
                  pl.BlockSpec((D_MODEL, D_MODEL), full),
                  pl.BlockSpec((FOX_W, D_MODEL), full),
                  pl.BlockSpec((NSA_W, D_MODEL), full),
                  pl.BlockSpec((D_MODEL, D_MODEL), full)],
        out_specs=pl.BlockSpec((tm, D_MODEL), row),
        out_shape=jax.ShapeDtypeStruct((M, D_MODEL), F32),
        compiler_params=pltpu.CompilerParams(dimension_semantics=("parallel",), vmem_limit_bytes=VMEM_LIMIT),
        name="merge",
    )(x2, g, o_fox, o_nsa, wga, wgb, wbf, wbn, wo)


def _mlp_kernel(x_ref, g_ref, w1_ref, w2_ref, gf_ref, o_ref, *, final):
    x = x_ref[...]
    h = _rms(x, g_ref[...]).astype(BF16)
    u = jnp.maximum(_dot(h, w1_ref[...]), 0.0)
    y = x + _dot((u * u).astype(BF16), w2_ref[...])
    o_ref[...] = _rms(y, gf_ref[...]) if final else y


def _mlp(x1, g, w1, w2, gf, tm, final):
    M = x1.shape[0]
    row = lambda i: (i, 0)
    full = lambda i: (0, 0)
    return pl.pallas_call(
        functools.partial(_mlp_kernel, final=final),
        grid=(M // tm,),
        in_specs=[pl.BlockSpec((tm, D_MODEL), row),
                  pl.BlockSpec((1, D_MODEL), full),
                  pl.BlockSpec((D_MODEL, D_FF), full),
                  pl.BlockSpec((D_FF, D_MODEL), full),
                  pl.BlockSpec((1, D_MODEL), full)],
        out_specs=pl.BlockSpec((tm, D_MODEL), row),
        out_shape=jax.ShapeDtypeStruct((M, D_MODEL), F32),
        compiler_params=pltpu.CompilerParams(dimension_semantics=("parallel",), vmem_limit_bytes=VMEM_LIMIT),
        name="mlp",
    )(x1, g, w1, w2, gf)


def _bias_tables(tq):
    q = np.arange(tq)[:, None]
    k = np.arange(tq)[None, :]
    idx = np.stack([_rel_bucket_np(np.maximum(q - k, 0)), _rel_bucket_np(q - k + tq)])
    f = np.arange(CMP_BAND)[None, :]
    d = q - CMP_STRIDE * (f - CMP_BAND // 2) - (CMP_BLOCK - 1)
    cidx = _rel_bucket_np(np.maximum(d, 0))
    cidx = np.where(d >= 0, cidx, REL_BUCKETS - 1)
    return idx, cidx


def _layer(x2, B, T, rel_tbl, g_attn, w_in, b_forget, pe_k, w1_k, w2_k, pe_v, w1_v, w2_v,
           w_br_fox, w_br_nsa, w_out, g_mlp, w_ff1, w_ff2, g_final, final):
    M = B * T
    NR = T // CMP_STRIDE
    offs = np.cumsum((FOX_W, FOX_W, FOX_W, FOX_HEADS, NSA_W, KV_W, KV_W, KV_W, KV_W, KV_W, KV_W,
                      3 * NSA_HEADS, D_MODEL, D_MODEL))
    (w_fq, w_fk, w_fv, w_fl, w_nq, w_kc, w_vc, w_ksl, w_vsl, w_kwn, w_vwn, w_ng, w_ga, w_gb) = jnp.split(
        w_in, offs[:-1].tolist(), axis=-1)
    w_all = jnp.concatenate(
        [w_fq * SCALE, w_fk, w_nq * SCALE, w_ksl, w_kwn, w_fv, w_vsl, w_vwn, w_kc, w_vc, w_fl, w_ng,
         jnp.zeros((D_MODEL, _W_ALL - _SEG_MISC - FOX_HEADS - 3 * NSA_HEADS), F32)], axis=-1).astype(BF16)

    fq, fk, nq, nk, fv, nv, ckv, misc = _in_proj(x2, g_attn.reshape(1, D_MODEL), w_all, tm=512)

    bf_pad = jnp.zeros((1, LANES), F32).at[0, :FOX_HEADS].set(b_forget)
    crow = _fox_decay(misc, bf_pad, B, T)
    o_fox = _fox_attention(fq, fk, fv, crow, B, T, tq=512, tk=512)

    r4 = ckv.reshape(B, NR, CMP_STRIDE, 2 * NSA_GROUPS, HEAD_DIM).transpose(0, 3, 1, 2, 4)
    r4 = r4.reshape(B, 2 * NSA_GROUPS, NR, CMP_STRIDE * HEAD_DIM)
    w1 = jnp.stack([w1_k, w1_v]).astype(BF16)
    pe = jnp.stack([pe_k.reshape(1, -1), pe_v.reshape(1, -1)]).astype(BF16)
    pe = jnp.broadcast_to(pe, (2, 16, CMP_BLOCK * HEAD_DIM))
    zpad = jnp.zeros((CMP_HIDDEN, HEAD_DIM), F32)
    w2v_pad = jnp.stack([jnp.concatenate([w2_v, zpad], axis=1), jnp.concatenate([zpad, w2_v], axis=1)]).astype(BF16)
    kc, vc = _compress(r4, w1, pe, w2_k.astype(BF16), w2v_pad, B, NR)

    idx, cidx = _bias_tables(NSA_TILE)
    tbl = (rel_tbl - rel_tbl[REL_BUCKETS - 1]).T
    bias_t = tbl[:, idx]
    hi, mid, lo = _split3(tbl[:, cidx])
    bt = jnp.concatenate([hi, mid, lo, jnp.zeros_like(hi)], axis=-1)

    ocmp, sel = _nsa_cmp(nq, kc, vc, bt, B, T, NR)
    o_nsa = _nsa_attention(nq, nk, nv, sel, ocmp, misc, bias_t, B, T)

    wbn = w_br_nsa.reshape(NSA_GROUPS, NSA_REP, HEAD_DIM, D_MODEL).transpose(1, 0, 2, 3).reshape(NSA_W, D_MODEL)
    x1 = _merge(x2, g_attn.reshape(1, D_MODEL), o_fox, o_nsa, w_ga.astype(BF16), w_gb.astype(BF16),
                w_br_fox.astype(BF16), wbn.astype(BF16), w_out.astype(BF16), tm=512)
    return _mlp(x1, g_mlp.reshape(1, D_MODEL), w_ff1.astype(BF16), w_ff2.astype(BF16),
                g_final.reshape(1, D_MODEL), tm=256, final=final)


def kernel(x, rel_bias_table, g_attn, w_in, b_forget, cmp_pe_k, cmp_w1_k, cmp_w2_k, cmp_pe_v, cmp_w1_v, cmp_w2_v,
           w_br_fox, w_br_nsa, w_out, g_mlp, w_ff1, w_ff2, g_final):
    B, T, _ = x.shape
    depth = g_attn.shape[0]
    x2 = x.reshape(B * T, D_MODEL)
    for l in range(depth):
        x2 = _layer(x2, B, T, rel_bias_table, g_attn[l], w_in[l], b_forget[l], cmp_pe_k[l], cmp_w1_k[l],
                    cmp_w2_k[l], cmp_pe_v[l], cmp_w1_v[l], cmp_w2_v[l], w_br_fox[l], w_br_nsa[l], w_out[l],
                    g_mlp[l], w_ff1[l], w_ff2[l], g_final, final=(l == depth - 1))
    return x2.reshape(B, T, D_MODEL)
```

```python
import functools
import math

import numpy as np
import jax
import jax.numpy as jnp
from jax import lax
from jax.experimental import pallas as pl
from jax.experimental.pallas import tpu as pltpu

F32 = jnp.float32
BF16 = jnp.bfloat16

D_MODEL = 1024
HEAD_DIM = 64
FOX_HEADS = 8
NSA_HEADS = 8
NSA_GROUPS = 2
NSA_REP = NSA_HEADS // NSA_GROUPS
CMP_BLOCK = 32
CMP_STRIDE = 16
CMP_HIDDEN = 256
SLC_BLOCK = 64
SLC_TOPK = 16
WINDOW = 512
REL_BUCKETS = 32
REL_MAX_DIST = 128
D_FF = 4 * D_MODEL
RMS_EPS = 1e-6
NEG_INF = -1e30
FORCED_SCORE = 1e4
SCALE = HEAD_DIM ** -0.5
LOG2E = math.log2(math.e)

FOX_W = FOX_HEADS * HEAD_DIM
NSA_W = NSA_HEADS * HEAD_DIM
KV_W = NSA_GROUPS * HEAD_DIM
LANES = 128
NSA_TILE = 256
REL_FAR = 113
CMP_BAND = 32

VMEM_LIMIT = 56 * 1024 * 1024

_NT = (((1,), (1,)), ((), ()))


def _dot(a, b):
    return jnp.dot(a, b, preferred_element_type=F32)


def _dot_nt(a, b):
    return lax.dot_general(a, b, _NT, preferred_element_type=F32)


def _split3(x):
    hi = x.astype(BF16)
    r1 = x - hi.astype(F32)
    mid = r1.astype(BF16)
    lo = (r1 - mid.astype(F32)).astype(BF16)
    return hi, mid, lo


def _rms(x, g):
    return x * lax.rsqrt(jnp.mean(x * x, axis=-1, keepdims=True) + RMS_EPS) * g


def _rel_bucket_np(n):
    exact = REL_BUCKETS // 2
    nf = np.maximum(n, exact).astype(np.float64)
    log_b = exact + (np.log(nf / exact) / math.log(REL_MAX_DIST / exact) * (REL_BUCKETS - exact)).astype(np.int64)
    return np.where(n < exact, n, np.minimum(log_b, REL_BUCKETS - 1))


_SEG_FQ, _SEG_FK, _SEG_NQ, _SEG_NK, _SEG_FV, _SEG_NV, _SEG_CKV, _SEG_MISC = (
    0, 512, 1024, 1536, 1792, 2304, 2560, 2816)
_W_ALL = 2944


def _inproj_kernel(x_ref, g_ref, w_ref, fq_ref, fk_ref, nq_ref, nk_ref, fv_ref, nv_ref, ckv_ref, misc_ref):
    h = _rms(x_ref[...], g_ref[...]).astype(BF16)

    def seg(a, n):
        return _dot(h, w_ref[:, a:a + n])

    def heads(ref, a, n_heads):
        r = seg(a, n_heads * HEAD_DIM)
        for j in range(n_heads):
            ref[j] = r[:, j * HEAD_DIM:(j + 1) * HEAD_DIM].astype(BF16)

    lane = lax.broadcasted_iota(jnp.int32, (1, LANES), 1)

    def heads_wide(ref, a, n_heads, mult, upper):
        r = seg(a, n_heads * HEAD_DIM)
        for j in range(n_heads):
            pair = r[:, (j // 2) * LANES:(j // 2 + 1) * LANES]
            if j % 2:
                pair = pltpu.roll(pair, HEAD_DIM, 1)
            ref[j] = jnp.where(lane < HEAD_DIM, pair * mult, upper).astype(BF16)

    q_upper = jnp.where(lane < HEAD_DIM + 3, -1.0, 0.0)
    heads_wide(fq_ref, _SEG_FQ, FOX_HEADS, LOG2E, q_upper)
    heads_wide(fk_ref, _SEG_FK, FOX_HEADS, 1.0, 0.0)
    heads(nq_ref, _SEG_NQ, NSA_HEADS)
    heads(nk_ref, _SEG_NK, 2 * NSA_GROUPS)
    fv_ref[...] = seg(_SEG_FV, FOX_W).astype(BF16)
    nv_ref[...] = seg(_SEG_NV, 2 * KV_W).astype(BF16)
    ckv_ref[...] = seg(_SEG_CKV, 2 * KV_W).astype(BF16)
    misc_ref[...] = seg(_SEG_MISC, LANES)


def _in_proj(x2, g, w_all, tm):
    M = x2.shape[0]
    row = lambda i: (i, 0)
    hrow = lambda i: (0, i, 0)
    return pl.pallas_call(
        _inproj_kernel,
        grid=(M // tm,),
        in_specs=[pl.BlockSpec((tm, D_MODEL), row),
                  pl.BlockSpec((1, D_MODEL), lambda i: (0, 0)),
                  pl.BlockSpec((D_MODEL, _W_ALL), lambda i: (0, 0))],
        out_specs=[pl.BlockSpec((FOX_HEADS, tm, LANES), hrow),
                   pl.BlockSpec((FOX_HEADS, tm, LANES), hrow),
                   pl.BlockSpec((NSA_HEADS, tm, HEAD_DIM), hrow),
                   pl.BlockSpec((2 * NSA_GROUPS, tm, HEAD_DIM), hrow),
                   pl.BlockSpec((tm, FOX_W), row),
                   pl.BlockSpec((tm, 2 * KV_W), row),
                   pl.BlockSpec((tm, 2 * KV_W), row),
                   pl.BlockSpec((tm, LANES), row)],
        out_shape=[jax.ShapeDtypeStruct((FOX_HEADS, M, LANES), BF16),
                   jax.ShapeDtypeStruct((FOX_HEADS, M, LANES), BF16),
                   jax.ShapeDtypeStruct((NSA_HEADS, M, HEAD_DIM), BF16),
                   jax.ShapeDtypeStruct((2 * NSA_GROUPS, M, HEAD_DIM), BF16),
                   jax.ShapeDtypeStruct((M, FOX_W), BF16),
                   jax.ShapeDtypeStruct((M, 2 * KV_W), BF16),
                   jax.ShapeDtypeStruct((M, 2 * KV_W), BF16),
                   jax.ShapeDtypeStruct((M, LANES), F32)],
        compiler_params=pltpu.CompilerParams(dimension_semantics=("parallel",), vmem_limit_bytes=VMEM_LIMIT),
        name="in_proj",
    )(x2, g, w_all)


_SCAN_BLK = 256


def _decay_kernel(misc_ref, bf_ref, ccol_ref, *, T):
    r = lax.broadcasted_iota(jnp.int32, (_SCAN_BLK, _SCAN_BLK), 0)
    c = lax.broadcasted_iota(jnp.int32, (_SCAN_BLK, _SCAN_BLK), 1)
    tri = jnp.where(r >= c, 1.0, 0.0).astype(BF16)

    def blk(n, carry):
        s0 = pl.multiple_of(n * _SCAN_BLK, _SCAN_BLK)
        x = misc_ref[pl.ds(s0, _SCAN_BLK), :] + bf_ref[...]
        lf = jnp.minimum(x, 0.0) - jnp.log1p(jnp.exp(-jnp.abs(x)))
        hi, mid, lo = _split3(lf)
        cs = _dot(tri, hi) + _dot(tri, mid) + _dot(tri, lo) + carry
        ccol_ref[pl.ds(s0, _SCAN_BLK), :] = cs
        return cs[_SCAN_BLK - 1:_SCAN_BLK, :]

    lax.fori_loop(0, T // _SCAN_BLK, blk, jnp.zeros((1, LANES), F32))


def _fox_decay(misc, bf_pad, B, T):
    return pl.pallas_call(
        functools.partial(_decay_kernel, T=T),
        grid=(B,),
        in_specs=[pl.BlockSpec((T, LANES), lambda b: (b, 0)),
                  pl.BlockSpec((1, LANES), lambda b: (0, 0))],
        out_specs=pl.BlockSpec((T, LANES), lambda b: (b, 0)),
        out_shape=jax.ShapeDtypeStruct((B * T, LANES), F32),
        compiler_params=pltpu.CompilerParams(dimension_semantics=("parallel",), vmem_limit_bytes=VMEM_LIMIT),
        name="fox_decay",
    )(misc, bf_pad)


_FOX_PREP_BLK = 512


def _fox_kernel(q_ref, k_ref, vt_ref, c_ref, o_ref, kaug_sc, s_sc, *, tq, T):
    tk = tq
    hp = pl.program_id(1)
    i = pl.program_id(2)
    lane = lax.broadcasted_iota(jnp.int32, (1, LANES), 1)

    @pl.when(i == 0)
    def _():
        rr = lax.broadcasted_iota(jnp.int32, (LANES, LANES), 0)
        cc = lax.broadcasted_iota(jnp.int32, (LANES, LANES), 1)
        for hh in range(2):
            h = 2 * hp + hh
            places = [jnp.where((rr == h) & (cc == HEAD_DIM + e), 1.0, 0.0).astype(BF16) for e in range(3)]

            def prep(n, _, hh=hh, places=places):
                r0 = pl.multiple_of(n * _FOX_PREP_BLK, _FOX_PREP_BLK)
                terms = _split3(c_ref[pl.ds(r0, _FOX_PREP_BLK), :] * LOG2E)
                placed = sum(_dot(t, pm) for t, pm in zip(terms, places))
                kaug_sc[hh, pl.ds(r0, _FOX_PREP_BLK), :] = jnp.where(
                    lane < HEAD_DIM, k_ref[hh, pl.ds(r0, _FOX_PREP_BLK), :], placed.astype(BF16))
                return 0

            lax.fori_loop(0, T // _FOX_PREP_BLK, prep, 0)

    keys = lax.broadcasted_iota(jnp.int32, (tk, tq), 0)
    queries = lax.broadcasted_iota(jnp.int32, (tk, tq), 1)
    outs = []
    for hh in range(2):
        q = q_ref[hh]

        def scores(j, hh=hh, q=q):
            ks = pl.multiple_of(j * tk, tk)
            return _dot_nt(kaug_sc[hh, pl.ds(ks, tk), :], q)

        def update(j, s, carry, masked, hh=hh):
            m, l, acc = carry
            if masked:
                s = jnp.where(keys <= queries, s, NEG_INF)
            m_new = jnp.maximum(m, jnp.max(s, axis=0, keepdims=True))
            p = jnp.exp2(s - m_new)
            alpha = jnp.exp2(m - m_new)
            l = alpha * l + jnp.sum(p, axis=0, keepdims=True)
            ks = pl.multiple_of(j * tk, tk)
            vt = vt_ref[hh * HEAD_DIM:(hh + 1) * HEAD_DIM, pl.ds(ks, tk)]
            acc = alpha * acc + _dot(vt, p.astype(BF16))
            return m_new, l, acc

        def pair(jj, carry, scores=scores, update=update):
            j0 = 2 * jj
            s_sc[1] = scores(j0 + 1)
            carry = update(j0, s_sc[0], carry, False)
            s_sc[0] = scores(j0 + 2)
            return update(j0 + 1, s_sc[1], carry, False)

        def tail_even(carry, update=update):
            return update(i, s_sc[0], carry, True)

        def tail_odd(carry, scores=scores, update=update):
            s_sc[1] = scores(i)
            carry = update(i - 1, s_sc[0], carry, False)
            return update(i, s_sc[1], carry, True)

        s_sc[0] = scores(0)
        carry = (jnp.full((1, tq), NEG_INF, F32), jnp.zeros((1, tq), F32), jnp.zeros((HEAD_DIM, tq), F32))
        carry = lax.fori_loop(0, i // 2, pair, carry)
        m, l, acc = lax.cond(i % 2 == 1, tail_odd, tail_even, carry)
        outs.append(acc * (1.0 / l))
    o_ref[...] = jnp.concatenate(outs, axis=0).T.astype(BF16)


def _fox_attention(fq, fk, fvt, ccol, B, T, tq):
    nT = T // tq
    return pl.pallas_call(
        functools.partial(_fox_kernel, tq=tq, T=T),
        grid=(B, FOX_HEADS // 2, nT),
        in_specs=[pl.BlockSpec((2, tq, LANES), lambda b, hp, i: (hp, b * nT + i, 0)),
                  pl.BlockSpec((2, T, LANES), lambda b, hp, i: (hp, b, 0)),
                  pl.BlockSpec((LANES, T), lambda b, hp, i: (hp, b)),
                  pl.BlockSpec((T, LANES), lambda b, hp, i: (b, 0))],
        out_specs=pl.BlockSpec((tq, LANES), lambda b, hp, i: (b * nT + i, hp)),
        out_shape=jax.ShapeDtypeStruct((B * T, FOX_W), BF16),
        scratch_shapes=[pltpu.VMEM((2, T, LANES), BF16),
                        pltpu.VMEM((2, tq, tq), F32)],
        compiler_params=pltpu.CompilerParams(dimension_semantics=("parallel", "parallel", "arbitrary"),
                                             vmem_limit_bytes=VMEM_LIMIT),
        name="fox_attn",
    )(fq, fk, fvt, ccol)


def _compress_kernel(r_ref, w1_ref, pe_ref, w2k_ref, w2v_ref, kc_ref, vc_ref, *, NR):
    half = CMP_STRIDE * HEAD_DIM
    vacc = jnp.zeros((NR, LANES), F32)
    for idx in range(2 * NSA_GROUPS):
        kind, g = divmod(idx, NSA_GROUPS)
        rm = r_ref[0, idx]
        a = _dot(rm, w1_ref[kind, 0:half, :])
        bm = _dot(rm, w1_ref[kind, half:2 * half, :])
        pe_term = _dot(pe_ref[kind], w1_ref[kind])[0:1, :]
        pre = a + pltpu.roll(bm, NR - 1, 0) + pe_term
        hid = (pre * jax.nn.sigmoid(pre)).astype(BF16)
        if kind == 0:
            kc_ref[0, g] = _dot(hid, w2k_ref[...]).astype(BF16)
        else:
            vacc = vacc + _dot(hid, w2v_ref[g])
    vc_ref[0] = vacc.astype(BF16)


def _compress(r4, w1, pe, w2k, w2v_pad, B, NR):
    return pl.pallas_call(
        functools.partial(_compress_kernel, NR=NR),
        grid=(B,),
        in_specs=[pl.BlockSpec((1, 2 * NSA_GROUPS, NR, CMP_STRIDE * HEAD_DIM), lambda b: (b, 0, 0, 0)),
                  pl.BlockSpec((2, CMP_BLOCK * HEAD_DIM, CMP_HIDDEN), lambda b: (0, 0, 0)),
                  pl.BlockSpec((2, 16, CMP_BLOCK * HEAD_DIM), lambda b: (0, 0, 0)),
                  pl.BlockSpec((CMP_HIDDEN, HEAD_DIM), lambda b: (0, 0)),
                  pl.BlockSpec((NSA_GROUPS, CMP_HIDDEN, LANES), lambda b: (0, 0, 0))],
        out_specs=[pl.BlockSpec((1, NSA_GROUPS, NR, HEAD_DIM), lambda b: (b, 0, 0, 0)),
                   pl.BlockSpec((1, NR, LANES), lambda b: (b, 0, 0))],
        out_shape=[jax.ShapeDtypeStruct((B, NSA_GROUPS, NR, HEAD_DIM), BF16),
                   jax.ShapeDtypeStruct((B, NR, LANES), BF16)],
        compiler_params=pltpu.CompilerParams(dimension_semantics=("parallel",), vmem_limit_bytes=VMEM_LIMIT),
        name="compress",
    )(r4, w1, pe, w2k, w2v_pad)


def _cmp_kernel(q_ref, kc_ref, vc_ref, bt_ref, ocmp_ref, sel_ref, bias_sc, val_sc, *, tq, NR):
    i = pl.program_id(0)
    b = pl.program_id(1)
    t0 = i * tq
    n_slc = LANES // 2

    @pl.when(b == 0)
    def _():
        f = lax.broadcasted_iota(jnp.int32, (LANES, NR), 0)
        c = lax.broadcasted_iota(jnp.int32, (LANES, NR), 1)
        place = (((f % CMP_BAND) == (c - t0 // CMP_STRIDE + CMP_BAND // 2)) & (f < 3 * CMP_BAND))
        place = jnp.where(place, 1.0, 0.0).astype(BF16)
        for h in range(NSA_HEADS):
            bias_sc[h] = _dot(bt_ref[h], place)

    t = t0 + lax.broadcasted_iota(jnp.int32, (tq, 1), 0)
    c = lax.broadcasted_iota(jnp.int32, (1, NR), 1)
    cmask = (c * CMP_STRIDE + (CMP_BLOCK - 1)) <= t
    lane = lax.broadcasted_iota(jnp.int32, (1, LANES), 1)

    jj = lax.broadcasted_iota(jnp.int32, (LANES, NR), 0)
    cc = lax.broadcasted_iota(jnp.int32, (LANES, NR), 1)
    ov = ((cc * CMP_STRIDE < jj * SLC_BLOCK + SLC_BLOCK) & (cc * CMP_STRIDE + CMP_BLOCK > jj * SLC_BLOCK)
          & (jj < n_slc) & (cc < NR - 1))
    ov = jnp.where(ov, 1.0, 0.0).astype(BF16)

    jrow = lax.broadcasted_iota(jnp.int32, (n_slc, tq), 0)
    tt = t0 + lax.broadcasted_iota(jnp.int32, (n_slc, tq), 1)
    cur = tt // SLC_BLOCK
    forced = (jrow == 0) | (jrow == cur) | (jrow == cur - 1)
    valid = jrow * SLC_BLOCK <= tt
    jcol = lax.broadcasted_iota(jnp.int32, (n_slc, 1), 0)

    outs = []
    for g in range(NSA_GROUPS):
        qs = q_ref[g * NSA_REP:(g + 1) * NSA_REP].reshape(NSA_REP * tq, HEAD_DIM)
        s = _dot_nt(qs, kc_ref[0, g]).reshape(NSA_REP, tq, NR) + bias_sc[g * NSA_REP:(g + 1) * NSA_REP]
        s = jnp.where(cmask, s, NEG_INF)
        m = jnp.max(s, axis=-1, keepdims=True)
        e = jnp.where(cmask, jnp.exp(s - m), 0.0)
        l = jnp.sum(e, axis=-1, keepdims=True)
        p = e * (1.0 / jnp.where(l > 0.0, l, 1.0))
        o = _dot(p.reshape(NSA_REP * tq, NR).astype(BF16), vc_ref[0])
        outs.append(o.reshape(NSA_REP, tq, LANES))

        hi, mid, lo = _split3(jnp.sum(p, axis=0))
        imp = (_dot_nt(ov, hi) + _dot_nt(ov, mid) + _dot_nt(ov, lo))[0:n_slc, :]
        val_sc[...] = jnp.where(forced, FORCED_SCORE, jnp.where(valid, imp, -1.0))
        val = val_sc[...]
        rank = jnp.zeros((n_slc, tq), F32)
        for k in range(n_slc):
            vk = val_sc[k:k + 1, :]
            tie = jnp.where(jcol > k, 1.0, 0.0)
            rank = rank + jnp.where(vk > val, 1.0, jnp.where(vk == val, tie, 0.0))
        selneg = jnp.where(rank < float(SLC_TOPK), 0.0, NEG_INF)
        selneg = jnp.concatenate([selneg, jnp.zeros((LANES - n_slc, tq), F32)], axis=0)
        sel_ref[g] = selneg.T.astype(BF16)

    for r in range(NSA_REP):
        ocmp_ref[:, r * LANES:(r + 1) * LANES] = jnp.where(lane < HEAD_DIM, outs[0][r], outs[1][r])


def _nsa_cmp(nq, kc, vc, bt, B, T, NR):
    tq = NSA_TILE
    nT = T // tq
    M = B * T
    return pl.pallas_call(
        functools.partial(_cmp_kernel, tq=tq, NR=NR),
        grid=(nT, B),
        in_specs=[pl.BlockSpec((NSA_HEADS, tq, HEAD_DIM), lambda i, b: (0, b * nT + i, 0)),
                  pl.BlockSpec((1, NSA_GROUPS, NR, HEAD_DIM), lambda i, b: (b, 0, 0, 0)),
                  pl.BlockSpec((1, NR, LANES), lambda i, b: (b, 0, 0)),
                  pl.BlockSpec((NSA_HEADS, tq, LANES), lambda i, b: (0, 0, 0))],
        out_specs=[pl.BlockSpec((tq, NSA_REP * LANES), lambda i, b: (b * nT + i, 0)),
                   pl.BlockSpec((NSA_GROUPS, tq, LANES), lambda i, b: (0, b * nT + i, 0))],
        out_shape=[jax.ShapeDtypeStruct((M, NSA_REP * LANES), F32),
                   jax.ShapeDtypeStruct((NSA_GROUPS, M, LANES), BF16)],
        scratch_shapes=[pltpu.VMEM((NSA_HEADS, tq, NR), F32),
                        pltpu.VMEM((LANES // 2, tq), F32)],
        compiler_params=pltpu.CompilerParams(dimension_semantics=("arbitrary", "arbitrary"),
                                             vmem_limit_bytes=VMEM_LIMIT),
        name="nsa_cmp",
    )(nq, kc, vc, bt)


def _nsa_kernel(q_ref, k_ref, v_ref, sel_ref, ocmp_ref, misc_ref, bias_ref, o_ref, *, tq):
    i = pl.program_id(1)
    tk = tq
    rq = NSA_REP * tq
    gates = jax.nn.sigmoid(misc_ref[...])
    lane = lax.broadcasted_iota(jnp.int32, (1, LANES), 1)
    rows = lax.broadcasted_iota(jnp.int32, (tq, tk), 0)
    cols = lax.broadcasted_iota(jnp.int32, (tq, tk), 1)
    causal = cols <= rows
    ej = lax.broadcasted_iota(jnp.int32, (LANES, tk), 0)
    ec = lax.broadcasted_iota(jnp.int32, (LANES, tk), 1)

    def gate_col(h, branch):
        c0 = FOX_HEADS + 3 * h + branch
        return gates[:, c0:c0 + 1]

    def update(carry, qs, k, v, bias):
        m, l, acc = carry
        s = _dot_nt(qs, k).reshape(NSA_REP, tq, tk) + bias
        m_new = jnp.maximum(m, jnp.max(s, axis=-1, keepdims=True))
        p = jnp.exp(s - m_new)
        alpha = jnp.exp(m - m_new)
        l = alpha * l + jnp.sum(p, axis=-1, keepdims=True)
        pv = _dot(p.reshape(rq, tk).astype(BF16), v).reshape(NSA_REP, tq, LANES)
        return m_new, l, alpha * acc + pv

    def init():
        return (jnp.full((NSA_REP, tq, 1), NEG_INF, F32), jnp.zeros((NSA_REP, tq, 1), F32),
                jnp.zeros((NSA_REP, tq, LANES), F32))

    def kv(kidx, vcol, j):
        ks = pl.multiple_of(j * tk, tk)
        return k_ref[kidx, pl.ds(ks, tk), :], v_ref[pl.ds(ks, tk), vcol * LANES:(vcol + 1) * LANES]

    combs = []
    for g in range(NSA_GROUPS):
        qs = q_ref[g * NSA_REP:(g + 1) * NSA_REP].reshape(rq, HEAD_DIM)
        selg = sel_ref[g]
        hs = slice(g * NSA_REP, (g + 1) * NSA_REP)

        def sel_bias(j, selg=selg):
            expand = jnp.where((ec + j * tk) // SLC_BLOCK == ej, 1.0, 0.0).astype(BF16)
            return _dot(selg, expand)

        def far(j, carry, qs=qs, g=g, sel_bias=sel_bias):
            k, v = kv(g, 0, j)
            return update(carry, qs, k, v, sel_bias(j))

        cs = lax.fori_loop(0, jnp.maximum(i - 1, 0), far, init())
        jp = jnp.maximum(i - 1, 0)
        k, v = kv(g, 0, jp)
        prev_bias = jnp.where(i >= 1, sel_bias(jp) + bias_ref[hs, 1], NEG_INF)
        cs = update(cs, qs, k, v, prev_bias)
        k, v = kv(g, 0, i)
        cs = update(cs, qs, k, v, jnp.where(causal, sel_bias(i) + bias_ref[hs, 0], NEG_INF))

        cw = init()
        jf = jnp.maximum(i - 2, 0)
        k, v = kv(NSA_GROUPS + g, 1, jf)
        cw = update(cw, qs, k, v, jnp.where(cols > rows, jnp.where(i >= 2, 0.0, NEG_INF), NEG_INF))
        k, v = kv(NSA_GROUPS + g, 1, jp)
        cw = update(cw, qs, k, v, jnp.where(i >= 1, bias_ref[hs, 1], NEG_INF))
        k, v = kv(NSA_GROUPS + g, 1, i)
        cw = update(cw, qs, k, v, jnp.where(causal, bias_ref[hs, 0], NEG_INF))

        comb = []
        for r in range(NSA_REP):
            h = g * NSA_REP + r
            comb.append(cs[2][r] * (gate_col(h, 1) / cs[1][r]) + cw[2][r] * (gate_col(h, 2) / cw[1][r]))
        combs.append(comb)

    for r in range(NSA_REP):
        gc = jnp.where(lane < HEAD_DIM, gate_col(r, 0), gate_col(NSA_REP + r, 0))
        o = jnp.where(lane < HEAD_DIM, combs[0][r], combs[1][r]) + gc * ocmp_ref[:, r * LANES:(r + 1) * LANES]
        o_ref[:, r * LANES:(r + 1) * LANES] = o.astype(BF16)


def _nsa_attention(nq, nk, nv, sel, ocmp, misc, bias_t, B, T):
    tq = NSA_TILE
    assert WINDOW == 2 * tq
    nT = T // tq
    M = B * T
    return pl.pallas_call(
        functools.partial(_nsa_kernel, tq=tq),
        grid=(B, nT),
        in_specs=[pl.BlockSpec((NSA_HEADS, tq, HEAD_DIM), lambda b, i: (0, b * nT + i, 0)),
                  pl.BlockSpec((2 * NSA_GROUPS, T, HEAD_DIM), lambda b, i: (0, b, 0)),
                  pl.BlockSpec((T, 2 * KV_W), lambda b, i: (b, 0)),
                  pl.BlockSpec((NSA_GROUPS, tq, LANES), lambda b, i: (0, b * nT + i, 0)),
                  pl.BlockSpec((tq, NSA_REP * LANES), lambda b, i: (b * nT + i, 0)),
                  pl.BlockSpec((tq, LANES), lambda b, i: (b * nT + i, 0)),
                  pl.BlockSpec((NSA_HEADS, 2, tq, tq), lambda b, i: (0, 0, 0, 0))],
        out_specs=pl.BlockSpec((tq, NSA_REP * LANES), lambda b, i: (b * nT + i, 0)),
        out_shape=jax.ShapeDtypeStruct((M, NSA_W), BF16),
        compiler_params=pltpu.CompilerParams(dimension_semantics=("parallel", "arbitrary"),
                                             vmem_limit_bytes=VMEM_LIMIT),
        name="nsa_attn",
    )(nq, nk, nv, sel, ocmp, misc, bias_t)


def _merge_kernel(x_ref, g_ref, of_ref, on_ref, wga_ref, wgb_ref, wbf_ref, wbn_ref, wo_ref, x1_ref):
    x = x_ref[...]
    h = _rms(x, g_ref[...]).astype(BF16)
    ga = jax.nn.sigmoid(_dot(h, wga_ref[...]))
    gb = jax.nn.sigmoid(_dot(h, wgb_ref[...]))
    merged = ga * _dot(of_ref[...], wbf_ref[...]) + gb * _dot(on_ref[...], wbn_ref[...])
    x1_ref[...] = x + _dot(merged.astype(BF16), wo_ref[...])


def _merge(x2, g, o_fox, o_nsa, wga, wgb, wbf, wbn, wo, tm):
    M = x2.shape[0]
    row = lambda i: (i, 0)
    full = lambda i: (0, 0)
    return pl.pallas_call(
        _merge_kernel,
        grid=(M // tm,),
        in_specs=[pl.BlockSpec((tm, D_MODEL), row),
                  pl.BlockSpec((1, D_MODEL), full),
                  pl.BlockSpec((tm, FOX_W), row),
                  pl.BlockSpec((tm, NSA_W), row),
                  pl.BlockSpec((D_MODEL, D_MODEL), full),
                  pl.BlockSpec((D_MODEL, D_MODEL), full),
                  pl.BlockSpec((FOX_W, D_MODEL), full),
                  pl.BlockSpec((NSA_W, D_MODEL), full),
                  pl.BlockSpec((D_MODEL, D_MODEL), full)],
        out_specs=pl.BlockSpec((tm, D_MODEL), row),
        out_shape=jax.ShapeDtypeStruct((M, D_MODEL), F32),
        compiler_params=pltpu.CompilerParams(dimension_semantics=("parallel",), vmem_limit_bytes=VMEM_LIMIT),
        name="merge",
    )(x2, g, o_fox, o_nsa, wga, wgb, wbf, wbn, wo)


def _mlp_kernel(x_ref, g_ref, w1_ref, w2_ref, gf_ref, o_ref, *, final):
    x = x_ref[...]
    h = _rms(x, g_ref[...]).astype(BF16)
    u = jnp.maximum(_dot(h, w1_ref[...]), 0.0)
    y = x + _dot((u * u).astype(BF16), w2_ref[...])
    o_ref[...] = _rms(y, gf_ref[...]) if final else y


def _mlp(x1, g, w1, w2, gf, tm, final):
    M = x1.shape[0]
    row = lambda i: (i, 0)
    full = lambda i: (0, 0)
    return pl.pallas_call(
        functools.partial(_mlp_kernel, final=final),
        grid=(M // tm,),
        in_specs=[pl.BlockSpec((tm, D_MODEL), row),
                  pl.BlockSpec((1, D_MODEL), full),
                  pl.BlockSpec((D_MODEL, D_FF), full),
                  pl.BlockSpec((D_FF, D_MODEL), full),
                  pl.BlockSpec((1, D_MODEL), full)],
        out_specs=pl.BlockSpec((tm, D_MODEL), row),
        out_shape=jax.ShapeDtypeStruct((M, D_MODEL), F32),
        compiler_params=pltpu.CompilerParams(dimension_semantics=("parallel",), vmem_limit_bytes=VMEM_LIMIT),
        name="mlp",
    )(x1, g, w1, w2, gf)


def _bias_tiles(rel_tbl, tq):
    L = 2 * tq
    g = (rel_tbl - rel_tbl[REL_BUCKETS - 1]).T[:, _rel_bucket_np(np.arange(L))]
    w = jnp.roll(g[:, ::-1], 1, axis=1)
    w = jnp.stack([w, jnp.roll(w, tq, axis=1)], axis=1)
    tiles = jnp.tile(w, (1, 1, tq))[:, :, :tq * (L - 1)].reshape(NSA_HEADS, 2, tq, L - 1)[..., :tq]
    base = CMP_STRIDE * (CMP_BAND // 2) - (CMP_BLOCK - 1)
    left = CMP_STRIDE * (CMP_BAND - 1) - base
    gp = jnp.concatenate([jnp.zeros((NSA_HEADS, left), F32), g], axis=1)
    band = jnp.stack([gp[:, left + base - CMP_STRIDE * f:left + base - CMP_STRIDE * f + tq] for f in range(CMP_BAND)],
                     axis=-1)
    return tiles, band


def _layer(x2, B, T, rel_tbl, g_attn, w_in, b_forget, pe_k, w1_k, w2_k, pe_v, w1_v, w2_v,
           w_br_fox, w_br_nsa, w_out, g_mlp, w_ff1, w_ff2, g_final, final):
    M = B * T
    NR = T // CMP_STRIDE
    offs = np.cumsum((FOX_W, FOX_W, FOX_W, FOX_HEADS, NSA_W, KV_W, KV_W, KV_W, KV_W, KV_W, KV_W,
                      3 * NSA_HEADS, D_MODEL, D_MODEL))
    (w_fq, w_fk, w_fv, w_fl, w_nq, w_kc, w_vc, w_ksl, w_vsl, w_kwn, w_vwn, w_ng, w_ga, w_gb) = jnp.split(
        w_in, offs[:-1].tolist(), axis=-1)
    w_all = jnp.concatenate(
        [w_fq * SCALE, w_fk, w_nq * SCALE, w_ksl, w_kwn, w_fv, w_vsl, w_vwn, w_kc, w_vc, w_fl, w_ng,
         jnp.zeros((D_MODEL, _W_ALL - _SEG_MISC - FOX_HEADS - 3 * NSA_HEADS), F32)], axis=-1).astype(BF16)

    fq, fk, nq, nk, fv, nv, ckv, misc = _in_proj(x2, g_attn.reshape(1, D_MODEL), w_all, tm=512)

    bf_pad = jnp.zeros((1, LANES), F32).at[0, :FOX_HEADS].set(b_forget)
    ccol = _fox_decay(misc, bf_pad, B, T)
    o_fox = _fox_attention(fq, fk, fv.T, ccol, B, T, tq=512)

    r4 = ckv.reshape(B, NR, CMP_STRIDE, 2 * NSA_GROUPS, HEAD_DIM).transpose(0, 3, 1, 2, 4)
    r4 = r4.reshape(B, 2 * NSA_GROUPS, NR, CMP_STRIDE * HEAD_DIM)
    w1 = jnp.stack([w1_k, w1_v]).astype(BF16)
    pe = jnp.stack([pe_k.reshape(1, -1), pe_v.reshape(1, -1)]).astype(BF16)
    pe = jnp.broadcast_to(pe, (2, 16, CMP_BLOCK * HEAD_DIM))
    zpad = jnp.zeros((CMP_HIDDEN, HEAD_DIM), F32)
    w2v_pad = jnp.stack([jnp.concatenate([w2_v, zpad], axis=1), jnp.concatenate([zpad, w2_v], axis=1)]).astype(BF16)
    kc, vc = _compress(r4, w1, pe, w2_k.astype(BF16), w2v_pad, B, NR)

    bias_t, band = _bias_tiles(rel_tbl, NSA_TILE)
    hi, mid, lo = _split3(band)
    bt = jnp.concatenate([hi, mid, lo, jnp.zeros_like(hi)], axis=-1)

    ocmp, sel = _nsa_cmp(nq, kc, vc, bt, B, T, NR)
    o_nsa = _nsa_attention(nq, nk, nv, sel, ocmp, misc, bias_t, B, T)

    wbn = w_br_nsa.reshape(NSA_GROUPS, NSA_REP, HEAD_DIM, D_MODEL).transpose(1, 0, 2, 3).reshape(NSA_W, D_MODEL)
    x1 = _merge(x2, g_attn.reshape(1, D_MODEL), o_fox, o_nsa, w_ga.astype(BF16), w_gb.astype(BF16),
                w_br_fox.astype(BF16), wbn.astype(BF16), w_out.astype(BF16), tm=512)
    return _mlp(x1, g_mlp.reshape(1, D_MODEL), w_ff1.astype(BF16), w_ff2.astype(BF16),
                g_final.reshape(1, D_MODEL), tm=256, final=final)


def kernel(x, rel_bias_table, g_attn, w_in, b_forget, cmp_pe_k, cmp_w1_k, cmp_w2_k, cmp_pe_v, cmp_w1_v, cmp_w2_v,
           w_br_fox, w_br_nsa, w_out, g_mlp, w_ff1, w_ff2, g_final):
    B, T, _ = x.shape
    depth = g_attn.shape[0]
    x2 = x.reshape(B * T, D_MODEL)
    for l in range(depth):
        x2 = _layer(x2, B, T, rel_bias_table, g_attn[l], w_in[l], b_forget[l], cmp_pe_k[l], cmp_w1_k[l],
                    cmp_w2_k[l], cmp_pe_v[l], cmp_w1_v[l], cmp_w2_v[l], w_br_fox[l], w_br_nsa[l], w_out[l],
                    g_mlp[l], w_ff1[l], w_ff2[l], g_final, final=(l == depth - 1))
    return x2.reshape(B, T, D_MODEL)
```

```python
import functools
import math

import numpy as np
import jax
import jax.numpy as jnp
from jax import lax
from jax.experimental import pallas as pl
from jax.experimental.pallas import tpu as pltpu

F32 = jnp.float32
BF16 = jnp.bfloat16

D_MODEL = 1024
HEAD_DIM = 64
FOX_HEADS = 8
NSA_HEADS = 8
NSA_GROUPS = 2
NSA_REP = NSA_HEADS // NSA_GROUPS
CMP_BLOCK = 32
CMP_STRIDE = 16
CMP_HIDDEN = 256
SLC_BLOCK = 64
SLC_TOPK = 16
WINDOW = 512
REL_BUCKETS = 32
REL_MAX_DIST = 128
D_FF = 4 * D_MODEL
RMS_EPS = 1e-6
NEG_INF = -1e30
FORCED_SCORE = 1e4
SCALE = HEAD_DIM ** -0.5
LOG2E = math.log2(math.e)

FOX_W = FOX_HEADS * HEAD_DIM
NSA_W = NSA_HEADS * HEAD_DIM
KV_W = NSA_GROUPS * HEAD_DIM
LANES = 128
NSA_TILE = 256
REL_FAR = 113
CMP_BAND = 32

VMEM_LIMIT = 56 * 1024 * 1024

_NT = (((1,), (1,)), ((), ()))


def _dot(a, b):
    return jnp.dot(a, b, preferred_element_type=F32)


def _dot_nt(a, b):
    return lax.dot_general(a, b, _NT, preferred_element_type=F32)


def _split3(x):
    hi = x.astype(BF16)
    r1 = x - hi.astype(F32)
    mid = r1.astype(BF16)
    lo = (r1 - mid.astype(F32)).astype(BF16)
    return hi, mid, lo


def _rms(x, g):
    return x * lax.rsqrt(jnp.mean(x * x, axis=-1, keepdims=True) + RMS_EPS) * g


def _rel_bucket_np(n):
    exact = REL_BUCKETS // 2
    nf = np.maximum(n, exact).astype(np.float64)
    log_b = exact + (np.log(nf / exact) / math.log(REL_MAX_DIST / exact) * (REL_BUCKETS - exact)).astype(np.int64)
    return np.where(n < exact, n, np.minimum(log_b, REL_BUCKETS - 1))


_SEG_FQ, _SEG_FK, _SEG_NQ, _SEG_NK, _SEG_FV, _SEG_NV, _SEG_CKV, _SEG_MISC = (
    0, 512, 1024, 1536, 1792, 2304, 2560, 2816)
_W_ALL = 2944


def _inproj_kernel(x_ref, g_ref, w_ref, fq_ref, fk_ref, nq_ref, nk_ref, fv_ref, nv_ref, ckv_ref, misc_ref, *, tm, T):
    h = _rms(x_ref[...], g_ref[...]).astype(BF16)

    def seg(a, n):
        return _dot(h, w_ref[:, a:a + n])

    lane = lax.broadcasted_iota(jnp.int32, (1, LANES), 1)

    def heads_wide(ref, a, n_heads, mult, upper):
        r = seg(a, n_heads * HEAD_DIM)
        for j in range(n_heads):
            pair = r[:, (j // 2) * LANES:(j // 2 + 1) * LANES]
            if j % 2:
                pair = pltpu.roll(pair, HEAD_DIM, 1)
            ref[j] = jnp.where(lane < HEAD_DIM, pair * mult, upper(j)).astype(BF16)

    q_upper = jnp.where(lane < HEAD_DIM + 3, -1.0, 0.0)
    heads_wide(fq_ref, _SEG_FQ, FOX_HEADS, LOG2E, lambda j: q_upper)
    heads_wide(fk_ref, _SEG_FK, FOX_HEADS, 1.0, lambda j: 0.0)
    heads_wide(nq_ref, _SEG_NQ, NSA_HEADS, LOG2E, lambda j: 0.0)
    t = (pl.program_id(0) * tm + lax.broadcasted_iota(jnp.int32, (tm, 1), 0)) % T
    block_onehot = jnp.where(lane - HEAD_DIM == t // SLC_BLOCK, 1.0, 0.0)
    heads_wide(nk_ref, _SEG_NK, 2 * NSA_GROUPS, 1.0, lambda j: block_onehot if j < NSA_GROUPS else 0.0)
    fv_ref[...] = seg(_SEG_FV, FOX_W).astype(BF16)
    nv_ref[...] = seg(_SEG_NV, 2 * KV_W).astype(BF16)
    ckv_ref[...] = seg(_SEG_CKV, 2 * KV_W).astype(BF16)
    misc_ref[...] = seg(_SEG_MISC, LANES)


def _in_proj(x2, g, w_all, tm, T):
    M = x2.shape[0]
    assert T % tm == 0
    row = lambda i: (i, 0)
    hrow = lambda i: (0, i, 0)
    return pl.pallas_call(
        functools.partial(_inproj_kernel, tm=tm, T=T),
        grid=(M // tm,),
        in_specs=[pl.BlockSpec((tm, D_MODEL), row),
                  pl.BlockSpec((1, D_MODEL), lambda i: (0, 0)),
                  pl.BlockSpec((D_MODEL, _W_ALL), lambda i: (0, 0))],
        out_specs=[pl.BlockSpec((FOX_HEADS, tm, LANES), hrow),
                   pl.BlockSpec((FOX_HEADS, tm, LANES), hrow),
                   pl.BlockSpec((NSA_HEADS, tm, LANES), hrow),
                   pl.BlockSpec((2 * NSA_GROUPS, tm, LANES), hrow),
                   pl.BlockSpec((tm, FOX_W), row),
                   pl.BlockSpec((tm, 2 * KV_W), row),
                   pl.BlockSpec((tm, 2 * KV_W), row),
                   pl.BlockSpec((tm, LANES), row)],
        out_shape=[jax.ShapeDtypeStruct((FOX_HEADS, M, LANES), BF16),
                   jax.ShapeDtypeStruct((FOX_HEADS, M, LANES), BF16),
                   jax.ShapeDtypeStruct((NSA_HEADS, M, LANES), BF16),
                   jax.ShapeDtypeStruct((2 * NSA_GROUPS, M, LANES), BF16),
                   jax.ShapeDtypeStruct((M, FOX_W), BF16),
                   jax.ShapeDtypeStruct((M, 2 * KV_W), BF16),
                   jax.ShapeDtypeStruct((M, 2 * KV_W), BF16),
                   jax.ShapeDtypeStruct((M, LANES), F32)],
        compiler_params=pltpu.CompilerParams(dimension_semantics=("parallel",), vmem_limit_bytes=VMEM_LIMIT),
        name="in_proj",
    )(x2, g, w_all)


_SCAN_BLK = 256


def _decay_kernel(misc_ref, bf_ref, ccol_ref, *, T):
    r = lax.broadcasted_iota(jnp.int32, (_SCAN_BLK, _SCAN_BLK), 0)
    c = lax.broadcasted_iota(jnp.int32, (_SCAN_BLK, _SCAN_BLK), 1)
    tri = jnp.where(r >= c, 1.0, 0.0).astype(BF16)

    def blk(n, carry):
        s0 = pl.multiple_of(n * _SCAN_BLK, _SCAN_BLK)
        x = misc_ref[pl.ds(s0, _SCAN_BLK), :] + bf_ref[...]
        lf = jnp.minimum(x, 0.0) - jnp.log1p(jnp.exp(-jnp.abs(x)))
        hi, mid, lo = _split3(lf)
        cs = _dot(tri, hi) + _dot(tri, mid) + _dot(tri, lo) + carry
        ccol_ref[pl.ds(s0, _SCAN_BLK), :] = cs
        return cs[_SCAN_BLK - 1:_SCAN_BLK, :]

    lax.fori_loop(0, T // _SCAN_BLK, blk, jnp.zeros((1, LANES), F32))


def _fox_decay(misc, bf_pad, B, T):
    return pl.pallas_call(
        functools.partial(_decay_kernel, T=T),
        grid=(B,),
        in_specs=[pl.BlockSpec((T, LANES), lambda b: (b, 0)),
                  pl.BlockSpec((1, LANES), lambda b: (0, 0))],
        out_specs=pl.BlockSpec((T, LANES), lambda b: (b, 0)),
        out_shape=jax.ShapeDtypeStruct((B * T, LANES), F32),
        compiler_params=pltpu.CompilerParams(dimension_semantics=("parallel",), vmem_limit_bytes=VMEM_LIMIT),
        name="fox_decay",
    )(misc, bf_pad)


_FOX_PREP_BLK = 512


def _fox_kernel(q_ref, k_ref, vt_ref, c_ref, o_ref, kaug_sc, s_sc, *, tq, T):
    tk = tq
    hp = pl.program_id(1)
    i = pl.program_id(2)
    lane = lax.broadcasted_iota(jnp.int32, (1, LANES), 1)

    @pl.when(i == 0)
    def _():
        rr = lax.broadcasted_iota(jnp.int32, (LANES, LANES), 0)
        cc = lax.broadcasted_iota(jnp.int32, (LANES, LANES), 1)
        for hh in range(2):
            h = 2 * hp + hh
            places = [jnp.where((rr == h) & (cc == HEAD_DIM + e), 1.0, 0.0).astype(BF16) for e in range(3)]

            def prep(n, _, hh=hh, places=places):
                r0 = pl.multiple_of(n * _FOX_PREP_BLK, _FOX_PREP_BLK)
                terms = _split3(c_ref[pl.ds(r0, _FOX_PREP_BLK), :] * LOG2E)
                placed = sum(_dot(t, pm) for t, pm in zip(terms, places))
                kaug_sc[hh, pl.ds(r0, _FOX_PREP_BLK), :] = jnp.where(
                    lane < HEAD_DIM, k_ref[hh, pl.ds(r0, _FOX_PREP_BLK), :], placed.astype(BF16))
                return 0

            lax.fori_loop(0, T // _FOX_PREP_BLK, prep, 0)

    keys = lax.broadcasted_iota(jnp.int32, (tk, tq), 0)
    queries = lax.broadcasted_iota(jnp.int32, (tk, tq), 1)
    outs = []
    for hh in range(2):
        q = q_ref[hh]

        def scores(j, hh=hh, q=q):
            ks = pl.multiple_of(j * tk, tk)
            return _dot_nt(kaug_sc[hh, pl.ds(ks, tk), :], q)

        def update(j, s, carry, masked, hh=hh):
            m, l, acc = carry
            if masked:
                s = jnp.where(keys <= queries, s, NEG_INF)
            m_new = jnp.maximum(m, jnp.max(s, axis=0, keepdims=True))
            p = jnp.exp2(s - m_new)
            alpha = jnp.exp2(m - m_new)
            l = alpha * l + jnp.sum(p, axis=0, keepdims=True)
            ks = pl.multiple_of(j * tk, tk)
            vt = vt_ref[hh * HEAD_DIM:(hh + 1) * HEAD_DIM, pl.ds(ks, tk)]
            acc = alpha * acc + _dot(vt, p.astype(BF16))
            return m_new, l, acc

        def pair(jj, carry, scores=scores, update=update):
            j0 = 2 * jj
            s_sc[1] = scores(j0 + 1)
            carry = update(j0, s_sc[0], carry, False)
            s_sc[0] = scores(j0 + 2)
            return update(j0 + 1, s_sc[1], carry, False)

        def tail_even(carry, update=update):
            return update(i, s_sc[0], carry, True)

        def tail_odd(carry, scores=scores, update=update):
            s_sc[1] = scores(i)
            carry = update(i - 1, s_sc[0], carry, False)
            return update(i, s_sc[1], carry, True)

        s_sc[0] = scores(0)
        carry = (jnp.full((1, tq), NEG_INF, F32), jnp.zeros((1, tq), F32), jnp.zeros((HEAD_DIM, tq), F32))
        carry = lax.fori_loop(0, i // 2, pair, carry)
        m, l, acc = lax.cond(i % 2 == 1, tail_odd, tail_even, carry)
        outs.append(acc * (1.0 / l))
    o_ref[...] = jnp.concatenate(outs, axis=0).T.astype(BF16)


def _fox_attention(fq, fk, fvt, ccol, B, T, tq):
    nT = T // tq
    return pl.pallas_call(
        functools.partial(_fox_kernel, tq=tq, T=T),
        grid=(B, FOX_HEADS // 2, nT),
        in_specs=[pl.BlockSpec((2, tq, LANES), lambda b, hp, i: (hp, b * nT + i, 0)),
                  pl.BlockSpec((2, T, LANES), lambda b, hp, i: (hp, b, 0)),
                  pl.BlockSpec((LANES, T), lambda b, hp, i: (hp, b)),
                  pl.BlockSpec((T, LANES), lambda b, hp, i: (b, 0))],
        out_specs=pl.BlockSpec((tq, LANES), lambda b, hp, i: (b * nT + i, hp)),
        out_shape=jax.ShapeDtypeStruct((B * T, FOX_W), BF16),
        scratch_shapes=[pltpu.VMEM((2, T, LANES), BF16),
                        pltpu.VMEM((2, tq, tq), F32)],
        compiler_params=pltpu.CompilerParams(dimension_semantics=("parallel", "parallel", "arbitrary"),
                                             vmem_limit_bytes=VMEM_LIMIT),
        name="fox_attn",
    )(fq, fk, fvt, ccol)


def _compress_kernel(r_ref, w1_ref, pe_ref, w2k_ref, w2v_ref, kc_ref, vc_ref, *, NR):
    half = CMP_STRIDE * HEAD_DIM
    vacc = jnp.zeros((NR, LANES), F32)
    for idx in range(2 * NSA_GROUPS):
        kind, g = divmod(idx, NSA_GROUPS)
        rm = r_ref[0, idx]
        a = _dot(rm, w1_ref[kind, 0:half, :])
        bm = _dot(rm, w1_ref[kind, half:2 * half, :])
        pe_term = _dot(pe_ref[kind], w1_ref[kind])[0:1, :]
        pre = a + pltpu.roll(bm, NR - 1, 0) + pe_term
        hid = (pre * jax.nn.sigmoid(pre)).astype(BF16)
        if kind == 0:
            kc_ref[0, g] = _dot(hid, w2k_ref[...]).astype(BF16)
        else:
            vacc = vacc + _dot(hid, w2v_ref[g])
    vc_ref[0] = vacc.astype(BF16)


def _compress(r4, w1, pe, w2k, w2v_pad, B, NR):
    return pl.pallas_call(
        functools.partial(_compress_kernel, NR=NR),
        grid=(B,),
        in_specs=[pl.BlockSpec((1, 2 * NSA_GROUPS, NR, CMP_STRIDE * HEAD_DIM), lambda b: (b, 0, 0, 0)),
                  pl.BlockSpec((2, CMP_BLOCK * HEAD_DIM, CMP_HIDDEN), lambda b: (0, 0, 0)),
                  pl.BlockSpec((2, 16, CMP_BLOCK * HEAD_DIM), lambda b: (0, 0, 0)),
                  pl.BlockSpec((CMP_HIDDEN, LANES), lambda b: (0, 0)),
                  pl.BlockSpec((NSA_GROUPS, CMP_HIDDEN, LANES), lambda b: (0, 0, 0))],
        out_specs=[pl.BlockSpec((1, NSA_GROUPS, NR, LANES), lambda b: (b, 0, 0, 0)),
                   pl.BlockSpec((1, NR, LANES), lambda b: (b, 0, 0))],
        out_shape=[jax.ShapeDtypeStruct((B, NSA_GROUPS, NR, LANES), BF16),
                   jax.ShapeDtypeStruct((B, NR, LANES), BF16)],
        compiler_params=pltpu.CompilerParams(dimension_semantics=("parallel",), vmem_limit_bytes=VMEM_LIMIT),
        name="compress",
    )(r4, w1, pe, w2k, w2v_pad)


def _cmp_kernel(q_ref, kc_ref, vc_ref, bt_ref, misc_ref, ocmp_ref, qaug_ref, bias_sc, val_sc, *, tq, NR):
    i = pl.program_id(0)
    b = pl.program_id(1)
    t0 = i * tq
    n_slc = LANES // 2
    gates = jax.nn.sigmoid(misc_ref[...])

    @pl.when(b == 0)
    def _():
        f = lax.broadcasted_iota(jnp.int32, (LANES, NR), 0)
        c = lax.broadcasted_iota(jnp.int32, (LANES, NR), 1)
        place = (((f % CMP_BAND) == (c - t0 // CMP_STRIDE + CMP_BAND // 2)) & (f < 3 * CMP_BAND))
        place = jnp.where(place, 1.0, 0.0).astype(BF16)
        for h in range(NSA_HEADS):
            bias_sc[h] = _dot(bt_ref[h], place)

    t = t0 + lax.broadcasted_iota(jnp.int32, (tq, 1), 0)
    c = lax.broadcasted_iota(jnp.int32, (1, NR), 1)
    cmask = (c * CMP_STRIDE + (CMP_BLOCK - 1)) <= t
    lane = lax.broadcasted_iota(jnp.int32, (1, LANES), 1)

    jj = lax.broadcasted_iota(jnp.int32, (LANES, NR), 0)
    cc = lax.broadcasted_iota(jnp.int32, (LANES, NR), 1)
    ov = ((cc * CMP_STRIDE < jj * SLC_BLOCK + SLC_BLOCK) & (cc * CMP_STRIDE + CMP_BLOCK > jj * SLC_BLOCK)
          & (jj < n_slc) & (cc < NR - 1))
    ov = jnp.where(ov, 1.0, 0.0).astype(BF16)

    jrow = lax.broadcasted_iota(jnp.int32, (n_slc, tq), 0)
    tt = t0 + lax.broadcasted_iota(jnp.int32, (n_slc, tq), 1)
    cur = tt // SLC_BLOCK
    forced = (jrow == 0) | (jrow == cur) | (jrow == cur - 1)
    valid = jrow * SLC_BLOCK <= tt
    jcol = lax.broadcasted_iota(jnp.int32, (n_slc, 1), 0)

    outs = []
    for g in range(NSA_GROUPS):
        qs = q_ref[g * NSA_REP:(g + 1) * NSA_REP].reshape(NSA_REP * tq, LANES)
        s = _dot_nt(qs, kc_ref[0, g]).reshape(NSA_REP, tq, NR) + bias_sc[g * NSA_REP:(g + 1) * NSA_REP]
        s = jnp.where(cmask, s, NEG_INF)
        m = jnp.max(s, axis=-1, keepdims=True)
        e = jnp.where(cmask, jnp.exp2(s - m), 0.0)
        l = jnp.sum(e, axis=-1, keepdims=True)
        p = e * (1.0 / jnp.where(l > 0.0, l, 1.0))
        o = _dot(p.reshape(NSA_REP * tq, NR).astype(BF16), vc_ref[0])
        outs.append(o.reshape(NSA_REP, tq, LANES))

        hi, mid, lo = _split3(jnp.sum(p, axis=0))
        imp = (_dot_nt(ov, hi) + _dot_nt(ov, mid) + _dot_nt(ov, lo))[0:n_slc, :]
        val_sc[...] = jnp.where(forced, FORCED_SCORE, jnp.where(valid, imp, -1.0))
        val = val_sc[...]
        rank = jnp.zeros((n_slc, tq), F32)
        for k in range(n_slc):
            vk = val_sc[k:k + 1, :]
            tie = jnp.where(jcol > k, 1.0, 0.0)
            rank = rank + jnp.where(vk > val, 1.0, jnp.where(vk == val, tie, 0.0))
        selneg = jnp.where(rank < float(SLC_TOPK), 0.0, NEG_INF)
        selneg = jnp.concatenate([jnp.zeros((LANES - n_slc, tq), F32), selneg], axis=0).T.astype(BF16)
        for r in range(NSA_REP):
            h = g * NSA_REP + r
            qaug_ref[h] = jnp.where(lane < HEAD_DIM, q_ref[h], selneg)

    def gate_col(h):
        c0 = FOX_HEADS + 3 * h
        return gates[:, c0:c0 + 1]

    for r in range(NSA_REP):
        ocmp_ref[:, r * LANES:(r + 1) * LANES] = jnp.where(
            lane < HEAD_DIM, outs[0][r] * gate_col(r), outs[1][r] * gate_col(NSA_REP + r))


def _nsa_cmp(nq, kc, vc, bt, misc, B, T, NR):
    tq = NSA_TILE
    nT = T // tq
    M = B * T
    return pl.pallas_call(
        functools.partial(_cmp_kernel, tq=tq, NR=NR),
        grid=(nT, B),
        in_specs=[pl.BlockSpec((NSA_HEADS, tq, LANES), lambda i, b: (0, b * nT + i, 0)),
                  pl.BlockSpec((1, NSA_GROUPS, NR, LANES), lambda i, b: (b, 0, 0, 0)),
                  pl.BlockSpec((1, NR, LANES), lambda i, b: (b, 0, 0)),
                  pl.BlockSpec((NSA_HEADS, tq, LANES), lambda i, b: (0, 0, 0)),
                  pl.BlockSpec((tq, LANES), lambda i, b: (b * nT + i, 0))],
        out_specs=[pl.BlockSpec((tq, NSA_REP * LANES), lambda i, b: (b * nT + i, 0)),
                   pl.BlockSpec((NSA_HEADS, tq, LANES), lambda i, b: (0, b * nT + i, 0))],
        out_shape=[jax.ShapeDtypeStruct((M, NSA_REP * LANES), F32),
                   jax.ShapeDtypeStruct((NSA_HEADS, M, LANES), BF16)],
        scratch_shapes=[pltpu.VMEM((NSA_HEADS, tq, NR), F32),
                        pltpu.VMEM((LANES // 2, tq), F32)],
        compiler_params=pltpu.CompilerParams(dimension_semantics=("arbitrary", "arbitrary"),
                                             vmem_limit_bytes=VMEM_LIMIT),
        name="nsa_cmp",
    )(nq, kc, vc, bt, misc)


def _nsa_kernel(q_ref, k_ref, vt_ref, ocmp_ref, misc_ref, bias_ref, wmask_ref, o_ref, s_sc, *, tq):
    i = pl.program_id(1)
    tk = tq
    nq = NSA_REP * tq
    jp = jnp.maximum(i - 1, 0)
    jf = jnp.maximum(i - 2, 0)
    qs = [q_ref[g * NSA_REP:(g + 1) * NSA_REP].reshape(nq, LANES) for g in range(NSA_GROUPS)]

    def scores(g, branch, j):
        ks = pl.multiple_of(j * tk, tk)
        return _dot_nt(k_ref[branch * NSA_GROUPS + g, pl.ds(ks, tk), :], qs[g])

    def update(g, branch, j, s, carry):
        m, l, acc = carry
        m_new = jnp.maximum(m, jnp.max(s, axis=0, keepdims=True))
        p = jnp.exp2(s - m_new)
        alpha = jnp.exp2(m - m_new)
        l = alpha * l + jnp.sum(p, axis=0, keepdims=True)
        ks = pl.multiple_of(j * tk, tk)
        r0 = (branch * NSA_GROUPS + g) * HEAD_DIM
        acc = alpha * acc + _dot(vt_ref[r0:r0 + HEAD_DIM, pl.ds(ks, tk)], p.astype(BF16))
        return m_new, l, acc

    init = (jnp.full((1, nq), NEG_INF, F32), jnp.zeros((1, nq), F32), jnp.zeros((HEAD_DIM, nq), F32))

    def far(j, carry):
        c0, c1 = carry
        s_sc[1] = scores(1, 0, j)
        c0 = update(0, 0, j, s_sc[0], c0)
        s_sc[0] = scores(0, 0, j + 1)
        c1 = update(1, 0, j, s_sc[1], c1)
        return c0, c1

    s_sc[0] = scores(0, 0, 0)
    c_slc = list(lax.fori_loop(0, jp, far, (init, init)))
    c_win = [init, init]

    neg = jnp.float32(NEG_INF)
    near = [(0, jp, lambda g: jnp.where(i >= 1, bias_ref[g, 1], neg)),
            (0, i, lambda g: bias_ref[g, 0]),
            (1, jf, lambda g: jnp.where(i >= 2, wmask_ref[...], neg)),
            (1, jp, lambda g: jnp.where(i >= 1, bias_ref[g, 1], neg)),
            (1, i, lambda g: bias_ref[g, 0])]
    steps = [(g, branch, j, bias) for (branch, j, bias) in near for g in range(NSA_GROUPS)]
    for n, (g, branch, j, bias) in enumerate(steps):
        if n + 1 < len(steps):
            g2, branch2, j2, _ = steps[n + 1]
            s_sc[(n + 1) % 2] = scores(g2, branch2, j2)
        state = c_slc if branch == 0 else c_win
        state[g] = update(g, branch, j, s_sc[n % 2] + bias(g), state[g])

    gates = jax.nn.sigmoid(misc_ref[...].T)
    for r in range(NSA_REP):
        halves = []
        for g in range(NSA_GROUPS):
            c0 = FOX_HEADS + 3 * (g * NSA_REP + r)
            sl = slice(r * tq, (r + 1) * tq)
            (_, ls, accs), (_, lw, accw) = c_slc[g], c_win[g]
            halves.append(accs[:, sl] * (gates[c0 + 1:c0 + 2, :] / ls[:, sl])
                          + accw[:, sl] * (gates[c0 + 2:c0 + 3, :] / lw[:, sl]))
        o = jnp.concatenate(halves, axis=0).T + ocmp_ref[:, r * LANES:(r + 1) * LANES]
        o_ref[:, r * LANES:(r + 1) * LANES] = o.astype(BF16)


def _nsa_attention(qaug, nk, nvt, ocmp, misc, bias_t, wmask, B, T):
    tq = NSA_TILE
    assert WINDOW == 2 * tq
    nT = T // tq
    M = B * T
    return pl.pallas_call(
        functools.partial(_nsa_kernel, tq=tq),
        grid=(B, nT),
        in_specs=[pl.BlockSpec((NSA_HEADS, tq, LANES), lambda b, i: (0, b * nT + i, 0)),
                  pl.BlockSpec((2 * NSA_GROUPS, T, LANES), lambda b, i: (0, b, 0)),
                  pl.BlockSpec((2 * KV_W, T), lambda b, i: (0, b)),
                  pl.BlockSpec((tq, NSA_REP * LANES), lambda b, i: (b * nT + i, 0)),
                  pl.BlockSpec((tq, LANES), lambda b, i: (b * nT + i, 0)),
                  pl.BlockSpec((NSA_GROUPS, 2, tq, NSA_REP * tq), lambda b, i: (0, 0, 0, 0)),
                  pl.BlockSpec((tq, NSA_REP * tq), lambda b, i: (0, 0))],
        out_specs=pl.BlockSpec((tq, NSA_REP * LANES), lambda b, i: (b * nT + i, 0)),
        out_shape=jax.ShapeDtypeStruct((M, NSA_W), BF16),
        scratch_shapes=[pltpu.VMEM((2, tq, NSA_REP * tq), F32)],
        compiler_params=pltpu.CompilerParams(dimension_semantics=("parallel", "arbitrary"),
                                             vmem_limit_bytes=VMEM_LIMIT),
        name="nsa_attn",
    )(qaug, nk, nvt, ocmp, misc, bias_t, wmask)


def _merge_kernel(x_ref, g_ref, of_ref, on_ref, wga_ref, wgb_ref, wbf_ref, wbn_ref, wo_ref, x1_ref):
    x = x_ref[...]
    h = _rms(x, g_ref[...]).astype(BF16)
    ga = jax.nn.sigmoid(_dot(h, wga_ref[...]))
    gb = jax.nn.sigmoid(_dot(h, wgb_ref[...]))
    merged = ga * _dot(of_ref[...], wbf_ref[...]) + gb * _dot(on_ref[...], wbn_ref[...])
    x1_ref[...] = x + _dot(merged.astype(BF16), wo_ref[...])


def _merge(x2, g, o_fox, o_nsa, wga, wgb, wbf, wbn, wo, tm):
    M = x2.shape[0]
    row = lambda i: (i, 0)
    full = lambda i: (0, 0)
    return pl.pallas_call(
        _merge_kernel,
        grid=(M // tm,),
        in_specs=[pl.BlockSpec((tm, D_MODEL), row),
                  pl.BlockSpec((1, D_MODEL), full),
                  pl.BlockSpec((tm, FOX_W), row),
                  pl.BlockSpec((tm, NSA_W), row),
                  pl.BlockSpec((D_MODEL, D_MODEL), full),
                  pl.BlockSpec((D_MODEL, D_MODEL), full),
                  pl.BlockSpec((FOX_W, D_MODEL), full),
                  pl.BlockSpec((NSA_W, D_MODEL), full),
                  pl.BlockSpec((D_MODEL, D_MODEL), full)],
        out_specs=pl.BlockSpec((tm, D_MODEL), row),
        out_shape=jax.ShapeDtypeStruct((M, D_MODEL), F32),
        compiler_params=pltpu.CompilerParams(dimension_semantics=("parallel",), vmem_limit_bytes=VMEM_LIMIT),
        name="merge",
    )(x2, g, o_fox, o_nsa, wga, wgb, wbf, wbn, wo)


def _mlp_kernel(x_ref, g_ref, w1_ref, w2_ref, gf_ref, o_ref, *, final):
    x = x_ref[...]
    h = _rms(x, g_ref[...]).astype(BF16)
    u = jnp.maximum(_dot(h, w1_ref[...]), 0.0)
    y = x + _dot((u * u).astype(BF16), w2_ref[...])
    o_ref[...] = _rms(y, gf_ref[...]) if final else y


def _mlp(x1, g, w1, w2, gf, tm, final):
    M = x1.shape[0]
    row = lambda i: (i, 0)
    full = lambda i: (0, 0)
    return pl.pallas_call(
        functools.partial(_mlp_kernel, final=final),
        grid=(M // tm,),
        in_specs=[pl.BlockSpec((tm, D_MODEL), row),
                  pl.BlockSpec((1, D_MODEL), full),
                  pl.BlockSpec((D_MODEL, D_FF), full),
                  pl.BlockSpec((D_FF, D_MODEL), full),
                  pl.BlockSpec((1, D_MODEL), full)],
        out_specs=pl.BlockSpec((tm, D_MODEL), row),
        out_shape=jax.ShapeDtypeStruct((M, D_MODEL), F32),
        compiler_params=pltpu.CompilerParams(dimension_semantics=("parallel",), vmem_limit_bytes=VMEM_LIMIT),
        name="mlp",
    )(x1, g, w1, w2, gf)


def _bias_tiles(rel_tbl, tq):
    L = 2 * tq
    g = (rel_tbl - rel_tbl[REL_BUCKETS - 1]).T[:, _rel_bucket_np(np.arange(L))] * LOG2E
    w = jnp.roll(g[:, ::-1], 1, axis=1)
    w = jnp.stack([w, jnp.roll(w, tq, axis=1)], axis=1)
    tiles = jnp.tile(w, (1, 1, tq))[:, :, :tq * (L - 1)].reshape(NSA_HEADS, 2, tq, L - 1)[..., :tq]
    base = CMP_STRIDE * (CMP_BAND // 2) - (CMP_BLOCK - 1)
    left = CMP_STRIDE * (CMP_BAND - 1) - base
    gp = jnp.concatenate([jnp.zeros((NSA_HEADS, left), F32), g], axis=1)
    band = jnp.stack([gp[:, left + base - CMP_STRIDE * f:left + base - CMP_STRIDE * f + tq] for f in range(CMP_BAND)],
                     axis=-1)
    kk = np.arange(tq)[:, None]
    qq = np.arange(tq)[None, :]
    tiles = jnp.swapaxes(tiles, -1, -2)
    tiles = jnp.stack([jnp.where(kk <= qq, tiles[:, 0], NEG_INF), tiles[:, 1]], axis=1)
    tiles = tiles.reshape(NSA_GROUPS, NSA_REP, 2, tq, tq).transpose(0, 2, 3, 1, 4).reshape(
        NSA_GROUPS, 2, tq, NSA_REP * tq)
    wmask = jnp.asarray(np.tile(np.where(kk > qq, 0.0, NEG_INF).astype(np.float32), (1, NSA_REP)))
    return tiles, band, wmask


def _layer(x2, B, T, rel_tbl, g_attn, w_in, b_forget, pe_k, w1_k, w2_k, pe_v, w1_v, w2_v,
           w_br_fox, w_br_nsa, w_out, g_mlp, w_ff1, w_ff2, g_final, final):
    M = B * T
    NR = T // CMP_STRIDE
    offs = np.cumsum((FOX_W, FOX_W, FOX_W, FOX_HEADS, NSA_W, KV_W, KV_W, KV_W, KV_W, KV_W, KV_W,
                      3 * NSA_HEADS, D_MODEL, D_MODEL))
    (w_fq, w_fk, w_fv, w_fl, w_nq, w_kc, w_vc, w_ksl, w_vsl, w_kwn, w_vwn, w_ng, w_ga, w_gb) = jnp.split(
        w_in, offs[:-1].tolist(), axis=-1)
    w_all = jnp.concatenate(
        [w_fq * SCALE, w_fk, w_nq * SCALE, w_ksl, w_kwn, w_fv, w_vsl, w_vwn, w_kc, w_vc, w_fl, w_ng,
         jnp.zeros((D_MODEL, _W_ALL - _SEG_MISC - FOX_HEADS - 3 * NSA_HEADS), F32)], axis=-1).astype(BF16)

    fq, fk, nq, nk, fv, nv, ckv, misc = _in_proj(x2, g_attn.reshape(1, D_MODEL), w_all, tm=512, T=T)

    bf_pad = jnp.zeros((1, LANES), F32).at[0, :FOX_HEADS].set(b_forget)
    ccol = _fox_decay(misc, bf_pad, B, T)
    o_fox = _fox_attention(fq, fk, fv.T, ccol, B, T, tq=512)

    r4 = ckv.reshape(B, NR, CMP_STRIDE, 2 * NSA_GROUPS, HEAD_DIM).transpose(0, 3, 1, 2, 4)
    r4 = r4.reshape(B, 2 * NSA_GROUPS, NR, CMP_STRIDE * HEAD_DIM)
    w1 = jnp.stack([w1_k, w1_v]).astype(BF16)
    pe = jnp.stack([pe_k.reshape(1, -1), pe_v.reshape(1, -1)]).astype(BF16)
    pe = jnp.broadcast_to(pe, (2, 16, CMP_BLOCK * HEAD_DIM))
    zpad = jnp.zeros((CMP_HIDDEN, HEAD_DIM), F32)
    w2v_pad = jnp.stack([jnp.concatenate([w2_v, zpad], axis=1), jnp.concatenate([zpad, w2_v], axis=1)]).astype(BF16)
    w2k_pad = jnp.concatenate([w2_k, zpad], axis=1).astype(BF16)
    kc, vc = _compress(r4, w1, pe, w2k_pad, w2v_pad, B, NR)

    bias_t, band, wmask = _bias_tiles(rel_tbl, NSA_TILE)
    hi, mid, lo = _split3(band)
    bt = jnp.concatenate([hi, mid, lo, jnp.zeros_like(hi)], axis=-1)

    ocmp, qaug = _nsa_cmp(nq, kc, vc, bt, misc, B, T, NR)
    o_nsa = _nsa_attention(qaug, nk, nv.T, ocmp, misc, bias_t, wmask, B, T)

    wbn = w_br_nsa.reshape(NSA_GROUPS, NSA_REP, HEAD_DIM, D_MODEL).transpose(1, 0, 2, 3).reshape(NSA_W, D_MODEL)
    x1 = _merge(x2, g_attn.reshape(1, D_MODEL), o_fox, o_nsa, w_ga.astype(BF16), w_gb.astype(BF16),
                w_br_fox.astype(BF16), wbn.astype(BF16), w_out.astype(BF16), tm=512)
    return _mlp(x1, g_mlp.reshape(1, D_MODEL), w_ff1.astype(BF16), w_ff2.astype(BF16),
                g_final.reshape(1, D_MODEL), tm=256, final=final)


def kernel(x, rel_bias_table, g_attn, w_in, b_forget, cmp_pe_k, cmp_w1_k, cmp_w2_k, cmp_pe_v, cmp_w1_v, cmp_w2_v,
           w_br_fox, w_br_nsa, w_out, g_mlp, w_ff1, w_ff2, g_final):
    B, T, _ = x.shape
    depth = g_attn.shape[0]
    x2 = x.reshape(B * T, D_MODEL)
    for l in range(depth):
        x2 = _layer(x2, B, T, rel_bias_table, g_attn[l], w_in[l], b_forget[l], cmp_pe_k[l], cmp_w1_k[l],
                    cmp_w2_k[l], cmp_pe_v[l], cmp_w1_v[l], cmp_w2_v[l], w_br_fox[l], w_br_nsa[l], w_out[l],
                    g_mlp[l], w_ff1[l], w_ff2[l], g_final, final=(l == depth - 1))
    return x2.reshape(B, T, D_MODEL)
```

```python
import functools
import math

import numpy as np
import jax
import jax.numpy as jnp
from jax import lax
from jax.experimental import pallas as pl
from jax.experimental.pallas import tpu as pltpu

F32 = jnp.float32
BF16 = jnp.bfloat16

D_MODEL = 1024
HEAD_DIM = 64
FOX_HEADS = 8
NSA_HEADS = 8
NSA_GROUPS = 2
NSA_REP = NSA_HEADS // NSA_GROUPS
CMP_BLOCK = 32
CMP_STRIDE = 16
CMP_HIDDEN = 256
SLC_BLOCK = 64
SLC_TOPK = 16
WINDOW = 512
REL_BUCKETS = 32
REL_MAX_DIST = 128
D_FF = 4 * D_MODEL
RMS_EPS = 1e-6
NEG_INF = -1e30
FORCED_SCORE = 1e4
SCALE = HEAD_DIM ** -0.5
LOG2E = math.log2(math.e)

FOX_W = FOX_HEADS * HEAD_DIM
NSA_W = NSA_HEADS * HEAD_DIM
KV_W = NSA_GROUPS * HEAD_DIM
LANES = 128
NSA_TILE = 256
REL_FAR = 113
CMP_BAND = 32

VMEM_LIMIT = 56 * 1024 * 1024

_NT = (((1,), (1,)), ((), ()))


def _dot(a, b):
    return jnp.dot(a, b, preferred_element_type=F32)


def _dot_nt(a, b):
    return lax.dot_general(a, b, _NT, preferred_element_type=F32)


def _split3(x):
    hi = x.astype(BF16)
    r1 = x - hi.astype(F32)
    mid = r1.astype(BF16)
    lo = (r1 - mid.astype(F32)).astype(BF16)
    return hi, mid, lo


def _rms(x, g):
    return x * lax.rsqrt(jnp.mean(x * x, axis=-1, keepdims=True) + RMS_EPS) * g


def _rel_bucket_np(n):
    exact = REL_BUCKETS // 2
    nf = np.maximum(n, exact).astype(np.float64)
    log_b = exact + (np.log(nf / exact) / math.log(REL_MAX_DIST / exact) * (REL_BUCKETS - exact)).astype(np.int64)
    return np.where(n < exact, n, np.minimum(log_b, REL_BUCKETS - 1))


_SEG_FQ, _SEG_FK, _SEG_NQ, _SEG_NK, _SEG_FV, _SEG_NV, _SEG_CKV, _SEG_MISC = (
    0, 512, 1024, 1536, 1792, 2304, 2560, 2816)
_W_ALL = 2944


def _inproj_kernel(x_ref, g_ref, w_ref, fq_ref, fk_ref, nq_ref, nk_ref, fv_ref, nv_ref, ckv_ref, misc_ref, *, tm, T):
    h = _rms(x_ref[...], g_ref[...]).astype(BF16)

    def seg(a, n):
        return _dot(h, w_ref[:, a:a + n])

    lane = lax.broadcasted_iota(jnp.int32, (1, LANES), 1)

    def heads_wide(ref, a, n_heads, mult, upper):
        r = seg(a, n_heads * HEAD_DIM)
        for j in range(n_heads):
            pair = r[:, (j // 2) * LANES:(j // 2 + 1) * LANES]
            if j % 2:
                pair = pltpu.roll(pair, HEAD_DIM, 1)
            ref[j] = jnp.where(lane < HEAD_DIM, pair * mult, upper(j)).astype(BF16)

    q_upper = jnp.where(lane < HEAD_DIM + 3, -1.0, 0.0)
    heads_wide(fq_ref, _SEG_FQ, FOX_HEADS, LOG2E, lambda j: q_upper)
    heads_wide(fk_ref, _SEG_FK, FOX_HEADS, 1.0, lambda j: 0.0)
    heads_wide(nq_ref, _SEG_NQ, NSA_HEADS, LOG2E, lambda j: 0.0)
    t = (pl.program_id(0) * tm + lax.broadcasted_iota(jnp.int32, (tm, 1), 0)) % T
    block_onehot = jnp.where(lane - HEAD_DIM == t // SLC_BLOCK, 1.0, 0.0)
    heads_wide(nk_ref, _SEG_NK, 2 * NSA_GROUPS, 1.0, lambda j: block_onehot if j < NSA_GROUPS else 0.0)
    fv_ref[...] = seg(_SEG_FV, FOX_W).astype(BF16).T
    nv_ref[...] = seg(_SEG_NV, 2 * KV_W).astype(BF16).T
    ckv_ref[...] = seg(_SEG_CKV, 2 * KV_W).astype(BF16)
    misc_ref[...] = seg(_SEG_MISC, LANES)


def _in_proj(x2, g, w_all, tm, T):
    M = x2.shape[0]
    assert T % tm == 0
    row = lambda i: (i, 0)
    hrow = lambda i: (0, i, 0)
    return pl.pallas_call(
        functools.partial(_inproj_kernel, tm=tm, T=T),
        grid=(M // tm,),
        in_specs=[pl.BlockSpec((tm, D_MODEL), row),
                  pl.BlockSpec((1, D_MODEL), lambda i: (0, 0)),
                  pl.BlockSpec((D_MODEL, _W_ALL), lambda i: (0, 0))],
        out_specs=[pl.BlockSpec((FOX_HEADS, tm, LANES), hrow),
                   pl.BlockSpec((FOX_HEADS, tm, LANES), hrow),
                   pl.BlockSpec((NSA_HEADS, tm, LANES), hrow),
                   pl.BlockSpec((2 * NSA_GROUPS, tm, LANES), hrow),
                   pl.BlockSpec((FOX_W, tm), lambda i: (0, i)),
                   pl.BlockSpec((2 * KV_W, tm), lambda i: (0, i)),
                   pl.BlockSpec((tm, 2 * KV_W), row),
                   pl.BlockSpec((tm, LANES), row)],
        out_shape=[jax.ShapeDtypeStruct((FOX_HEADS, M, LANES), BF16),
                   jax.ShapeDtypeStruct((FOX_HEADS, M, LANES), BF16),
                   jax.ShapeDtypeStruct((NSA_HEADS, M, LANES), BF16),
                   jax.ShapeDtypeStruct((2 * NSA_GROUPS, M, LANES), BF16),
                   jax.ShapeDtypeStruct((FOX_W, M), BF16),
                   jax.ShapeDtypeStruct((2 * KV_W, M), BF16),
                   jax.ShapeDtypeStruct((M, 2 * KV_W), BF16),
                   jax.ShapeDtypeStruct((M, LANES), F32)],
        compiler_params=pltpu.CompilerParams(dimension_semantics=("parallel",), vmem_limit_bytes=VMEM_LIMIT),
        name="in_proj",
    )(x2, g, w_all)


_SCAN_BLK = 256


def _decay_kernel(misc_ref, bf_ref, ccol_ref, *, T):
    r = lax.broadcasted_iota(jnp.int32, (_SCAN_BLK, _SCAN_BLK), 0)
    c = lax.broadcasted_iota(jnp.int32, (_SCAN_BLK, _SCAN_BLK), 1)
    tri = jnp.where(r >= c, 1.0, 0.0).astype(BF16)

    def blk(n, carry):
        s0 = pl.multiple_of(n * _SCAN_BLK, _SCAN_BLK)
        x = misc_ref[pl.ds(s0, _SCAN_BLK), :] + bf_ref[...]
        lf = jnp.minimum(x, 0.0) - jnp.log1p(jnp.exp(-jnp.abs(x)))
        hi, mid, lo = _split3(lf)
        cs = _dot(tri, hi) + _dot(tri, mid) + _dot(tri, lo) + carry
        ccol_ref[pl.ds(s0, _SCAN_BLK), :] = cs
        return cs[_SCAN_BLK - 1:_SCAN_BLK, :]

    lax.fori_loop(0, T // _SCAN_BLK, blk, jnp.zeros((1, LANES), F32))


def _fox_decay(misc, bf_pad, B, T):
    return pl.pallas_call(
        functools.partial(_decay_kernel, T=T),
        grid=(B,),
        in_specs=[pl.BlockSpec((T, LANES), lambda b: (b, 0)),
                  pl.BlockSpec((1, LANES), lambda b: (0, 0))],
        out_specs=pl.BlockSpec((T, LANES), lambda b: (b, 0)),
        out_shape=jax.ShapeDtypeStruct((B * T, LANES), F32),
        compiler_params=pltpu.CompilerParams(dimension_semantics=("parallel",), vmem_limit_bytes=VMEM_LIMIT),
        name="fox_decay",
    )(misc, bf_pad)


_FOX_PREP_BLK = 512


SUM_ROWS = 16


def _with_sum_row(vt):
    r = lax.broadcasted_iota(jnp.int32, (SUM_ROWS, vt.shape[1]), 0)
    return jnp.concatenate([vt, jnp.where(r == 0, 1.0, 0.0).astype(vt.dtype)], axis=0)


def _softmax_step(s, vt, carry):
    m, acc = carry
    m_new = jnp.maximum(m, jnp.max(s, axis=0, keepdims=True))
    p = jnp.exp2(s - m_new).astype(BF16)
    return m_new, jnp.exp2(m - m_new) * acc + _dot(_with_sum_row(vt), p)


def _fox_kernel(q_ref, k_ref, vt_ref, c_ref, mask_ref, o_ref, kaug_sc, s_sc, qt_sc, *, tq, tk, T):
    hp = pl.program_id(1)
    i = pl.program_id(2)
    lane = lax.broadcasted_iota(jnp.int32, (1, LANES), 1)

    @pl.when(i == 0)
    def _():
        for hh in range(2):
            h = 2 * hp + hh

            def prep(n, _, hh=hh, h=h):
                r0 = pl.multiple_of(n * _FOX_PREP_BLK, _FOX_PREP_BLK)
                c = pltpu.roll(c_ref[pl.ds(r0, _FOX_PREP_BLK), :] * LOG2E, HEAD_DIM - h, 1)
                hi, mid, lo = (t.astype(F32) for t in _split3(c))
                terms = jnp.where(lane == HEAD_DIM, hi,
                                  jnp.where(lane == HEAD_DIM + 1, pltpu.roll(mid, 1, 1),
                                            jnp.where(lane == HEAD_DIM + 2, pltpu.roll(lo, 2, 1), 0.0)))
                kaug_sc[hh, pl.ds(r0, _FOX_PREP_BLK), :] = jnp.where(
                    lane < HEAD_DIM, k_ref[hh, pl.ds(r0, _FOX_PREP_BLK), :], terms.astype(BF16))
                return 0

            lax.fori_loop(0, T // _FOX_PREP_BLK, prep, 0)

    for hh in range(2):
        qt_sc[hh] = q_ref[hh].T

    def scores(hh, j):
        ks = pl.multiple_of(j * tk, tk)
        return _dot(kaug_sc[hh, pl.ds(ks, tk), :], qt_sc[hh])

    def update(hh, j, s, carry):
        ks = pl.multiple_of(j * tk, tk)
        return _softmax_step(s, vt_ref[hh * HEAD_DIM:(hh + 1) * HEAD_DIM, pl.ds(ks, tk)], carry)

    def far(j, carry):
        c0, c1 = carry
        s_sc[1] = scores(1, j)
        c0 = update(0, j, s_sc[0], c0)
        s_sc[0] = scores(0, j + 1)
        c1 = update(1, j, s_sc[1], c1)
        return c0, c1

    init = (jnp.full((1, tq), NEG_INF, F32), jnp.zeros((HEAD_DIM + SUM_ROWS, tq), F32))
    n_diag = tq // tk
    n_far = i * n_diag
    s_sc[0] = scores(0, 0)
    state = list(lax.fori_loop(0, n_far, far, (init, init)))
    steps = [(hh, d) for d in range(n_diag) for hh in range(2)]
    for n, (hh, d) in enumerate(steps):
        if n + 1 < len(steps):
            hh2, d2 = steps[n + 1]
            s_sc[(n + 1) % 2] = scores(hh2, n_far + d2)
        state[hh] = update(hh, n_far + d, s_sc[n % 2] + mask_ref[d], state[hh])
    outs = [acc[0:HEAD_DIM] * (1.0 / acc[HEAD_DIM:HEAD_DIM + 1]) for _, acc in state]
    o_ref[...] = jnp.concatenate(outs, axis=0).T.astype(BF16)


def _fox_attention(fq, fk, fvt, ccol, B, T, tq, tk):
    nT = T // tq
    n_diag = tq // tk
    kk = np.arange(tk)[None, :, None] + tk * np.arange(n_diag)[:, None, None]
    mask = jnp.asarray(np.where(kk <= np.arange(tq)[None, None, :], 0.0, NEG_INF).astype(np.float32))
    return pl.pallas_call(
        functools.partial(_fox_kernel, tq=tq, tk=tk, T=T),
        grid=(B, FOX_HEADS // 2, nT),
        in_specs=[pl.BlockSpec((2, tq, LANES), lambda b, hp, i: (hp, b * nT + i, 0)),
                  pl.BlockSpec((2, T, LANES), lambda b, hp, i: (hp, b, 0)),
                  pl.BlockSpec((LANES, T), lambda b, hp, i: (hp, b)),
                  pl.BlockSpec((T, LANES), lambda b, hp, i: (b, 0)),
                  pl.BlockSpec((n_diag, tk, tq), lambda b, hp, i: (0, 0, 0))],
        out_specs=pl.BlockSpec((tq, LANES), lambda b, hp, i: (b * nT + i, hp)),
        out_shape=jax.ShapeDtypeStruct((B * T, FOX_W), BF16),
        scratch_shapes=[pltpu.VMEM((2, T, LANES), BF16),
                        pltpu.VMEM((2, tk, tq), F32),
                        pltpu.VMEM((2, LANES, tq), BF16)],
        compiler_params=pltpu.CompilerParams(dimension_semantics=("parallel", "parallel", "arbitrary"),
                                             vmem_limit_bytes=VMEM_LIMIT),
        name="fox_attn",
    )(fq, fk, fvt, ccol, mask)


def _compress_kernel(r_ref, w1_ref, pe_ref, w2k_ref, w2v_ref, kc_ref, vc_ref, *, NR):
    half = CMP_STRIDE * HEAD_DIM
    vacc = jnp.zeros((NR, LANES), F32)
    for idx in range(2 * NSA_GROUPS):
        kind, g = divmod(idx, NSA_GROUPS)
        rm = r_ref[0, idx]
        a = _dot(rm, w1_ref[kind, 0:half, :])
        bm = _dot(rm, w1_ref[kind, half:2 * half, :])
        pe_term = _dot(pe_ref[kind], w1_ref[kind])[0:1, :]
        pre = a + pltpu.roll(bm, NR - 1, 0) + pe_term
        hid = (pre * jax.nn.sigmoid(pre)).astype(BF16)
        if kind == 0:
            kc_ref[0, g] = _dot(hid, w2k_ref[...]).astype(BF16)
        else:
            vacc = vacc + _dot(hid, w2v_ref[g])
    vc_ref[0] = vacc.astype(BF16)


def _compress(r4, w1, pe, w2k, w2v_pad, B, NR):
    return pl.pallas_call(
        functools.partial(_compress_kernel, NR=NR),
        grid=(B,),
        in_specs=[pl.BlockSpec((1, 2 * NSA_GROUPS, NR, CMP_STRIDE * HEAD_DIM), lambda b: (b, 0, 0, 0)),
                  pl.BlockSpec((2, CMP_BLOCK * HEAD_DIM, CMP_HIDDEN), lambda b: (0, 0, 0)),
                  pl.BlockSpec((2, 16, CMP_BLOCK * HEAD_DIM), lambda b: (0, 0, 0)),
                  pl.BlockSpec((CMP_HIDDEN, LANES), lambda b: (0, 0)),
                  pl.BlockSpec((NSA_GROUPS, CMP_HIDDEN, LANES), lambda b: (0, 0, 0))],
        out_specs=[pl.BlockSpec((1, NSA_GROUPS, NR, LANES), lambda b: (b, 0, 0, 0)),
                   pl.BlockSpec((1, NR, LANES), lambda b: (b, 0, 0))],
        out_shape=[jax.ShapeDtypeStruct((B, NSA_GROUPS, NR, LANES), BF16),
                   jax.ShapeDtypeStruct((B, NR, LANES), BF16)],
        compiler_params=pltpu.CompilerParams(dimension_semantics=("parallel",), vmem_limit_bytes=VMEM_LIMIT),
        name="compress",
    )(r4, w1, pe, w2k, w2v_pad)


def _cmp_kernel(q_ref, kc_ref, vc_ref, bt_ref, misc_ref, ocmp_ref, qaug_ref, bias_sc, val_sc, *, tq, NR):
    i = pl.program_id(0)
    b = pl.program_id(1)
    t0 = i * tq
    n_slc = LANES // 2
    gates = jax.nn.sigmoid(misc_ref[...])

    @pl.when(b == 0)
    def _():
        f = lax.broadcasted_iota(jnp.int32, (LANES, NR), 0)
        c = lax.broadcasted_iota(jnp.int32, (LANES, NR), 1)
        place = (((f % CMP_BAND) == (c - t0 // CMP_STRIDE + CMP_BAND // 2)) & (f < 3 * CMP_BAND))
        place = jnp.where(place, 1.0, 0.0).astype(BF16)
        for h in range(NSA_HEADS):
            bias_sc[h] = _dot(bt_ref[h], place)

    t = t0 + lax.broadcasted_iota(jnp.int32, (tq, 1), 0)
    c = lax.broadcasted_iota(jnp.int32, (1, NR), 1)
    cmask = (c * CMP_STRIDE + (CMP_BLOCK - 1)) <= t
    lane = lax.broadcasted_iota(jnp.int32, (1, LANES), 1)

    jj = lax.broadcasted_iota(jnp.int32, (LANES, NR), 0)
    cc = lax.broadcasted_iota(jnp.int32, (LANES, NR), 1)
    ov = ((cc * CMP_STRIDE < jj * SLC_BLOCK + SLC_BLOCK) & (cc * CMP_STRIDE + CMP_BLOCK > jj * SLC_BLOCK)
          & (jj < n_slc) & (cc < NR - 1))
    ov = jnp.where(ov, 1.0, 0.0).astype(BF16)

    jrow = lax.broadcasted_iota(jnp.int32, (n_slc, tq), 0)
    tt = t0 + lax.broadcasted_iota(jnp.int32, (n_slc, tq), 1)
    cur = tt // SLC_BLOCK
    forced = (jrow == 0) | (jrow == cur) | (jrow == cur - 1)
    valid = jrow * SLC_BLOCK <= tt
    jcol = lax.broadcasted_iota(jnp.int32, (n_slc, 1), 0)

    outs = []
    for g in range(NSA_GROUPS):
        qs = q_ref[g * NSA_REP:(g + 1) * NSA_REP].reshape(NSA_REP * tq, LANES)
        s = _dot_nt(qs, kc_ref[0, g]).reshape(NSA_REP, tq, NR) + bias_sc[g * NSA_REP:(g + 1) * NSA_REP]
        s = jnp.where(cmask, s, NEG_INF)
        m = jnp.max(s, axis=-1, keepdims=True)
        e = jnp.where(cmask, jnp.exp2(s - m), 0.0)
        l = jnp.sum(e, axis=-1, keepdims=True)
        p = e * (1.0 / jnp.where(l > 0.0, l, 1.0))
        o = _dot(p.reshape(NSA_REP * tq, NR).astype(BF16), vc_ref[0])
        outs.append(o.reshape(NSA_REP, tq, LANES))

        hi, mid, lo = _split3(jnp.sum(p, axis=0))
        imp = (_dot_nt(ov, hi) + _dot_nt(ov, mid) + _dot_nt(ov, lo))[0:n_slc, :]
        val_sc[...] = jnp.where(forced, FORCED_SCORE, jnp.where(valid, imp, -1.0))
        vals = [val_sc[8 * a:8 * a + 8, :] for a in range(n_slc // 8)]
        ranks = [jnp.zeros((8, tq), F32) for _ in vals]
        j8 = lax.broadcasted_iota(jnp.int32, (8, 1), 0)
        for k in range(n_slc):
            vk = val_sc[k:k + 1, :]
            for a, va in enumerate(vals):
                if 8 * a > k:
                    ahead = jnp.where(vk >= va, 1.0, 0.0)
                elif 8 * a + 7 < k:
                    ahead = jnp.where(vk > va, 1.0, 0.0)
                else:
                    tie = jnp.where(j8 > k - 8 * a, 1.0, 0.0)
                    ahead = jnp.where(vk > va, 1.0, jnp.where(vk == va, tie, 0.0))
                ranks[a] = ranks[a] + ahead
        rank = jnp.concatenate(ranks, axis=0)
        selneg = jnp.where(rank < float(SLC_TOPK), 0.0, NEG_INF)
        selneg = jnp.concatenate([jnp.zeros((LANES - n_slc, tq), F32), selneg], axis=0).T.astype(BF16)
        for r in range(NSA_REP):
            h = g * NSA_REP + r
            qaug_ref[h] = jnp.where(lane < HEAD_DIM, q_ref[h], selneg)

    def gate_col(h):
        c0 = FOX_HEADS + 3 * h
        return gates[:, c0:c0 + 1]

    for r in range(NSA_REP):
        ocmp_ref[:, r * LANES:(r + 1) * LANES] = jnp.where(
            lane < HEAD_DIM, outs[0][r] * gate_col(r), outs[1][r] * gate_col(NSA_REP + r))


def _nsa_cmp(nq, kc, vc, bt, misc, B, T, NR):
    tq = NSA_TILE
    nT = T // tq
    M = B * T
    return pl.pallas_call(
        functools.partial(_cmp_kernel, tq=tq, NR=NR),
        grid=(nT, B),
        in_specs=[pl.BlockSpec((NSA_HEADS, tq, LANES), lambda i, b: (0, b * nT + i, 0)),
                  pl.BlockSpec((1, NSA_GROUPS, NR, LANES), lambda i, b: (b, 0, 0, 0)),
                  pl.BlockSpec((1, NR, LANES), lambda i, b: (b, 0, 0)),
                  pl.BlockSpec((NSA_HEADS, tq, LANES), lambda i, b: (0, 0, 0)),
                  pl.BlockSpec((tq, LANES), lambda i, b: (b * nT + i, 0))],
        out_specs=[pl.BlockSpec((tq, NSA_REP * LANES), lambda i, b: (b * nT + i, 0)),
                   pl.BlockSpec((NSA_HEADS, tq, LANES), lambda i, b: (0, b * nT + i, 0))],
        out_shape=[jax.ShapeDtypeStruct((M, NSA_REP * LANES), F32),
                   jax.ShapeDtypeStruct((NSA_HEADS, M, LANES), BF16)],
        scratch_shapes=[pltpu.VMEM((NSA_HEADS, tq, NR), F32),
                        pltpu.VMEM((LANES // 2, tq), F32)],
        compiler_params=pltpu.CompilerParams(dimension_semantics=("arbitrary", "arbitrary"),
                                             vmem_limit_bytes=VMEM_LIMIT),
        name="nsa_cmp",
    )(nq, kc, vc, bt, misc)


def _nsa_kernel(q_ref, k_ref, vt_ref, ocmp_ref, misc_ref, bias_ref, wmask_ref, o_ref, s_sc, qt_sc, *, tq):
    i = pl.program_id(1)
    tk = tq
    nq = NSA_REP * tq
    jp = jnp.maximum(i - 1, 0)
    jf = jnp.maximum(i - 2, 0)
    for g in range(NSA_GROUPS):
        qt_sc[g] = q_ref[g * NSA_REP:(g + 1) * NSA_REP].reshape(nq, LANES).T

    def scores(g, branch, j):
        ks = pl.multiple_of(j * tk, tk)
        return _dot(k_ref[branch * NSA_GROUPS + g, pl.ds(ks, tk), :], qt_sc[g])

    def update(g, branch, j, s, carry):
        ks = pl.multiple_of(j * tk, tk)
        r0 = (branch * NSA_GROUPS + g) * HEAD_DIM
        return _softmax_step(s, vt_ref[r0:r0 + HEAD_DIM, pl.ds(ks, tk)], carry)

    init = (jnp.full((1, nq), NEG_INF, F32), jnp.zeros((HEAD_DIM + SUM_ROWS, nq), F32))

    def far(j, carry):
        c0, c1 = carry
        s_sc[1] = scores(1, 0, j)
        c0 = update(0, 0, j, s_sc[0], c0)
        s_sc[0] = scores(0, 0, j + 1)
        c1 = update(1, 0, j, s_sc[1], c1)
        return c0, c1

    s_sc[0] = scores(0, 0, 0)
    c_slc = list(lax.fori_loop(0, jp, far, (init, init)))
    c_win = [init, init]

    prev_tile = jnp.where(i >= 1, 1, 2)
    near = [(0, jp, lambda g: bias_ref[g, prev_tile]),
            (0, i, lambda g: bias_ref[g, 0]),
            (1, jf, lambda g: wmask_ref[jnp.where(i >= 2, 0, 1)]),
            (1, jp, lambda g: bias_ref[g, prev_tile]),
            (1, i, lambda g: bias_ref[g, 0])]
    steps = [(g, branch, j, bias) for (branch, j, bias) in near for g in range(NSA_GROUPS)]
    for n, (g, branch, j, bias) in enumerate(steps):
        if n + 1 < len(steps):
            g2, branch2, j2, _ = steps[n + 1]
            s_sc[(n + 1) % 2] = scores(g2, branch2, j2)
        state = c_slc if branch == 0 else c_win
        state[g] = update(g, branch, j, s_sc[n % 2] + bias(g), state[g])

    gates = jax.nn.sigmoid(misc_ref[...].T)
    for r in range(NSA_REP):
        halves = []
        for g in range(NSA_GROUPS):
            c0 = FOX_HEADS + 3 * (g * NSA_REP + r)
            sl = slice(r * tq, (r + 1) * tq)
            (_, accs), (_, accw) = c_slc[g], c_win[g]
            halves.append(
                accs[0:HEAD_DIM, sl] * (gates[c0 + 1:c0 + 2, :] / accs[HEAD_DIM:HEAD_DIM + 1, sl])
                + accw[0:HEAD_DIM, sl] * (gates[c0 + 2:c0 + 3, :] / accw[HEAD_DIM:HEAD_DIM + 1, sl]))
        o = jnp.concatenate(halves, axis=0).T + ocmp_ref[:, r * LANES:(r + 1) * LANES]
        o_ref[:, r * LANES:(r + 1) * LANES] = o.astype(BF16)


def _nsa_attention(qaug, nk, nvt, ocmp, misc, bias_t, wmask, B, T):
    tq = NSA_TILE
    assert WINDOW == 2 * tq
    nT = T // tq
    M = B * T
    return pl.pallas_call(
        functools.partial(_nsa_kernel, tq=tq),
        grid=(B, nT),
        in_specs=[pl.BlockSpec((NSA_HEADS, tq, LANES), lambda b, i: (0, b * nT + i, 0)),
                  pl.BlockSpec((2 * NSA_GROUPS, T, LANES), lambda b, i: (0, b, 0)),
                  pl.BlockSpec((2 * KV_W, T), lambda b, i: (0, b)),
                  pl.BlockSpec((tq, NSA_REP * LANES), lambda b, i: (b * nT + i, 0)),
                  pl.BlockSpec((tq, LANES), lambda b, i: (b * nT + i, 0)),
                  pl.BlockSpec((NSA_GROUPS, 3, tq, NSA_REP * tq), lambda b, i: (0, 0, 0, 0)),
                  pl.BlockSpec((2, tq, NSA_REP * tq), lambda b, i: (0, 0, 0))],
        out_specs=pl.BlockSpec((tq, NSA_REP * LANES), lambda b, i: (b * nT + i, 0)),
        out_shape=jax.ShapeDtypeStruct((M, NSA_W), BF16),
        scratch_shapes=[pltpu.VMEM((2, tq, NSA_REP * tq), F32),
                        pltpu.VMEM((NSA_GROUPS, LANES, NSA_REP * tq), BF16)],
        compiler_params=pltpu.CompilerParams(dimension_semantics=("parallel", "arbitrary"),
                                             vmem_limit_bytes=VMEM_LIMIT),
        name="nsa_attn",
    )(qaug, nk, nvt, ocmp, misc, bias_t, wmask)


def _merge_kernel(x_ref, g_ref, of_ref, on_ref, wga_ref, wgb_ref, wbf_ref, wbn_ref, wo_ref, x1_ref):
    x = x_ref[...]
    h = _rms(x, g_ref[...]).astype(BF16)
    ga = jax.nn.sigmoid(_dot(h, wga_ref[...]))
    gb = jax.nn.sigmoid(_dot(h, wgb_ref[...]))
    merged = ga * _dot(of_ref[...], wbf_ref[...]) + gb * _dot(on_ref[...], wbn_ref[...])
    x1_ref[...] = x + _dot(merged.astype(BF16), wo_ref[...])


def _merge(x2, g, o_fox, o_nsa, wga, wgb, wbf, wbn, wo, tm):
    M = x2.shape[0]
    row = lambda i: (i, 0)
    full = lambda i: (0, 0)
    return pl.pallas_call(
        _merge_kernel,
        grid=(M // tm,),
        in_specs=[pl.BlockSpec((tm, D_MODEL), row),
                  pl.BlockSpec((1, D_MODEL), full),
                  pl.BlockSpec((tm, FOX_W), row),
                  pl.BlockSpec((tm, NSA_W), row),
                  pl.BlockSpec((D_MODEL, D_MODEL), full),
                  pl.BlockSpec((D_MODEL, D_MODEL), full),
                  pl.BlockSpec((FOX_W, D_MODEL), full),
                  pl.BlockSpec((NSA_W, D_MODEL), full),
                  pl.BlockSpec((D_MODEL, D_MODEL), full)],
        out_specs=pl.BlockSpec((tm, D_MODEL), row),
        out_shape=jax.ShapeDtypeStruct((M, D_MODEL), F32),
        compiler_params=pltpu.CompilerParams(dimension_semantics=("parallel",), vmem_limit_bytes=VMEM_LIMIT),
        name="merge",
    )(x2, g, o_fox, o_nsa, wga, wgb, wbf, wbn, wo)


def _mlp_kernel(x_ref, g_ref, w1_ref, w2_ref, gf_ref, o_ref, *, final):
    x = x_ref[...]
    h = _rms(x, g_ref[...]).astype(BF16)
    u = jnp.maximum(_dot(h, w1_ref[...]), 0.0)
    y = x + _dot((u * u).astype(BF16), w2_ref[...])
    o_ref[...] = _rms(y, gf_ref[...]) if final else y


def _mlp(x1, g, w1, w2, gf, tm, final):
    M = x1.shape[0]
    row = lambda i: (i, 0)
    full = lambda i: (0, 0)
    return pl.pallas_call(
        functools.partial(_mlp_kernel, final=final),
        grid=(M // tm,),
        in_specs=[pl.BlockSpec((tm, D_MODEL), row),
                  pl.BlockSpec((1, D_MODEL), full),
                  pl.BlockSpec((D_MODEL, D_FF), full),
                  pl.BlockSpec((D_FF, D_MODEL), full),
                  pl.BlockSpec((1, D_MODEL), full)],
        out_specs=pl.BlockSpec((tm, D_MODEL), row),
        out_shape=jax.ShapeDtypeStruct((M, D_MODEL), F32),
        compiler_params=pltpu.CompilerParams(dimension_semantics=("parallel",), vmem_limit_bytes=VMEM_LIMIT),
        name="mlp",
    )(x1, g, w1, w2, gf)


def _bias_tiles(rel_tbl, tq):
    L = 2 * tq
    g = (rel_tbl - rel_tbl[REL_BUCKETS - 1]).T[:, _rel_bucket_np(np.arange(L))] * LOG2E
    w = jnp.roll(g[:, ::-1], 1, axis=1)
    w = jnp.stack([w, jnp.roll(w, tq, axis=1)], axis=1)
    tiles = jnp.tile(w, (1, 1, tq))[:, :, :tq * (L - 1)].reshape(NSA_HEADS, 2, tq, L - 1)[..., :tq]
    base = CMP_STRIDE * (CMP_BAND // 2) - (CMP_BLOCK - 1)
    left = CMP_STRIDE * (CMP_BAND - 1) - base
    gp = jnp.concatenate([jnp.zeros((NSA_HEADS, left), F32), g], axis=1)
    band = jnp.stack([gp[:, left + base - CMP_STRIDE * f:left + base - CMP_STRIDE * f + tq] for f in range(CMP_BAND)],
                     axis=-1)
    kk = np.arange(tq)[:, None]
    qq = np.arange(tq)[None, :]
    tiles = jnp.swapaxes(tiles, -1, -2)
    tiles = jnp.stack([jnp.where(kk <= qq, tiles[:, 0], NEG_INF), tiles[:, 1],
                       jnp.full_like(tiles[:, 0], NEG_INF)], axis=1)
    tiles = tiles.reshape(NSA_GROUPS, NSA_REP, 3, tq, tq).transpose(0, 2, 3, 1, 4).reshape(
        NSA_GROUPS, 3, tq, NSA_REP * tq)
    wmask = np.tile(np.where(kk > qq, 0.0, NEG_INF).astype(np.float32), (1, NSA_REP))
    wmask = jnp.asarray(np.stack([wmask, np.full_like(wmask, NEG_INF)]))
    return tiles, band, wmask


def _layer(x2, B, T, rel_tbl, g_attn, w_in, b_forget, pe_k, w1_k, w2_k, pe_v, w1_v, w2_v,
           w_br_fox, w_br_nsa, w_out, g_mlp, w_ff1, w_ff2, g_final, final):
    M = B * T
    NR = T // CMP_STRIDE
    offs = np.cumsum((FOX_W, FOX_W, FOX_W, FOX_HEADS, NSA_W, KV_W, KV_W, KV_W, KV_W, KV_W, KV_W,
                      3 * NSA_HEADS, D_MODEL, D_MODEL))
    (w_fq, w_fk, w_fv, w_fl, w_nq, w_kc, w_vc, w_ksl, w_vsl, w_kwn, w_vwn, w_ng, w_ga, w_gb) = jnp.split(
        w_in, offs[:-1].tolist(), axis=-1)
    w_all = jnp.concatenate(
        [w_fq * SCALE, w_fk, w_nq * SCALE, w_ksl, w_kwn, w_fv, w_vsl, w_vwn, w_kc, w_vc, w_fl, w_ng,
         jnp.zeros((D_MODEL, _W_ALL - _SEG_MISC - FOX_HEADS - 3 * NSA_HEADS), F32)], axis=-1).astype(BF16)

    fq, fk, nq, nk, fv, nv, ckv, misc = _in_proj(x2, g_attn.reshape(1, D_MODEL), w_all, tm=512, T=T)

    bf_pad = jnp.zeros((1, LANES), F32).at[0, :FOX_HEADS].set(b_forget)
    ccol = _fox_decay(misc, bf_pad, B, T)
    o_fox = _fox_attention(fq, fk, fv, ccol, B, T, tq=512, tk=512)

    r4 = ckv.reshape(B, NR, CMP_STRIDE, 2 * NSA_GROUPS, HEAD_DIM).transpose(0, 3, 1, 2, 4)
    r4 = r4.reshape(B, 2 * NSA_GROUPS, NR, CMP_STRIDE * HEAD_DIM)
    w1 = jnp.stack([w1_k, w1_v]).astype(BF16)
    pe = jnp.stack([pe_k.reshape(1, -1), pe_v.reshape(1, -1)]).astype(BF16)
    pe = jnp.broadcast_to(pe, (2, 16, CMP_BLOCK * HEAD_DIM))
    zpad = jnp.zeros((CMP_HIDDEN, HEAD_DIM), F32)
    w2v_pad = jnp.stack([jnp.concatenate([w2_v, zpad], axis=1), jnp.concatenate([zpad, w2_v], axis=1)]).astype(BF16)
    w2k_pad = jnp.concatenate([w2_k, zpad], axis=1).astype(BF16)
    kc, vc = _compress(r4, w1, pe, w2k_pad, w2v_pad, B, NR)

    bias_t, band, wmask = _bias_tiles(rel_tbl, NSA_TILE)
    hi, mid, lo = _split3(band)
    bt = jnp.concatenate([hi, mid, lo, jnp.zeros_like(hi)], axis=-1)

    ocmp, qaug = _nsa_cmp(nq, kc, vc, bt, misc, B, T, NR)
    o_nsa = _nsa_attention(qaug, nk, nv, ocmp, misc, bias_t, wmask, B, T)

    wbn = w_br_nsa.reshape(NSA_GROUPS, NSA_REP, HEAD_DIM, D_MODEL).transpose(1, 0, 2, 3).reshape(NSA_W, D_MODEL)
    x1 = _merge(x2, g_attn.reshape(1, D_MODEL), o_fox, o_nsa, w_ga.astype(BF16), w_gb.astype(BF16),
                w_br_fox.astype(BF16), wbn.astype(BF16), w_out.astype(BF16), tm=512)
    return _mlp(x1, g_mlp.reshape(1, D_MODEL), w_ff1.astype(BF16), w_ff2.astype(BF16),
                g_final.reshape(1, D_MODEL), tm=256, final=final)


def kernel(x, rel_bias_table, g_attn, w_in, b_forget, cmp_pe_k, cmp_w1_k, cmp_w2_k, cmp_pe_v, cmp_w1_v, cmp_w2_v,
           w_br_fox, w_br_nsa, w_out, g_mlp, w_ff1, w_ff2, g_final):
    B, T, _ = x.shape
    depth = g_attn.shape[0]
    x2 = x.reshape(B * T, D_MODEL)
    for l in range(depth):
        x2 = _layer(x2, B, T, rel_bias_table, g_attn[l], w_in[l], b_forget[l], cmp_pe_k[l], cmp_w1_k[l],
                    cmp_w2_k[l], cmp_pe_v[l], cmp_w1_v[l], cmp_w2_v[l], w_br_fox[l], w_br_nsa[l], w_out[l],
                    g_mlp[l], w_ff1[l], w_ff2[l], g_final, final=(l == depth - 1))
    return x2.reshape(B, T, D_MODEL)
```

```python
import functools
import math

import numpy as np
import jax
import jax.numpy as jnp
from jax import lax
from jax.experimental import pallas as pl
from jax.experimental.pallas import tpu as pltpu

F32 = jnp.float32
BF16 = jnp.bfloat16

D_MODEL = 1024
HEAD_DIM = 64
FOX_HEADS = 8
NSA_HEADS = 8
NSA_GROUPS = 2
NSA_REP = NSA_HEADS // NSA_GROUPS
CMP_BLOCK = 32
CMP_STRIDE = 16
CMP_HIDDEN = 256
SLC_BLOCK = 64
SLC_TOPK = 16
WINDOW = 512
REL_BUCKETS = 32
REL_MAX_DIST = 128
D_FF = 4 * D_MODEL
RMS_EPS = 1e-6
NEG_INF = -1e30
FORCED_SCORE = 1e4
SCALE = HEAD_DIM ** -0.5
LOG2E = math.log2(math.e)

FOX_W = FOX_HEADS * HEAD_DIM
NSA_W = NSA_HEADS * HEAD_DIM
KV_W = NSA_GROUPS * HEAD_DIM
LANES = 128
NSA_TILE = 256
REL_FAR = 113
CMP_BAND = 32

VMEM_LIMIT = 56 * 1024 * 1024

_NT = (((1,), (1,)), ((), ()))


def _dot(a, b):
    return jnp.dot(a, b, preferred_element_type=F32)


def _dot_nt(a, b):
    return lax.dot_general(a, b, _NT, preferred_element_type=F32)


def _split3(x):
    hi = x.astype(BF16)
    r1 = x - hi.astype(F32)
    mid = r1.astype(BF16)
    lo = (r1 - mid.astype(F32)).astype(BF16)
    return hi, mid, lo


def _rms(x, g):
    return x * lax.rsqrt(jnp.mean(x * x, axis=-1, keepdims=True) + RMS_EPS) * g


def _rel_bucket_np(n):
    exact = REL_BUCKETS // 2
    nf = np.maximum(n, exact).astype(np.float64)
    log_b = exact + (np.log(nf / exact) / math.log(REL_MAX_DIST / exact) * (REL_BUCKETS - exact)).astype(np.int64)
    return np.where(n < exact, n, np.minimum(log_b, REL_BUCKETS - 1))


_SEG_FQ, _SEG_FK, _SEG_NQ, _SEG_NK, _SEG_FV, _SEG_NV, _SEG_CKV, _SEG_MISC = (
    0, 512, 1024, 1536, 1792, 2304, 2560, 2816)
_W_ALL = 2944


def _inproj_kernel(x_ref, g_ref, w_ref, fq_ref, fk_ref, nq_ref, nk_ref, fv_ref, nv_ref, ckv_ref, misc_ref, *, tm, T):
    h = _rms(x_ref[...], g_ref[...]).astype(BF16)

    def seg(a, n):
        return _dot(h, w_ref[:, a:a + n])

    lane = lax.broadcasted_iota(jnp.int32, (1, LANES), 1)

    def heads_wide(ref, a, n_heads, mult, upper):
        r = seg(a, n_heads * HEAD_DIM)
        for j in range(n_heads):
            pair = r[:, (j // 2) * LANES:(j // 2 + 1) * LANES]
            if j % 2:
                pair = pltpu.roll(pair, HEAD_DIM, 1)
            ref[j] = jnp.where(lane < HEAD_DIM, pair * mult, upper(j)).astype(BF16)

    q_upper = jnp.where((lane == DECAY_LANES[0]) | (lane == DECAY_LANES[1]) | (lane == DECAY_LANES[2]), -1.0, 0.0)
    heads_wide(fq_ref, _SEG_FQ, FOX_HEADS, LOG2E, lambda j: q_upper)
    heads_wide(fk_ref, _SEG_FK, FOX_HEADS, 1.0, lambda j: 0.0)
    heads_wide(nq_ref, _SEG_NQ, NSA_HEADS, LOG2E, lambda j: 0.0)
    t = (pl.program_id(0) * tm + lax.broadcasted_iota(jnp.int32, (tm, 1), 0)) % T
    block_onehot = jnp.where(lane - HEAD_DIM == t // SLC_BLOCK, 1.0, 0.0)
    heads_wide(nk_ref, _SEG_NK, 2 * NSA_GROUPS, 1.0, lambda j: block_onehot if j < NSA_GROUPS else 0.0)
    fv_ref[...] = seg(_SEG_FV, FOX_W).astype(BF16).T
    nv_ref[...] = seg(_SEG_NV, 2 * KV_W).astype(BF16).T
    ckv_ref[...] = seg(_SEG_CKV, 2 * KV_W).astype(BF16)
    misc_ref[...] = seg(_SEG_MISC, LANES)


def _in_proj(x2, g, w_all, tm, T):
    M = x2.shape[0]
    assert T % tm == 0
    row = lambda i: (i, 0)
    hrow = lambda i: (0, i, 0)
    return pl.pallas_call(
        functools.partial(_inproj_kernel, tm=tm, T=T),
        grid=(M // tm,),
        in_specs=[pl.BlockSpec((tm, D_MODEL), row),
                  pl.BlockSpec((1, D_MODEL), lambda i: (0, 0)),
                  pl.BlockSpec((D_MODEL, _W_ALL), lambda i: (0, 0))],
        out_specs=[pl.BlockSpec((FOX_HEADS, tm, LANES), hrow),
                   pl.BlockSpec((FOX_HEADS, tm, LANES), hrow),
                   pl.BlockSpec((NSA_HEADS, tm, LANES), hrow),
                   pl.BlockSpec((2 * NSA_GROUPS, tm, LANES), hrow),
                   pl.BlockSpec((FOX_W, tm), lambda i: (0, i)),
                   pl.BlockSpec((2 * KV_W, tm), lambda i: (0, i)),
                   pl.BlockSpec((tm, 2 * KV_W), row),
                   pl.BlockSpec((tm, LANES), row)],
        out_shape=[jax.ShapeDtypeStruct((FOX_HEADS, M, LANES), BF16),
                   jax.ShapeDtypeStruct((FOX_HEADS, M, LANES), BF16),
                   jax.ShapeDtypeStruct((NSA_HEADS, M, LANES), BF16),
                   jax.ShapeDtypeStruct((2 * NSA_GROUPS, M, LANES), BF16),
                   jax.ShapeDtypeStruct((FOX_W, M), BF16),
                   jax.ShapeDtypeStruct((2 * KV_W, M), BF16),
                   jax.ShapeDtypeStruct((M, 2 * KV_W), BF16),
                   jax.ShapeDtypeStruct((M, LANES), F32)],
        compiler_params=pltpu.CompilerParams(dimension_semantics=("parallel",), vmem_limit_bytes=VMEM_LIMIT),
        name="in_proj",
    )(x2, g, w_all)


_SCAN_BLK = 256


DECAY_LANES = (HEAD_DIM, HEAD_DIM + FOX_HEADS, HEAD_DIM + 2 * FOX_HEADS)


def _decay_kernel(misc_ref, bf_ref, c3_ref, *, T):
    r = lax.broadcasted_iota(jnp.int32, (_SCAN_BLK, _SCAN_BLK), 0)
    c = lax.broadcasted_iota(jnp.int32, (_SCAN_BLK, _SCAN_BLK), 1)
    tri = jnp.where(r >= c, 1.0, 0.0).astype(BF16)
    lane = lax.broadcasted_iota(jnp.int32, (1, LANES), 1)

    def blk(n, carry):
        s0 = pl.multiple_of(n * _SCAN_BLK, _SCAN_BLK)
        x = misc_ref[pl.ds(s0, _SCAN_BLK), :] + bf_ref[...]
        lf = jnp.minimum(x, 0.0) - jnp.log1p(jnp.exp(-jnp.abs(x)))
        hi, mid, lo = _split3(lf)
        cs = _dot(tri, hi) + _dot(tri, mid) + _dot(tri, lo) + carry
        hi, mid, lo = (t.astype(F32) for t in _split3(cs * LOG2E))
        c3_ref[pl.ds(s0, _SCAN_BLK), :] = jnp.where(
            lane < FOX_HEADS, hi,
            jnp.where(lane < 2 * FOX_HEADS, pltpu.roll(mid, FOX_HEADS, 1),
                      jnp.where(lane < 3 * FOX_HEADS, pltpu.roll(lo, 2 * FOX_HEADS, 1), 0.0)))
        return cs[_SCAN_BLK - 1:_SCAN_BLK, :]

    lax.fori_loop(0, T // _SCAN_BLK, blk, jnp.zeros((1, LANES), F32))


def _fox_decay(misc, bf_pad, B, T):
    return pl.pallas_call(
        functools.partial(_decay_kernel, T=T),
        grid=(B,),
        in_specs=[pl.BlockSpec((T, LANES), lambda b: (b, 0)),
                  pl.BlockSpec((1, LANES), lambda b: (0, 0))],
        out_specs=pl.BlockSpec((T, LANES), lambda b: (b, 0)),
        out_shape=jax.ShapeDtypeStruct((B * T, LANES), F32),
        compiler_params=pltpu.CompilerParams(dimension_semantics=("parallel",), vmem_limit_bytes=VMEM_LIMIT),
        name="fox_decay",
    )(misc, bf_pad)


_FOX_PREP_BLK = 512


SUM_ROWS = 16


def _with_sum_row(vt):
    r = lax.broadcasted_iota(jnp.int32, (SUM_ROWS, vt.shape[1]), 0)
    return jnp.concatenate([vt, jnp.where(r == 0, 1.0, 0.0).astype(vt.dtype)], axis=0)


def _softmax_step(s, vt, carry):
    m, acc = carry
    m_new = jnp.maximum(m, jnp.max(s, axis=0, keepdims=True))
    p = jnp.exp2(s - m_new).astype(BF16)
    return m_new, jnp.exp2(m - m_new) * acc + _dot(_with_sum_row(vt), p)


def _fox_kernel(q_ref, k_ref, vt_ref, c3_ref, mask_ref, o_ref, kaug_sc, s_sc, qt_sc, *, tq, tk, T, nh):
    hg = pl.program_id(1)
    i = pl.program_id(2)
    lane = lax.broadcasted_iota(jnp.int32, (1, LANES), 1)
    decay_lane = (lane == DECAY_LANES[0]) | (lane == DECAY_LANES[1]) | (lane == DECAY_LANES[2])

    @pl.when(i == 0)
    def _():
        for hh in range(nh):
            def prep(n, _, hh=hh):
                r0 = pl.multiple_of(n * _FOX_PREP_BLK, _FOX_PREP_BLK)
                c3 = pltpu.roll(c3_ref[pl.ds(r0, _FOX_PREP_BLK), :], HEAD_DIM - (nh * hg + hh), 1)
                kaug_sc[hh, pl.ds(r0, _FOX_PREP_BLK), :] = jnp.where(
                    lane < HEAD_DIM, k_ref[hh, pl.ds(r0, _FOX_PREP_BLK), :],
                    jnp.where(decay_lane, c3, 0.0).astype(BF16))
                return 0

            lax.fori_loop(0, T // _FOX_PREP_BLK, prep, 0)

    for hh in range(nh):
        qt_sc[hh] = q_ref[hh].T

    def scores(hh, j):
        ks = pl.multiple_of(j * tk, tk)
        return _dot(kaug_sc[hh, pl.ds(ks, tk), :], qt_sc[hh])

    def update(hh, j, s, carry):
        ks = pl.multiple_of(j * tk, tk)
        return _softmax_step(s, vt_ref[hh * HEAD_DIM:(hh + 1) * HEAD_DIM, pl.ds(ks, tk)], carry)

    def far(j, carry):
        state = list(carry)
        for hh in range(nh):
            if hh + 1 < nh:
                s_sc[(hh + 1) % 2] = scores(hh + 1, j)
            else:
                s_sc[0] = scores(0, j + 1)
            state[hh] = update(hh, j, s_sc[hh % 2], state[hh])
        return tuple(state)

    init = (jnp.full((1, tq), NEG_INF, F32), jnp.zeros((HEAD_DIM + SUM_ROWS, tq), F32))
    n_diag = tq // tk
    n_far = i * n_diag
    s_sc[0] = scores(0, 0)
    state = list(lax.fori_loop(0, n_far, far, (init,) * nh))
    steps = [(hh, d) for d in range(n_diag) for hh in range(nh)]
    for n, (hh, d) in enumerate(steps):
        if n + 1 < len(steps):
            hh2, d2 = steps[n + 1]
            s_sc[(n + 1) % 2] = scores(hh2, n_far + d2)
        state[hh] = update(hh, n_far + d, s_sc[n % 2] + mask_ref[d], state[hh])
    outs = [acc[0:HEAD_DIM] * (1.0 / acc[HEAD_DIM:HEAD_DIM + 1]) for _, acc in state]
    o_ref[...] = jnp.concatenate(outs, axis=0).T.astype(BF16)


def _fox_attention(fq, fk, fvt, c3, B, T, tq, tk, nh):
    assert nh % 2 == 0 and FOX_HEADS % nh == 0
    nT = T // tq
    n_diag = tq // tk
    kk = np.arange(tk)[None, :, None] + tk * np.arange(n_diag)[:, None, None]
    mask = jnp.asarray(np.where(kk <= np.arange(tq)[None, None, :], 0.0, NEG_INF).astype(np.float32))
    return pl.pallas_call(
        functools.partial(_fox_kernel, tq=tq, tk=tk, T=T, nh=nh),
        grid=(B, FOX_HEADS // nh, nT),
        in_specs=[pl.BlockSpec((nh, tq, LANES), lambda b, hp, i: (hp, b * nT + i, 0)),
                  pl.BlockSpec((nh, T, LANES), lambda b, hp, i: (hp, b, 0)),
                  pl.BlockSpec((nh * HEAD_DIM, T), lambda b, hp, i: (hp, b)),
                  pl.BlockSpec((T, LANES), lambda b, hp, i: (b, 0)),
                  pl.BlockSpec((n_diag, tk, tq), lambda b, hp, i: (0, 0, 0))],
        out_specs=pl.BlockSpec((tq, nh * HEAD_DIM), lambda b, hp, i: (b * nT + i, hp)),
        out_shape=jax.ShapeDtypeStruct((B * T, FOX_W), BF16),
        scratch_shapes=[pltpu.VMEM((nh, T, LANES), BF16),
                        pltpu.VMEM((2, tk, tq), F32),
                        pltpu.VMEM((nh, LANES, tq), BF16)],
        compiler_params=pltpu.CompilerParams(dimension_semantics=("parallel", "parallel", "arbitrary"),
                                             vmem_limit_bytes=VMEM_LIMIT),
        name="fox_attn",
    )(fq, fk, fvt, c3, mask)


def _compress_kernel(r_ref, w1_ref, pe_ref, w2k_ref, w2v_ref, kc_ref, vc_ref, *, NR):
    half = CMP_STRIDE * HEAD_DIM
    vacc = jnp.zeros((NR, LANES), F32)
    for idx in range(2 * NSA_GROUPS):
        kind, g = divmod(idx, NSA_GROUPS)
        rm = r_ref[0, idx]
        a = _dot(rm, w1_ref[kind, 0:half, :])
        bm = _dot(rm, w1_ref[kind, half:2 * half, :])
        pe_term = _dot(pe_ref[kind], w1_ref[kind])[0:1, :]
        pre = a + pltpu.roll(bm, NR - 1, 0) + pe_term
        hid = (pre * jax.nn.sigmoid(pre)).astype(BF16)
        if kind == 0:
            kc_ref[0, g] = _dot(hid, w2k_ref[...]).astype(BF16)
        else:
            vacc = vacc + _dot(hid, w2v_ref[g])
    vc_ref[0] = vacc.astype(BF16)


def _compress(r4, w1, pe, w2k, w2v_pad, B, NR):
    return pl.pallas_call(
        functools.partial(_compress_kernel, NR=NR),
        grid=(B,),
        in_specs=[pl.BlockSpec((1, 2 * NSA_GROUPS, NR, CMP_STRIDE * HEAD_DIM), lambda b: (b, 0, 0, 0)),
                  pl.BlockSpec((2, CMP_BLOCK * HEAD_DIM, CMP_HIDDEN), lambda b: (0, 0, 0)),
                  pl.BlockSpec((2, 16, CMP_BLOCK * HEAD_DIM), lambda b: (0, 0, 0)),
                  pl.BlockSpec((CMP_HIDDEN, LANES), lambda b: (0, 0)),
                  pl.BlockSpec((NSA_GROUPS, CMP_HIDDEN, LANES), lambda b: (0, 0, 0))],
        out_specs=[pl.BlockSpec((1, NSA_GROUPS, NR, LANES), lambda b: (b, 0, 0, 0)),
                   pl.BlockSpec((1, NR, LANES), lambda b: (b, 0, 0))],
        out_shape=[jax.ShapeDtypeStruct((B, NSA_GROUPS, NR, LANES), BF16),
                   jax.ShapeDtypeStruct((B, NR, LANES), BF16)],
        compiler_params=pltpu.CompilerParams(dimension_semantics=("parallel",), vmem_limit_bytes=VMEM_LIMIT),
        name="compress",
    )(r4, w1, pe, w2k, w2v_pad)


def _cmp_kernel(q_ref, kc_ref, vc_ref, bt_ref, misc_ref, ocmp_ref, qaug_ref, bias_sc, val_sc, *, tq, NR):
    i = pl.program_id(0)
    b = pl.program_id(1)
    t0 = i * tq
    n_slc = LANES // 2
    gates = jax.nn.sigmoid(misc_ref[...])

    @pl.when(b == 0)
    def _():
        f = lax.broadcasted_iota(jnp.int32, (LANES, NR), 0)
        c = lax.broadcasted_iota(jnp.int32, (LANES, NR), 1)
        place = (((f % CMP_BAND) == (c - t0 // CMP_STRIDE + CMP_BAND // 2)) & (f < 3 * CMP_BAND))
        place = jnp.where(place, 1.0, 0.0).astype(BF16)
        for h in range(NSA_HEADS):
            bias_sc[h] = _dot(bt_ref[h], place)

    t = t0 + lax.broadcasted_iota(jnp.int32, (tq, 1), 0)
    c = lax.broadcasted_iota(jnp.int32, (1, NR), 1)
    cmask = (c * CMP_STRIDE + (CMP_BLOCK - 1)) <= t
    lane = lax.broadcasted_iota(jnp.int32, (1, LANES), 1)

    jj = lax.broadcasted_iota(jnp.int32, (LANES, NR), 0)
    cc = lax.broadcasted_iota(jnp.int32, (LANES, NR), 1)
    ov = ((cc * CMP_STRIDE < jj * SLC_BLOCK + SLC_BLOCK) & (cc * CMP_STRIDE + CMP_BLOCK > jj * SLC_BLOCK)
          & (jj < n_slc) & (cc < NR - 1))
    ov = jnp.where(ov, 1.0, 0.0).astype(BF16)

    jrow = lax.broadcasted_iota(jnp.int32, (n_slc, tq), 0)
    tt = t0 + lax.broadcasted_iota(jnp.int32, (n_slc, tq), 1)
    cur = tt // SLC_BLOCK
    forced = (jrow == 0) | (jrow == cur) | (jrow == cur - 1)
    valid = jrow * SLC_BLOCK <= tt
    jcol = lax.broadcasted_iota(jnp.int32, (n_slc, 1), 0)

    outs = []
    for g in range(NSA_GROUPS):
        qs = q_ref[g * NSA_REP:(g + 1) * NSA_REP].reshape(NSA_REP * tq, LANES)
        s = _dot_nt(qs, kc_ref[0, g]).reshape(NSA_REP, tq, NR) + bias_sc[g * NSA_REP:(g + 1) * NSA_REP]
        s = jnp.where(cmask, s, NEG_INF)
        m = jnp.max(s, axis=-1, keepdims=True)
        e = jnp.where(cmask, jnp.exp2(s - m), 0.0)
        l = jnp.sum(e, axis=-1, keepdims=True)
        p = e * (1.0 / jnp.where(l > 0.0, l, 1.0))
        o = _dot(p.reshape(NSA_REP * tq, NR).astype(BF16), vc_ref[0])
        outs.append(o.reshape(NSA_REP, tq, LANES))

        hi, mid, lo = _split3(jnp.sum(p, axis=0))
        imp = (_dot_nt(ov, hi) + _dot_nt(ov, mid) + _dot_nt(ov, lo))[0:n_slc, :]
        val_sc[...] = jnp.where(forced, FORCED_SCORE, jnp.where(valid, imp, -1.0))
        vals = [val_sc[8 * a:8 * a + 8, :] for a in range(n_slc // 8)]
        ranks = [jnp.zeros((8, tq), F32) for _ in vals]
        j8 = lax.broadcasted_iota(jnp.int32, (8, 1), 0)
        for k in range(n_slc):
            vk = val_sc[k:k + 1, :]
            for a, va in enumerate(vals):
                if 8 * a > k:
                    ahead = jnp.where(vk >= va, 1.0, 0.0)
                elif 8 * a + 7 < k:
                    ahead = jnp.where(vk > va, 1.0, 0.0)
                else:
                    tie = jnp.where(j8 > k - 8 * a, 1.0, 0.0)
                    ahead = jnp.where(vk > va, 1.0, jnp.where(vk == va, tie, 0.0))
                ranks[a] = ranks[a] + ahead
        rank = jnp.concatenate(ranks, axis=0)
        selneg = jnp.where(rank < float(SLC_TOPK), 0.0, NEG_INF)
        selneg = jnp.concatenate([jnp.zeros((LANES - n_slc, tq), F32), selneg], axis=0).T.astype(BF16)
        for r in range(NSA_REP):
            h = g * NSA_REP + r
            qaug_ref[h] = jnp.where(lane < HEAD_DIM, q_ref[h], selneg)

    def gate_col(h):
        c0 = FOX_HEADS + 3 * h
        return gates[:, c0:c0 + 1]

    for r in range(NSA_REP):
        ocmp_ref[:, r * LANES:(r + 1) * LANES] = jnp.where(
            lane < HEAD_DIM, outs[0][r] * gate_col(r), outs[1][r] * gate_col(NSA_REP + r))


def _nsa_cmp(nq, kc, vc, bt, misc, B, T, NR):
    tq = NSA_TILE
    nT = T // tq
    M = B * T
    return pl.pallas_call(
        functools.partial(_cmp_kernel, tq=tq, NR=NR),
        grid=(nT, B),
        in_specs=[pl.BlockSpec((NSA_HEADS, tq, LANES), lambda i, b: (0, b * nT + i, 0)),
                  pl.BlockSpec((1, NSA_GROUPS, NR, LANES), lambda i, b: (b, 0, 0, 0)),
                  pl.BlockSpec((1, NR, LANES), lambda i, b: (b, 0, 0)),
                  pl.BlockSpec((NSA_HEADS, tq, LANES), lambda i, b: (0, 0, 0)),
                  pl.BlockSpec((tq, LANES), lambda i, b: (b * nT + i, 0))],
        out_specs=[pl.BlockSpec((tq, NSA_REP * LANES), lambda i, b: (b * nT + i, 0)),
                   pl.BlockSpec((NSA_HEADS, tq, LANES), lambda i, b: (0, b * nT + i, 0))],
        out_shape=[jax.ShapeDtypeStruct((M, NSA_REP * LANES), F32),
                   jax.ShapeDtypeStruct((NSA_HEADS, M, LANES), BF16)],
        scratch_shapes=[pltpu.VMEM((NSA_HEADS, tq, NR), F32),
                        pltpu.VMEM((LANES // 2, tq), F32)],
        compiler_params=pltpu.CompilerParams(dimension_semantics=("arbitrary", "arbitrary"),
                                             vmem_limit_bytes=VMEM_LIMIT),
        name="nsa_cmp",
    )(nq, kc, vc, bt, misc)


def _nsa_kernel(q_ref, k_ref, vt_ref, ocmp_ref, misc_ref, dist_ref, o_ref, s_sc, qt_sc, bias_ref, wmask_ref, *, tq):
    i = pl.program_id(1)
    tk = tq
    nq = NSA_REP * tq
    jp = jnp.maximum(i - 1, 0)
    jf = jnp.maximum(i - 2, 0)

    @pl.when((pl.program_id(0) == 0) & (i == 0))
    def _():
        kk = lax.broadcasted_iota(jnp.int32, (tk, tq), 0)
        qq = lax.broadcasted_iota(jnp.int32, (tk, tq), 1)
        for h in range(NSA_HEADS):
            g, r = divmod(h, NSA_REP)
            rows = jnp.broadcast_to(dist_ref[h:h + 1, :], (tk, 2 * tq))
            pair = pltpu.roll(rows, 0, 1, stride=1, stride_axis=0)
            bias_ref[g, 0, :, r * tq:(r + 1) * tq] = jnp.where(kk <= qq, pair[:, 0:tq], NEG_INF)
            bias_ref[g, 1, :, r * tq:(r + 1) * tq] = pair[:, tq:2 * tq]
        for g in range(NSA_GROUPS):
            bias_ref[g, 2] = jnp.full((tk, nq), NEG_INF, F32)
        for r in range(NSA_REP):
            wmask_ref[0, :, r * tq:(r + 1) * tq] = jnp.where(kk > qq, 0.0, NEG_INF)
        wmask_ref[1] = jnp.full((tk, nq), NEG_INF, F32)

    for g in range(NSA_GROUPS):
        qt_sc[g] = q_ref[g * NSA_REP:(g + 1) * NSA_REP].reshape(nq, LANES).T

    def scores(g, branch, j):
        ks = pl.multiple_of(j * tk, tk)
        return _dot(k_ref[branch * NSA_GROUPS + g, pl.ds(ks, tk), :], qt_sc[g])

    def update(g, branch, j, s, carry):
        ks = pl.multiple_of(j * tk, tk)
        r0 = (branch * NSA_GROUPS + g) * HEAD_DIM
        return _softmax_step(s, vt_ref[r0:r0 + HEAD_DIM, pl.ds(ks, tk)], carry)

    init = (jnp.full((1, nq), NEG_INF, F32), jnp.zeros((HEAD_DIM + SUM_ROWS, nq), F32))

    def far(j, carry):
        c0, c1 = carry
        s_sc[1] = scores(1, 0, j)
        c0 = update(0, 0, j, s_sc[0], c0)
        s_sc[0] = scores(0, 0, j + 1)
        c1 = update(1, 0, j, s_sc[1], c1)
        return c0, c1

    s_sc[0] = scores(0, 0, 0)
    c_slc = list(lax.fori_loop(0, jp, far, (init, init)))
    c_win = [init, init]

    prev_tile = jnp.where(i >= 1, 1, 2)
    near = [(0, jp, lambda g: bias_ref[g, prev_tile]),
            (0, i, lambda g: bias_ref[g, 0]),
            (1, jf, lambda g: wmask_ref[jnp.where(i >= 2, 0, 1)]),
            (1, jp, lambda g: bias_ref[g, prev_tile]),
            (1, i, lambda g: bias_ref[g, 0])]
    steps = [(g, branch, j, bias) for (branch, j, bias) in near for g in range(NSA_GROUPS)]
    for n, (g, branch, j, bias) in enumerate(steps):
        if n + 1 < len(steps):
            g2, branch2, j2, _ = steps[n + 1]
            s_sc[(n + 1) % 2] = scores(g2, branch2, j2)
        state = c_slc if branch == 0 else c_win
        state[g] = update(g, branch, j, s_sc[n % 2] + bias(g), state[g])

    gates = jax.nn.sigmoid(misc_ref[...].T)
    for r in range(NSA_REP):
        halves = []
        for g in range(NSA_GROUPS):
            c0 = FOX_HEADS + 3 * (g * NSA_REP + r)
            sl = slice(r * tq, (r + 1) * tq)
            (_, accs), (_, accw) = c_slc[g], c_win[g]
            halves.append(
                accs[0:HEAD_DIM, sl] * (gates[c0 + 1:c0 + 2, :] / accs[HEAD_DIM:HEAD_DIM + 1, sl])
                + accw[0:HEAD_DIM, sl] * (gates[c0 + 2:c0 + 3, :] / accw[HEAD_DIM:HEAD_DIM + 1, sl]))
        o = jnp.concatenate(halves, axis=0).T + ocmp_ref[:, r * LANES:(r + 1) * LANES]
        o_ref[:, r * LANES:(r + 1) * LANES] = o.astype(BF16)


def _nsa_attention(qaug, nk, nvt, ocmp, misc, dist_bias, B, T):
    tq = NSA_TILE
    assert WINDOW == 2 * tq
    nT = T // tq
    M = B * T
    return pl.pallas_call(
        functools.partial(_nsa_kernel, tq=tq),
        grid=(B, nT),
        in_specs=[pl.BlockSpec((NSA_HEADS, tq, LANES), lambda b, i: (0, b * nT + i, 0)),
                  pl.BlockSpec((2 * NSA_GROUPS, T, LANES), lambda b, i: (0, b, 0)),
                  pl.BlockSpec((2 * KV_W, T), lambda b, i: (0, b)),
                  pl.BlockSpec((tq, NSA_REP * LANES), lambda b, i: (b * nT + i, 0)),
                  pl.BlockSpec((tq, LANES), lambda b, i: (b * nT + i, 0)),
                  pl.BlockSpec((NSA_HEADS, 2 * tq), lambda b, i: (0, 0))],
        out_specs=pl.BlockSpec((tq, NSA_REP * LANES), lambda b, i: (b * nT + i, 0)),
        out_shape=jax.ShapeDtypeStruct((M, NSA_W), BF16),
        scratch_shapes=[pltpu.VMEM((2, tq, NSA_REP * tq), F32),
                        pltpu.VMEM((NSA_GROUPS, LANES, NSA_REP * tq), BF16),
                        pltpu.VMEM((NSA_GROUPS, 3, tq, NSA_REP * tq), F32),
                        pltpu.VMEM((2, tq, NSA_REP * tq), F32)],
        compiler_params=pltpu.CompilerParams(dimension_semantics=("arbitrary", "arbitrary"),
                                             vmem_limit_bytes=VMEM_LIMIT),
        name="nsa_attn",
    )(qaug, nk, nvt, ocmp, misc, dist_bias)


def _merge_kernel(x_ref, g_ref, of_ref, on_ref, wga_ref, wgb_ref, wbf_ref, wbn_ref, wo_ref, x1_ref):
    x = x_ref[...]
    h = _rms(x, g_ref[...]).astype(BF16)
    ga = jax.nn.sigmoid(_dot(h, wga_ref[...]))
    gb = jax.nn.sigmoid(_dot(h, wgb_ref[...]))
    merged = ga * _dot(of_ref[...], wbf_ref[...]) + gb * _dot(on_ref[...], wbn_ref[...])
    x1_ref[...] = x + _dot(merged.astype(BF16), wo_ref[...])


def _merge(x2, g, o_fox, o_nsa, wga, wgb, wbf, wbn, wo, tm):
    M = x2.shape[0]
    row = lambda i: (i, 0)
    full = lambda i: (0, 0)
    return pl.pallas_call(
        _merge_kernel,
        grid=(M // tm,),
        in_specs=[pl.BlockSpec((tm, D_MODEL), row),
                  pl.BlockSpec((1, D_MODEL), full),
                  pl.BlockSpec((tm, FOX_W), row),
                  pl.BlockSpec((tm, NSA_W), row),
                  pl.BlockSpec((D_MODEL, D_MODEL), full),
                  pl.BlockSpec((D_MODEL, D_MODEL), full),
                  pl.BlockSpec((FOX_W, D_MODEL), full),
                  pl.BlockSpec((NSA_W, D_MODEL), full),
                  pl.BlockSpec((D_MODEL, D_MODEL), full)],
        out_specs=pl.BlockSpec((tm, D_MODEL), row),
        out_shape=jax.ShapeDtypeStruct((M, D_MODEL), F32),
        compiler_params=pltpu.CompilerParams(dimension_semantics=("parallel",), vmem_limit_bytes=VMEM_LIMIT),
        name="merge",
    )(x2, g, o_fox, o_nsa, wga, wgb, wbf, wbn, wo)


def _mlp_kernel(x_ref, g_ref, w1_ref, w2_ref, gf_ref, o_ref, *, final):
    x = x_ref[...]
    h = _rms(x, g_ref[...]).astype(BF16)
    u = jnp.maximum(_dot(h, w1_ref[...]), 0.0)
    y = x + _dot((u * u).astype(BF16), w2_ref[...])
    o_ref[...] = _rms(y, gf_ref[...]) if final else y


def _mlp(x1, g, w1, w2, gf, tm, final):
    M = x1.shape[0]
    row = lambda i: (i, 0)
    full = lambda i: (0, 0)
    return pl.pallas_call(
        functools.partial(_mlp_kernel, final=final),
        grid=(M // tm,),
        in_specs=[pl.BlockSpec((tm, D_MODEL), row),
                  pl.BlockSpec((1, D_MODEL), full),
                  pl.BlockSpec((D_MODEL, D_FF), full),
                  pl.BlockSpec((D_FF, D_MODEL), full),
                  pl.BlockSpec((1, D_MODEL), full)],
        out_specs=pl.BlockSpec((tm, D_MODEL), row),
        out_shape=jax.ShapeDtypeStruct((M, D_MODEL), F32),
        compiler_params=pltpu.CompilerParams(dimension_semantics=("parallel",), vmem_limit_bytes=VMEM_LIMIT),
        name="mlp",
    )(x1, g, w1, w2, gf)


def _bias_tables(rel_tbl, tq):
    L = 2 * tq
    g = (rel_tbl - rel_tbl[REL_BUCKETS - 1]).T[:, _rel_bucket_np(np.arange(L))] * LOG2E
    base = CMP_STRIDE * (CMP_BAND // 2) - (CMP_BLOCK - 1)
    left = CMP_STRIDE * (CMP_BAND - 1) - base
    gp = jnp.concatenate([jnp.zeros((NSA_HEADS, left), F32), g], axis=1)
    band = jnp.stack([gp[:, left + base - CMP_STRIDE * f:left + base - CMP_STRIDE * f + tq] for f in range(CMP_BAND)],
                     axis=-1)
    return g, band


def _layer(x2, B, T, rel_tbl, g_attn, w_in, b_forget, pe_k, w1_k, w2_k, pe_v, w1_v, w2_v,
           w_br_fox, w_br_nsa, w_out, g_mlp, w_ff1, w_ff2, g_final, final):
    M = B * T
    NR = T // CMP_STRIDE
    offs = np.cumsum((FOX_W, FOX_W, FOX_W, FOX_HEADS, NSA_W, KV_W, KV_W, KV_W, KV_W, KV_W, KV_W,
                      3 * NSA_HEADS, D_MODEL, D_MODEL))
    (w_fq, w_fk, w_fv, w_fl, w_nq, w_kc, w_vc, w_ksl, w_vsl, w_kwn, w_vwn, w_ng, w_ga, w_gb) = jnp.split(
        w_in, offs[:-1].tolist(), axis=-1)
    w_all = jnp.concatenate(
        [w_fq * SCALE, w_fk, w_nq * SCALE, w_ksl, w_kwn, w_fv, w_vsl, w_vwn, w_kc, w_vc, w_fl, w_ng,
         jnp.zeros((D_MODEL, _W_ALL - _SEG_MISC - FOX_HEADS - 3 * NSA_HEADS), F32)], axis=-1).astype(BF16)

    fq, fk, nq, nk, fv, nv, ckv, misc = _in_proj(x2, g_attn.reshape(1, D_MODEL), w_all, tm=512, T=T)

    bf_pad = jnp.zeros((1, LANES), F32).at[0, :FOX_HEADS].set(b_forget)
    c3 = _fox_decay(misc, bf_pad, B, T)
    o_fox = _fox_attention(fq, fk, fv, c3, B, T, tq=512, tk=512, nh=4)

    r4 = ckv.reshape(B, NR, CMP_STRIDE, 2 * NSA_GROUPS, HEAD_DIM).transpose(0, 3, 1, 2, 4)
    r4 = r4.reshape(B, 2 * NSA_GROUPS, NR, CMP_STRIDE * HEAD_DIM)
    w1 = jnp.stack([w1_k, w1_v]).astype(BF16)
    pe = jnp.stack([pe_k.reshape(1, -1), pe_v.reshape(1, -1)]).astype(BF16)
    pe = jnp.broadcast_to(pe, (2, 16, CMP_BLOCK * HEAD_DIM))
    zpad = jnp.zeros((CMP_HIDDEN, HEAD_DIM), F32)
    w2v_pad = jnp.stack([jnp.concatenate([w2_v, zpad], axis=1), jnp.concatenate([zpad, w2_v], axis=1)]).astype(BF16)
    w2k_pad = jnp.concatenate([w2_k, zpad], axis=1).astype(BF16)
    kc, vc = _compress(r4, w1, pe, w2k_pad, w2v_pad, B, NR)

    dist_bias, band = _bias_tables(rel_tbl, NSA_TILE)
    hi, mid, lo = _split3(band)
    bt = jnp.concatenate([hi, mid, lo, jnp.zeros_like(hi)], axis=-1)

    ocmp, qaug = _nsa_cmp(nq, kc, vc, bt, misc, B, T, NR)
    o_nsa = _nsa_attention(qaug, nk, nv, ocmp, misc, dist_bias, B, T)

    wbn = w_br_nsa.reshape(NSA_GROUPS, NSA_REP, HEAD_DIM, D_MODEL).transpose(1, 0, 2, 3).reshape(NSA_W, D_MODEL)
    x1 = _merge(x2, g_attn.reshape(1, D_MODEL), o_fox, o_nsa, w_ga.astype(BF16), w_gb.astype(BF16),
                w_br_fox.astype(BF16), wbn.astype(BF16), w_out.astype(BF16), tm=512)
    return _mlp(x1, g_mlp.reshape(1, D_MODEL), w_ff1.astype(BF16), w_ff2.astype(BF16),
                g_final.reshape(1, D_MODEL), tm=256, final=final)


def kernel(x, rel_bias_table, g_attn, w_in, b_forget, cmp_pe_k, cmp_w1_k, cmp_w2_k, cmp_pe_v, cmp_w1_v, cmp_w2_v,
           w_br_fox, w_br_nsa, w_out, g_mlp, w_ff1, w_ff2, g_final):
    B, T, _ = x.shape
    depth = g_attn.shape[0]
    x2 = x.reshape(B * T, D_MODEL)
    for l in range(depth):
        x2 = _layer(x2, B, T, rel_bias_table, g_attn[l], w_in[l], b_forget[l], cmp_pe_k[l], cmp_w1_k[l],
                    cmp_w2_k[l], cmp_pe_v[l], cmp_w1_v[l], cmp_w2_v[l], w_br_fox[l], w_br_nsa[l], w_out[l],
                    g_mlp[l], w_ff1[l], w_ff2[l], g_final, final=(l == depth - 1))
    return x2.reshape(B, T, D_MODEL)
```

```python
import functools
import math

import numpy as np
import jax
import jax.numpy as jnp
from jax import lax
from jax.experimental import pallas as pl
from jax.experimental.pallas import tpu as pltpu

F32 = jnp.float32
BF16 = jnp.bfloat16

D_MODEL = 1024
HEAD_DIM = 64
FOX_HEADS = 8
NSA_HEADS = 8
NSA_GROUPS = 2
NSA_REP = NSA_HEADS // NSA_GROUPS
CMP_BLOCK = 32
CMP_STRIDE = 16
CMP_HIDDEN = 256
SLC_BLOCK = 64
SLC_TOPK = 16
WINDOW = 512
REL_BUCKETS = 32
REL_MAX_DIST = 128
D_FF = 4 * D_MODEL
RMS_EPS = 1e-6
NEG_INF = -1e30
FORCED_SCORE = 1e4
SCALE = HEAD_DIM ** -0.5
LOG2E = math.log2(math.e)

FOX_W = FOX_HEADS * HEAD_DIM
NSA_W = NSA_HEADS * HEAD_DIM
KV_W = NSA_GROUPS * HEAD_DIM
LANES = 128
NSA_TILE = 256
REL_FAR = 113
CMP_BAND = 32

VMEM_LIMIT = 56 * 1024 * 1024

_NT = (((1,), (1,)), ((), ()))


def _dot(a, b):
    return jnp.dot(a, b, preferred_element_type=F32)


def _dot_nt(a, b):
    return lax.dot_general(a, b, _NT, preferred_element_type=F32)


def _split3(x):
    hi = x.astype(BF16)
    r1 = x - hi.astype(F32)
    mid = r1.astype(BF16)
    lo = (r1 - mid.astype(F32)).astype(BF16)
    return hi, mid, lo


def _rms(x, g):
    return x * lax.rsqrt(jnp.mean(x * x, axis=-1, keepdims=True) + RMS_EPS) * g


def _rel_bucket_np(n):
    exact = REL_BUCKETS // 2
    nf = np.maximum(n, exact).astype(np.float64)
    log_b = exact + (np.log(nf / exact) / math.log(REL_MAX_DIST / exact) * (REL_BUCKETS - exact)).astype(np.int64)
    return np.where(n < exact, n, np.minimum(log_b, REL_BUCKETS - 1))


_SEG_FQ, _SEG_FK, _SEG_NQ, _SEG_NK, _SEG_FV, _SEG_NV, _SEG_CKV, _SEG_MISC = (
    0, 512, 1024, 1536, 1792, 2304, 2560, 2816)
_W_ALL = 2944


def _inproj_kernel(x_ref, g_ref, w_ref, fq_ref, fk_ref, nq_ref, nk_ref, fv_ref, nv_ref, ckv_ref, misc_ref, *, tm, T):
    h = _rms(x_ref[...], g_ref[...]).astype(BF16)

    def seg(a, n):
        return _dot(h, w_ref[:, a:a + n])

    lane = lax.broadcasted_iota(jnp.int32, (1, LANES), 1)

    def heads_wide(ref, a, n_heads, mult, upper):
        r = seg(a, n_heads * HEAD_DIM)
        for j in range(n_heads):
            pair = r[:, (j // 2) * LANES:(j // 2 + 1) * LANES]
            if j % 2:
                pair = pltpu.roll(pair, HEAD_DIM, 1)
            ref[j] = jnp.where(lane < HEAD_DIM, pair * mult, upper(j)).astype(BF16)

    q_upper = jnp.where((lane == DECAY_LANES[0]) | (lane == DECAY_LANES[1]) | (lane == DECAY_LANES[2]), -1.0, 0.0)
    heads_wide(fq_ref, _SEG_FQ, FOX_HEADS, LOG2E, lambda j: q_upper)
    heads_wide(fk_ref, _SEG_FK, FOX_HEADS, 1.0, lambda j: 0.0)
    heads_wide(nq_ref, _SEG_NQ, NSA_HEADS, LOG2E, lambda j: 0.0)
    t = (pl.program_id(0) * tm + lax.broadcasted_iota(jnp.int32, (tm, 1), 0)) % T
    block_onehot = jnp.where(lane - HEAD_DIM == t // SLC_BLOCK, 1.0, 0.0)
    heads_wide(nk_ref, _SEG_NK, 2 * NSA_GROUPS, 1.0, lambda j: block_onehot if j < NSA_GROUPS else 0.0)
    fv_ref[...] = seg(_SEG_FV, FOX_W).astype(BF16).T
    nv_ref[...] = seg(_SEG_NV, 2 * KV_W).astype(BF16).T
    ckv_ref[...] = seg(_SEG_CKV, 2 * KV_W).astype(BF16)
    misc_ref[...] = seg(_SEG_MISC, LANES)


def _in_proj(x2, g, w_all, tm, T):
    M = x2.shape[0]
    assert T % tm == 0
    row = lambda i: (i, 0)
    hrow = lambda i: (0, i, 0)
    return pl.pallas_call(
        functools.partial(_inproj_kernel, tm=tm, T=T),
        grid=(M // tm,),
        in_specs=[pl.BlockSpec((tm, D_MODEL), row),
                  pl.BlockSpec((1, D_MODEL), lambda i: (0, 0)),
                  pl.BlockSpec((D_MODEL, _W_ALL), lambda i: (0, 0))],
        out_specs=[pl.BlockSpec((FOX_HEADS, tm, LANES), hrow),
                   pl.BlockSpec((FOX_HEADS, tm, LANES), hrow),
                   pl.BlockSpec((NSA_HEADS, tm, LANES), hrow),
                   pl.BlockSpec((2 * NSA_GROUPS, tm, LANES), hrow),
                   pl.BlockSpec((FOX_W, tm), lambda i: (0, i)),
                   pl.BlockSpec((2 * KV_W, tm), lambda i: (0, i)),
                   pl.BlockSpec((tm, 2 * KV_W), row),
                   pl.BlockSpec((tm, LANES), row)],
        out_shape=[jax.ShapeDtypeStruct((FOX_HEADS, M, LANES), BF16),
                   jax.ShapeDtypeStruct((FOX_HEADS, M, LANES), BF16),
                   jax.ShapeDtypeStruct((NSA_HEADS, M, LANES), BF16),
                   jax.ShapeDtypeStruct((2 * NSA_GROUPS, M, LANES), BF16),
                   jax.ShapeDtypeStruct((FOX_W, M), BF16),
                   jax.ShapeDtypeStruct((2 * KV_W, M), BF16),
                   jax.ShapeDtypeStruct((M, 2 * KV_W), BF16),
                   jax.ShapeDtypeStruct((M, LANES), F32)],
        compiler_params=pltpu.CompilerParams(dimension_semantics=("parallel",), vmem_limit_bytes=VMEM_LIMIT),
        name="in_proj",
    )(x2, g, w_all)


_SCAN_BLK = 256


DECAY_LANES = (HEAD_DIM, HEAD_DIM + FOX_HEADS, HEAD_DIM + 2 * FOX_HEADS)


def _decay_kernel(misc_ref, bf_ref, c3_ref, *, T):
    r = lax.broadcasted_iota(jnp.int32, (_SCAN_BLK, _SCAN_BLK), 0)
    c = lax.broadcasted_iota(jnp.int32, (_SCAN_BLK, _SCAN_BLK), 1)
    tri = jnp.where(r >= c, 1.0, 0.0).astype(BF16)
    lane = lax.broadcasted_iota(jnp.int32, (1, LANES), 1)

    def blk(n, carry):
        s0 = pl.multiple_of(n * _SCAN_BLK, _SCAN_BLK)
        x = misc_ref[pl.ds(s0, _SCAN_BLK), :] + bf_ref[...]
        lf = jnp.minimum(x, 0.0) - jnp.log1p(jnp.exp(-jnp.abs(x)))
        hi, mid, lo = _split3(lf)
        cs = _dot(tri, hi) + _dot(tri, mid) + _dot(tri, lo) + carry
        hi, mid, lo = (t.astype(F32) for t in _split3(cs * LOG2E))
        c3_ref[pl.ds(s0, _SCAN_BLK), :] = jnp.where(
            lane < FOX_HEADS, hi,
            jnp.where(lane < 2 * FOX_HEADS, pltpu.roll(mid, FOX_HEADS, 1),
                      jnp.where(lane < 3 * FOX_HEADS, pltpu.roll(lo, 2 * FOX_HEADS, 1), 0.0)))
        return cs[_SCAN_BLK - 1:_SCAN_BLK, :]

    lax.fori_loop(0, T // _SCAN_BLK, blk, jnp.zeros((1, LANES), F32))


def _fox_decay(misc, bf_pad, B, T):
    return pl.pallas_call(
        functools.partial(_decay_kernel, T=T),
        grid=(B,),
        in_specs=[pl.BlockSpec((T, LANES), lambda b: (b, 0)),
                  pl.BlockSpec((1, LANES), lambda b: (0, 0))],
        out_specs=pl.BlockSpec((T, LANES), lambda b: (b, 0)),
        out_shape=jax.ShapeDtypeStruct((B * T, LANES), F32),
        compiler_params=pltpu.CompilerParams(dimension_semantics=("parallel",), vmem_limit_bytes=VMEM_LIMIT),
        name="fox_decay",
    )(misc, bf_pad)


_FOX_PREP_BLK = 512


SUM_ROWS = 16


def _with_sum_row(vt):
    r = lax.broadcasted_iota(jnp.int32, (SUM_ROWS, vt.shape[1]), 0)
    return jnp.concatenate([vt, jnp.where(r == 0, 1.0, 0.0).astype(vt.dtype)], axis=0)


def _stage_scores(slot_ref, s):
    slot_ref[...] = s
    return jnp.max(s, axis=0, keepdims=True)


def _softmax_step(slot_ref, smax, vt, carry):
    m, acc = carry
    m_new = jnp.maximum(m, smax)
    p = jnp.exp2(slot_ref[...] - m_new).astype(BF16)
    return m_new, jnp.exp2(m - m_new) * acc + _dot(_with_sum_row(vt), p)


def _fox_kernel(q_ref, k_ref, vt_ref, c3_ref, mask_ref, o_ref, kaug_sc, s_sc, qt_sc, *, tq, tk, T, nh):
    hg = pl.program_id(1)
    i = pl.program_id(2)
    lane = lax.broadcasted_iota(jnp.int32, (1, LANES), 1)
    decay_lane = (lane == DECAY_LANES[0]) | (lane == DECAY_LANES[1]) | (lane == DECAY_LANES[2])

    @pl.when(i == 0)
    def _():
        for hh in range(nh):
            def prep(n, _, hh=hh):
                r0 = pl.multiple_of(n * _FOX_PREP_BLK, _FOX_PREP_BLK)
                c3 = pltpu.roll(c3_ref[pl.ds(r0, _FOX_PREP_BLK), :], HEAD_DIM - (nh * hg + hh), 1)
                kaug_sc[hh, pl.ds(r0, _FOX_PREP_BLK), :] = jnp.where(
                    lane < HEAD_DIM, k_ref[hh, pl.ds(r0, _FOX_PREP_BLK), :],
                    jnp.where(decay_lane, c3, 0.0).astype(BF16))
                return 0

            lax.fori_loop(0, T // _FOX_PREP_BLK, prep, 0)

    for hh in range(nh):
        qt_sc[hh] = q_ref[hh].T

    n_diag = tq // tk
    n_far = i * n_diag
    no_mask = n_diag

    def stage(hh, j, slot, mask_idx):
        ks = pl.multiple_of(j * tk, tk)
        s = _dot(kaug_sc[hh, pl.ds(ks, tk), :], qt_sc[hh])
        if mask_idx is not None:
            s = s + mask_ref[mask_idx]
        return _stage_scores(s_sc.at[slot], s)

    def update(hh, j, slot, smax, carry):
        ks = pl.multiple_of(j * tk, tk)
        return _softmax_step(s_sc.at[slot], smax, vt_ref[hh * HEAD_DIM:(hh + 1) * HEAD_DIM, pl.ds(ks, tk)], carry)

    def far(j, carry):
        state, smax = list(carry[0]), carry[1]
        for hh in range(nh):
            if hh + 1 < nh:
                smax_next = stage(hh + 1, j, (hh + 1) % 2, None)
            else:
                smax_next = stage(0, j + 1, 0, jnp.where(j + 1 == n_far, 0, no_mask))
            state[hh] = update(hh, j, hh % 2, smax, state[hh])
            smax = smax_next
        return tuple(state), smax

    init = (jnp.full((1, tq), NEG_INF, F32), jnp.zeros((HEAD_DIM + SUM_ROWS, tq), F32))
    smax = stage(0, 0, 0, jnp.where(n_far == 0, 0, no_mask))
    state, smax = lax.fori_loop(0, n_far, far, ((init,) * nh, smax))
    state = list(state)
    steps = [(hh, d) for d in range(n_diag) for hh in range(nh)]
    for n, (hh, d) in enumerate(steps):
        if n + 1 < len(steps):
            hh2, d2 = steps[n + 1]
            smax_next = stage(hh2, n_far + d2, (n + 1) % 2, d2)
        state[hh] = update(hh, n_far + d, n % 2, smax, state[hh])
        smax = smax_next
    outs = [acc[0:HEAD_DIM] * (1.0 / acc[HEAD_DIM:HEAD_DIM + 1]) for _, acc in state]
    o_ref[...] = jnp.concatenate(outs, axis=0).T.astype(BF16)


def _fox_attention(fq, fk, fvt, c3, B, T, tq, tk, nh):
    assert nh % 2 == 0 and FOX_HEADS % nh == 0
    nT = T // tq
    n_diag = tq // tk
    kk = np.arange(tk)[None, :, None] + tk * np.arange(n_diag)[:, None, None]
    mask = np.where(kk <= np.arange(tq)[None, None, :], 0.0, NEG_INF).astype(np.float32)
    mask = jnp.asarray(np.concatenate([mask, np.zeros((1, tk, tq), np.float32)]))
    return pl.pallas_call(
        functools.partial(_fox_kernel, tq=tq, tk=tk, T=T, nh=nh),
        grid=(B, FOX_HEADS // nh, nT),
        in_specs=[pl.BlockSpec((nh, tq, LANES), lambda b, hp, i: (hp, b * nT + i, 0)),
                  pl.BlockSpec((nh, T, LANES), lambda b, hp, i: (hp, b, 0)),
                  pl.BlockSpec((nh * HEAD_DIM, T), lambda b, hp, i: (hp, b)),
                  pl.BlockSpec((T, LANES), lambda b, hp, i: (b, 0)),
                  pl.BlockSpec((n_diag + 1, tk, tq), lambda b, hp, i: (0, 0, 0))],
        out_specs=pl.BlockSpec((tq, nh * HEAD_DIM), lambda b, hp, i: (b * nT + i, hp)),
        out_shape=jax.ShapeDtypeStruct((B * T, FOX_W), BF16),
        scratch_shapes=[pltpu.VMEM((nh, T, LANES), BF16),
                        pltpu.VMEM((2, tk, tq), F32),
                        pltpu.VMEM((nh, LANES, tq), BF16)],
        compiler_params=pltpu.CompilerParams(dimension_semantics=("parallel", "parallel", "arbitrary"),
                                             vmem_limit_bytes=VMEM_LIMIT),
        name="fox_attn",
    )(fq, fk, fvt, c3, mask)


def _compress_kernel(r_ref, w1_ref, pe_ref, w2k_ref, w2v_ref, kc_ref, vc_ref, *, NR):
    half = CMP_STRIDE * HEAD_DIM
    vacc = jnp.zeros((NR, LANES), F32)
    for idx in range(2 * NSA_GROUPS):
        kind, g = divmod(idx, NSA_GROUPS)
        rm = r_ref[0, idx]
        a = _dot(rm, w1_ref[kind, 0:half, :])
        bm = _dot(rm, w1_ref[kind, half:2 * half, :])
        pe_term = _dot(pe_ref[kind], w1_ref[kind])[0:1, :]
        pre = a + pltpu.roll(bm, NR - 1, 0) + pe_term
        hid = (pre * jax.nn.sigmoid(pre)).astype(BF16)
        if kind == 0:
            kc_ref[0, g] = _dot(hid, w2k_ref[...]).astype(BF16)
        else:
            vacc = vacc + _dot(hid, w2v_ref[g])
    vc_ref[0] = vacc.astype(BF16)


def _compress(r4, w1, pe, w2k, w2v_pad, B, NR):
    return pl.pallas_call(
        functools.partial(_compress_kernel, NR=NR),
        grid=(B,),
        in_specs=[pl.BlockSpec((1, 2 * NSA_GROUPS, NR, CMP_STRIDE * HEAD_DIM), lambda b: (b, 0, 0, 0)),
                  pl.BlockSpec((2, CMP_BLOCK * HEAD_DIM, CMP_HIDDEN), lambda b: (0, 0, 0)),
                  pl.BlockSpec((2, 16, CMP_BLOCK * HEAD_DIM), lambda b: (0, 0, 0)),
                  pl.BlockSpec((CMP_HIDDEN, LANES), lambda b: (0, 0)),
                  pl.BlockSpec((NSA_GROUPS, CMP_HIDDEN, LANES), lambda b: (0, 0, 0))],
        out_specs=[pl.BlockSpec((1, NSA_GROUPS, NR, LANES), lambda b: (b, 0, 0, 0)),
                   pl.BlockSpec((1, NR, LANES), lambda b: (b, 0, 0))],
        out_shape=[jax.ShapeDtypeStruct((B, NSA_GROUPS, NR, LANES), BF16),
                   jax.ShapeDtypeStruct((B, NR, LANES), BF16)],
        compiler_params=pltpu.CompilerParams(dimension_semantics=("parallel",), vmem_limit_bytes=VMEM_LIMIT),
        name="compress",
    )(r4, w1, pe, w2k, w2v_pad)


def _cmp_kernel(q_ref, kc_ref, vc_ref, bt_ref, misc_ref, ocmp_ref, qaug_ref, bias_sc, val_sc, *, tq, NR):
    i = pl.program_id(0)
    b = pl.program_id(1)
    t0 = i * tq
    n_slc = LANES // 2
    gates = jax.nn.sigmoid(misc_ref[...])

    @pl.when(b == 0)
    def _():
        f = lax.broadcasted_iota(jnp.int32, (LANES, NR), 0)
        c = lax.broadcasted_iota(jnp.int32, (LANES, NR), 1)
        place = (((f % CMP_BAND) == (c - t0 // CMP_STRIDE + CMP_BAND // 2)) & (f < 3 * CMP_BAND))
        place = jnp.where(place, 1.0, 0.0).astype(BF16)
        for h in range(NSA_HEADS):
            bias_sc[h] = _dot(bt_ref[h], place)

    t = t0 + lax.broadcasted_iota(jnp.int32, (tq, 1), 0)
    c = lax.broadcasted_iota(jnp.int32, (1, NR), 1)
    cmask = (c * CMP_STRIDE + (CMP_BLOCK - 1)) <= t
    lane = lax.broadcasted_iota(jnp.int32, (1, LANES), 1)

    jj = lax.broadcasted_iota(jnp.int32, (LANES, NR), 0)
    cc = lax.broadcasted_iota(jnp.int32, (LANES, NR), 1)
    ov = ((cc * CMP_STRIDE < jj * SLC_BLOCK + SLC_BLOCK) & (cc * CMP_STRIDE + CMP_BLOCK > jj * SLC_BLOCK)
          & (jj < n_slc) & (cc < NR - 1))
    ov = jnp.where(ov, 1.0, 0.0).astype(BF16)

    jrow = lax.broadcasted_iota(jnp.int32, (n_slc, tq), 0)
    tt = t0 + lax.broadcasted_iota(jnp.int32, (n_slc, tq), 1)
    cur = tt // SLC_BLOCK
    forced = (jrow == 0) | (jrow == cur) | (jrow == cur - 1)
    valid = jrow * SLC_BLOCK <= tt
    jcol = lax.broadcasted_iota(jnp.int32, (n_slc, 1), 0)

    outs = []
    for g in range(NSA_GROUPS):
        qs = q_ref[g * NSA_REP:(g + 1) * NSA_REP].reshape(NSA_REP * tq, LANES)
        s = _dot_nt(qs, kc_ref[0, g]).reshape(NSA_REP, tq, NR) + bias_sc[g * NSA_REP:(g + 1) * NSA_REP]
        s = jnp.where(cmask, s, NEG_INF)
        m = jnp.max(s, axis=-1, keepdims=True)
        e = jnp.where(cmask, jnp.exp2(s - m), 0.0)
        l = jnp.sum(e, axis=-1, keepdims=True)
        p = e * (1.0 / jnp.where(l > 0.0, l, 1.0))
        o = _dot(p.reshape(NSA_REP * tq, NR).astype(BF16), vc_ref[0])
        outs.append(o.reshape(NSA_REP, tq, LANES))

        hi, mid, lo = _split3(jnp.sum(p, axis=0))
        imp = (_dot_nt(ov, hi) + _dot_nt(ov, mid) + _dot_nt(ov, lo))[0:n_slc, :]
        val_sc[...] = jnp.where(forced, FORCED_SCORE, jnp.where(valid, imp, -1.0))
        vals = [val_sc[8 * a:8 * a + 8, :] for a in range(n_slc // 8)]
        ranks = [jnp.zeros((8, tq), F32) for _ in vals]
        j8 = lax.broadcasted_iota(jnp.int32, (8, 1), 0)
        for k in range(n_slc):
            vk = val_sc[k:k + 1, :]
            for a, va in enumerate(vals):
                if 8 * a > k:
                    ahead = jnp.where(vk >= va, 1.0, 0.0)
                elif 8 * a + 7 < k:
                    ahead = jnp.where(vk > va, 1.0, 0.0)
                else:
                    tie = jnp.where(j8 > k - 8 * a, 1.0, 0.0)
                    ahead = jnp.where(vk > va, 1.0, jnp.where(vk == va, tie, 0.0))
                ranks[a] = ranks[a] + ahead
        rank = jnp.concatenate(ranks, axis=0)
        selneg = jnp.where(rank < float(SLC_TOPK), 0.0, NEG_INF)
        selneg = jnp.concatenate([jnp.zeros((LANES - n_slc, tq), F32), selneg], axis=0).T.astype(BF16)
        for r in range(NSA_REP):
            h = g * NSA_REP + r
            qaug_ref[h] = jnp.where(lane < HEAD_DIM, q_ref[h], selneg)

    def gate_col(h):
        c0 = FOX_HEADS + 3 * h
        return gates[:, c0:c0 + 1]

    for r in range(NSA_REP):
        ocmp_ref[:, r * LANES:(r + 1) * LANES] = jnp.where(
            lane < HEAD_DIM, outs[0][r] * gate_col(r), outs[1][r] * gate_col(NSA_REP + r))


def _nsa_cmp(nq, kc, vc, bt, misc, B, T, NR):
    tq = NSA_TILE
    nT = T // tq
    M = B * T
    return pl.pallas_call(
        functools.partial(_cmp_kernel, tq=tq, NR=NR),
        grid=(nT, B),
        in_specs=[pl.BlockSpec((NSA_HEADS, tq, LANES), lambda i, b: (0, b * nT + i, 0)),
                  pl.BlockSpec((1, NSA_GROUPS, NR, LANES), lambda i, b: (b, 0, 0, 0)),
                  pl.BlockSpec((1, NR, LANES), lambda i, b: (b, 0, 0)),
                  pl.BlockSpec((NSA_HEADS, tq, LANES), lambda i, b: (0, 0, 0)),
                  pl.BlockSpec((tq, LANES), lambda i, b: (b * nT + i, 0))],
        out_specs=[pl.BlockSpec((tq, NSA_REP * LANES), lambda i, b: (b * nT + i, 0)),
                   pl.BlockSpec((NSA_HEADS, tq, LANES), lambda i, b: (0, b * nT + i, 0))],
        out_shape=[jax.ShapeDtypeStruct((M, NSA_REP * LANES), F32),
                   jax.ShapeDtypeStruct((NSA_HEADS, M, LANES), BF16)],
        scratch_shapes=[pltpu.VMEM((NSA_HEADS, tq, NR), F32),
                        pltpu.VMEM((LANES // 2, tq), F32)],
        compiler_params=pltpu.CompilerParams(dimension_semantics=("arbitrary", "arbitrary"),
                                             vmem_limit_bytes=VMEM_LIMIT),
        name="nsa_cmp",
    )(nq, kc, vc, bt, misc)


def _nsa_kernel(q_ref, k_ref, vt_ref, ocmp_ref, misc_ref, dist_ref, o_ref, s_sc, qt_sc, bias_ref, wmask_ref, *, tq):
    i = pl.program_id(1)
    tk = tq
    nq = NSA_REP * tq
    jp = jnp.maximum(i - 1, 0)
    jf = jnp.maximum(i - 2, 0)

    @pl.when((pl.program_id(0) == 0) & (i == 0))
    def _():
        kk = lax.broadcasted_iota(jnp.int32, (tk, tq), 0)
        qq = lax.broadcasted_iota(jnp.int32, (tk, tq), 1)
        for h in range(NSA_HEADS):
            g, r = divmod(h, NSA_REP)
            rows = jnp.broadcast_to(dist_ref[h:h + 1, :], (tk, 2 * tq))
            pair = pltpu.roll(rows, 0, 1, stride=1, stride_axis=0)
            bias_ref[g, 0, :, r * tq:(r + 1) * tq] = jnp.where(kk <= qq, pair[:, 0:tq], NEG_INF)
            bias_ref[g, 1, :, r * tq:(r + 1) * tq] = pair[:, tq:2 * tq]
        for g in range(NSA_GROUPS):
            bias_ref[g, 2] = jnp.full((tk, nq), NEG_INF, F32)
            bias_ref[g, 3] = jnp.zeros((tk, nq), F32)
        for r in range(NSA_REP):
            wmask_ref[0, :, r * tq:(r + 1) * tq] = jnp.where(kk > qq, 0.0, NEG_INF)
        wmask_ref[1] = jnp.full((tk, nq), NEG_INF, F32)

    for g in range(NSA_GROUPS):
        qt_sc[g] = q_ref[g * NSA_REP:(g + 1) * NSA_REP].reshape(nq, LANES).T

    def stage(g, branch, j, slot, bias):
        ks = pl.multiple_of(j * tk, tk)
        s = _dot(k_ref[branch * NSA_GROUPS + g, pl.ds(ks, tk), :], qt_sc[g])
        if bias is not None:
            s = s + bias
        return _stage_scores(s_sc.at[slot], s)

    def update(g, branch, j, slot, smax, carry):
        ks = pl.multiple_of(j * tk, tk)
        r0 = (branch * NSA_GROUPS + g) * HEAD_DIM
        return _softmax_step(s_sc.at[slot], smax, vt_ref[r0:r0 + HEAD_DIM, pl.ds(ks, tk)], carry)

    init = (jnp.full((1, nq), NEG_INF, F32), jnp.zeros((HEAD_DIM + SUM_ROWS, nq), F32))
    prev_tile = jnp.where(i >= 1, 1, 2)

    def far(j, carry):
        c0, c1, smax0 = carry
        smax1 = stage(1, 0, j, 1, None)
        c0 = update(0, 0, j, 0, smax0, c0)
        smax0 = stage(0, 0, j + 1, 0, bias_ref[0, jnp.where(j + 1 == jp, 1, 3)])
        c1 = update(1, 0, j, 1, smax1, c1)
        return c0, c1, smax0

    smax = stage(0, 0, 0, 0, bias_ref[0, jnp.where(i >= 2, 3, prev_tile)])
    c0, c1, smax = lax.fori_loop(0, jp, far, (init, init, smax))
    c_slc = [c0, c1]
    c_win = [init, init]

    near = [(0, jp, lambda g: bias_ref[g, prev_tile]),
            (0, i, lambda g: bias_ref[g, 0]),
            (1, jf, lambda g: wmask_ref[jnp.where(i >= 2, 0, 1)]),
            (1, jp, lambda g: bias_ref[g, prev_tile]),
            (1, i, lambda g: bias_ref[g, 0])]
    steps = [(g, branch, j, bias) for (branch, j, bias) in near for g in range(NSA_GROUPS)]
    for n, (g, branch, j, _) in enumerate(steps):
        if n + 1 < len(steps):
            g2, branch2, j2, bias2 = steps[n + 1]
            smax_next = stage(g2, branch2, j2, (n + 1) % 2, bias2(g2))
        state = c_slc if branch == 0 else c_win
        state[g] = update(g, branch, j, n % 2, smax, state[g])
        smax = smax_next

    gates = jax.nn.sigmoid(misc_ref[...].T)
    for r in range(NSA_REP):
        halves = []
        for g in range(NSA_GROUPS):
            c0 = FOX_HEADS + 3 * (g * NSA_REP + r)
            sl = slice(r * tq, (r + 1) * tq)
            (_, accs), (_, accw) = c_slc[g], c_win[g]
            halves.append(
                accs[0:HEAD_DIM, sl] * (gates[c0 + 1:c0 + 2, :] / accs[HEAD_DIM:HEAD_DIM + 1, sl])
                + accw[0:HEAD_DIM, sl] * (gates[c0 + 2:c0 + 3, :] / accw[HEAD_DIM:HEAD_DIM + 1, sl]))
        o = jnp.concatenate(halves, axis=0).T + ocmp_ref[:, r * LANES:(r + 1) * LANES]
        o_ref[:, r * LANES:(r + 1) * LANES] = o.astype(BF16)


def _nsa_attention(qaug, nk, nvt, ocmp, misc, dist_bias, B, T):
    tq = NSA_TILE
    assert WINDOW == 2 * tq
    nT = T // tq
    M = B * T
    return pl.pallas_call(
        functools.partial(_nsa_kernel, tq=tq),
        grid=(B, nT),
        in_specs=[pl.BlockSpec((NSA_HEADS, tq, LANES), lambda b, i: (0, b * nT + i, 0)),
                  pl.BlockSpec((2 * NSA_GROUPS, T, LANES), lambda b, i: (0, b, 0)),
                  pl.BlockSpec((2 * KV_W, T), lambda b, i: (0, b)),
                  pl.BlockSpec((tq, NSA_REP * LANES), lambda b, i: (b * nT + i, 0)),
                  pl.BlockSpec((tq, LANES), lambda b, i: (b * nT + i, 0)),
                  pl.BlockSpec((NSA_HEADS, 2 * tq), lambda b, i: (0, 0))],
        out_specs=pl.BlockSpec((tq, NSA_REP * LANES), lambda b, i: (b * nT + i, 0)),
        out_shape=jax.ShapeDtypeStruct((M, NSA_W), BF16),
        scratch_shapes=[pltpu.VMEM((2, tq, NSA_REP * tq), F32),
                        pltpu.VMEM((NSA_GROUPS, LANES, NSA_REP * tq), BF16),
                        pltpu.VMEM((NSA_GROUPS, 4, tq, NSA_REP * tq), F32),
                        pltpu.VMEM((2, tq, NSA_REP * tq), F32)],
        compiler_params=pltpu.CompilerParams(dimension_semantics=("arbitrary", "arbitrary"),
                                             vmem_limit_bytes=VMEM_LIMIT),
        name="nsa_attn",
    )(qaug, nk, nvt, ocmp, misc, dist_bias)


def _merge_kernel(x_ref, g_ref, of_ref, on_ref, wga_ref, wgb_ref, wbf_ref, wbn_ref, wo_ref, x1_ref, *, sub):
    for r0 in range(0, x_ref.shape[0], sub):
        rows = slice(r0, r0 + sub)
        x = x_ref[rows, :]
        h = _rms(x, g_ref[...]).astype(BF16)
        ga = jax.nn.sigmoid(_dot(h, wga_ref[...]))
        gb = jax.nn.sigmoid(_dot(h, wgb_ref[...]))
        merged = ga * _dot(of_ref[rows, :], wbf_ref[...]) + gb * _dot(on_ref[rows, :], wbn_ref[...])
        x1_ref[rows, :] = x + _dot(merged.astype(BF16), wo_ref[...])


def _merge(x2, g, o_fox, o_nsa, wga, wgb, wbf, wbn, wo, tm):
    M = x2.shape[0]
    row = lambda i: (i, 0)
    full = lambda i: (0, 0)
    return pl.pallas_call(
        functools.partial(_merge_kernel, sub=tm // 2),
        grid=(M // tm,),
        in_specs=[pl.BlockSpec((tm, D_MODEL), row),
                  pl.BlockSpec((1, D_MODEL), full),
                  pl.BlockSpec((tm, FOX_W), row),
                  pl.BlockSpec((tm, NSA_W), row),
                  pl.BlockSpec((D_MODEL, D_MODEL), full),
                  pl.BlockSpec((D_MODEL, D_MODEL), full),
                  pl.BlockSpec((FOX_W, D_MODEL), full),
                  pl.BlockSpec((NSA_W, D_MODEL), full),
                  pl.BlockSpec((D_MODEL, D_MODEL), full)],
        out_specs=pl.BlockSpec((tm, D_MODEL), row),
        out_shape=jax.ShapeDtypeStruct((M, D_MODEL), F32),
        compiler_params=pltpu.CompilerParams(dimension_semantics=("parallel",), vmem_limit_bytes=VMEM_LIMIT),
        name="merge",
    )(x2, g, o_fox, o_nsa, wga, wgb, wbf, wbn, wo)


def _mlp_kernel(x_ref, g_ref, w1_ref, w2_ref, gf_ref, o_ref, *, final, sub):
    for r0 in range(0, x_ref.shape[0], sub):
        x = x_ref[r0:r0 + sub, :]
        h = _rms(x, g_ref[...]).astype(BF16)
        u = jnp.maximum(_dot(h, w1_ref[...]), 0.0)
        y = x + _dot((u * u).astype(BF16), w2_ref[...])
        o_ref[r0:r0 + sub, :] = _rms(y, gf_ref[...]) if final else y


def _mlp(x1, g, w1, w2, gf, tm, sub, final):
    M = x1.shape[0]
    row = lambda i: (i, 0)
    full = lambda i: (0, 0)
    once = pl.Buffered(1)
    return pl.pallas_call(
        functools.partial(_mlp_kernel, final=final, sub=sub),
        grid=(M // tm,),
        in_specs=[pl.BlockSpec((tm, D_MODEL), row),
                  pl.BlockSpec((1, D_MODEL), full),
                  pl.BlockSpec((D_MODEL, D_FF), full, pipeline_mode=once),
                  pl.BlockSpec((D_FF, D_MODEL), full, pipeline_mode=once),
                  pl.BlockSpec((1, D_MODEL), full)],
        out_specs=pl.BlockSpec((tm, D_MODEL), row),
        out_shape=jax.ShapeDtypeStruct((M, D_MODEL), F32),
        compiler_params=pltpu.CompilerParams(dimension_semantics=("parallel",), vmem_limit_bytes=VMEM_LIMIT),
        name="mlp",
    )(x1, g, w1, w2, gf)


def _bias_tables(rel_tbl, tq):
    L = 2 * tq
    g = (rel_tbl - rel_tbl[REL_BUCKETS - 1]).T[:, _rel_bucket_np(np.arange(L))] * LOG2E
    base = CMP_STRIDE * (CMP_BAND // 2) - (CMP_BLOCK - 1)
    left = CMP_STRIDE * (CMP_BAND - 1) - base
    gp = jnp.concatenate([jnp.zeros((NSA_HEADS, left), F32), g], axis=1)
    band = jnp.stack([gp[:, left + base - CMP_STRIDE * f:left + base - CMP_STRIDE * f + tq] for f in range(CMP_BAND)],
                     axis=-1)
    return g, band


def _layer(x2, B, T, rel_tbl, g_attn, w_in, b_forget, pe_k, w1_k, w2_k, pe_v, w1_v, w2_v,
           w_br_fox, w_br_nsa, w_out, g_mlp, w_ff1, w_ff2, g_final, final):
    M = B * T
    NR = T // CMP_STRIDE
    offs = np.cumsum((FOX_W, FOX_W, FOX_W, FOX_HEADS, NSA_W, KV_W, KV_W, KV_W, KV_W, KV_W, KV_W,
                      3 * NSA_HEADS, D_MODEL, D_MODEL))
    (w_fq, w_fk, w_fv, w_fl, w_nq, w_kc, w_vc, w_ksl, w_vsl, w_kwn, w_vwn, w_ng, w_ga, w_gb) = jnp.split(
        w_in, offs[:-1].tolist(), axis=-1)
    w_all = jnp.concatenate(
        [w_fq * SCALE, w_fk, w_nq * SCALE, w_ksl, w_kwn, w_fv, w_vsl, w_vwn, w_kc, w_vc, w_fl, w_ng,
         jnp.zeros((D_MODEL, _W_ALL - _SEG_MISC - FOX_HEADS - 3 * NSA_HEADS), F32)], axis=-1).astype(BF16)

    fq, fk, nq, nk, fv, nv, ckv, misc = _in_proj(x2, g_attn.reshape(1, D_MODEL), w_all, tm=512, T=T)

    bf_pad = jnp.zeros((1, LANES), F32).at[0, :FOX_HEADS].set(b_forget)
    c3 = _fox_decay(misc, bf_pad, B, T)
    o_fox = _fox_attention(fq, fk, fv, c3, B, T, tq=512, tk=512, nh=4)

    r4 = ckv.reshape(B, NR, CMP_STRIDE, 2 * NSA_GROUPS, HEAD_DIM).transpose(0, 3, 1, 2, 4)
    r4 = r4.reshape(B, 2 * NSA_GROUPS, NR, CMP_STRIDE * HEAD_DIM)
    w1 = jnp.stack([w1_k, w1_v]).astype(BF16)
    pe = jnp.stack([pe_k.reshape(1, -1), pe_v.reshape(1, -1)]).astype(BF16)
    pe = jnp.broadcast_to(pe, (2, 16, CMP_BLOCK * HEAD_DIM))
    zpad = jnp.zeros((CMP_HIDDEN, HEAD_DIM), F32)
    w2v_pad = jnp.stack([jnp.concatenate([w2_v, zpad], axis=1), jnp.concatenate([zpad, w2_v], axis=1)]).astype(BF16)
    w2k_pad = jnp.concatenate([w2_k, zpad], axis=1).astype(BF16)
    kc, vc = _compress(r4, w1, pe, w2k_pad, w2v_pad, B, NR)

    dist_bias, band = _bias_tables(rel_tbl, NSA_TILE)
    hi, mid, lo = _split3(band)
    bt = jnp.concatenate([hi, mid, lo, jnp.zeros_like(hi)], axis=-1)

    ocmp, qaug = _nsa_cmp(nq, kc, vc, bt, misc, B, T, NR)
    o_nsa = _nsa_attention(qaug, nk, nv, ocmp, misc, dist_bias, B, T)

    wbn = w_br_nsa.reshape(NSA_GROUPS, NSA_REP, HEAD_DIM, D_MODEL).transpose(1, 0, 2, 3).reshape(NSA_W, D_MODEL)
    x1 = _merge(x2, g_attn.reshape(1, D_MODEL), o_fox, o_nsa, w_ga.astype(BF16), w_gb.astype(BF16),
                w_br_fox.astype(BF16), wbn.astype(BF16), w_out.astype(BF16), tm=512)
    return _mlp(x1, g_mlp.reshape(1, D_MODEL), w_ff1.astype(BF16), w_ff2.astype(BF16),
                g_final.reshape(1, D_MODEL), tm=512, sub=256, final=final)


def kernel(x, rel_bias_table, g_attn, w_in, b_forget, cmp_pe_k, cmp_w1_k, cmp_w2_k, cmp_pe_v, cmp_w1_v, cmp_w2_v,
           w_br_fox, w_br_nsa, w_out, g_mlp, w_ff1, w_ff2, g_final):
    B, T, _ = x.shape
    depth = g_attn.shape[0]
    x2 = x.reshape(B * T, D_MODEL)
    for l in range(depth):
        x2 = _layer(x2, B, T, rel_bias_table, g_attn[l], w_in[l], b_forget[l], cmp_pe_k[l], cmp_w1_k[l],
                    cmp_w2_k[l], cmp_pe_v[l], cmp_w1_v[l], cmp_w2_v[l], w_br_fox[l], w_br_nsa[l], w_out[l],
                    g_mlp[l], w_ff1[l], w_ff2[l], g_final, final=(l == depth - 1))
    return x2.reshape(B, T, D_MODEL)
```

```python
import functools
import math

import numpy as np
import jax
import jax.numpy as jnp
from jax import lax
from jax.experimental import pallas as pl
from jax.experimental.pallas import tpu as pltpu

F32 = jnp.float32
BF16 = jnp.bfloat16

D_MODEL = 1024
HEAD_DIM = 64
FOX_HEADS = 8
NSA_HEADS = 8
NSA_GROUPS = 2
NSA_REP = NSA_HEADS // NSA_GROUPS
CMP_BLOCK = 32
CMP_STRIDE = 16
CMP_HIDDEN = 256
SLC_BLOCK = 64
SLC_TOPK = 16
WINDOW = 512
REL_BUCKETS = 32
REL_MAX_DIST = 128
D_FF = 4 * D_MODEL
RMS_EPS = 1e-6
NEG_INF = -1e30
FORCED_SCORE = 1e4
SCALE = HEAD_DIM ** -0.5
LOG2E = math.log2(math.e)

FOX_W = FOX_HEADS * HEAD_DIM
NSA_W = NSA_HEADS * HEAD_DIM
KV_W = NSA_GROUPS * HEAD_DIM
LANES = 128
NSA_TILE = 256
REL_FAR = 113
CMP_BAND = 32

VMEM_LIMIT = 56 * 1024 * 1024

_NT = (((1,), (1,)), ((), ()))


def _dot(a, b):
    return jnp.dot(a, b, preferred_element_type=F32)


def _dot_nt(a, b):
    return lax.dot_general(a, b, _NT, preferred_element_type=F32)


def _split3(x):
    hi = x.astype(BF16)
    r1 = x - hi.astype(F32)
    mid = r1.astype(BF16)
    lo = (r1 - mid.astype(F32)).astype(BF16)
    return hi, mid, lo


def _rms(x, g):
    return x * lax.rsqrt(jnp.mean(x * x, axis=-1, keepdims=True) + RMS_EPS) * g


def _rel_bucket_np(n):
    exact = REL_BUCKETS // 2
    nf = np.maximum(n, exact).astype(np.float64)
    log_b = exact + (np.log(nf / exact) / math.log(REL_MAX_DIST / exact) * (REL_BUCKETS - exact)).astype(np.int64)
    return np.where(n < exact, n, np.minimum(log_b, REL_BUCKETS - 1))


_SEG_FQ, _SEG_FK, _SEG_NQ, _SEG_NK, _SEG_FV, _SEG_NV, _SEG_CKV, _SEG_MISC = (
    0, 512, 1024, 1536, 1792, 2304, 2560, 2816)
_W_ALL = 2944


def _inproj_kernel(x_ref, g_ref, w_ref, fq_ref, fk_ref, nq_ref, nk_ref, fv_ref, nv_ref, ckv_ref, misc_ref, *, tm, T):
    h = _rms(x_ref[...], g_ref[...]).astype(BF16)

    def seg(a, n):
        return _dot(h, w_ref[:, a:a + n])

    lane = lax.broadcasted_iota(jnp.int32, (1, LANES), 1)

    def heads_wide(ref, a, n_heads, mult, upper):
        r = seg(a, n_heads * HEAD_DIM)
        for j in range(n_heads):
            pair = r[:, (j // 2) * LANES:(j // 2 + 1) * LANES]
            if j % 2:
                pair = pltpu.roll(pair, HEAD_DIM, 1)
            ref[j] = jnp.where(lane < HEAD_DIM, pair * mult, upper(j)).astype(BF16)

    q_upper = jnp.where((lane == DECAY_LANES[0]) | (lane == DECAY_LANES[1]) | (lane == DECAY_LANES[2]), -1.0, 0.0)
    heads_wide(fq_ref, _SEG_FQ, FOX_HEADS, LOG2E, lambda j: q_upper)
    heads_wide(fk_ref, _SEG_FK, FOX_HEADS, 1.0, lambda j: 0.0)
    heads_wide(nq_ref, _SEG_NQ, NSA_HEADS, LOG2E, lambda j: 0.0)
    t = (pl.program_id(0) * tm + lax.broadcasted_iota(jnp.int32, (tm, 1), 0)) % T
    block_onehot = jnp.where(lane - HEAD_DIM == t // SLC_BLOCK, 1.0, 0.0)
    heads_wide(nk_ref, _SEG_NK, 2 * NSA_GROUPS, 1.0, lambda j: block_onehot if j < NSA_GROUPS else 0.0)
    fv_ref[...] = seg(_SEG_FV, FOX_W).astype(BF16).T
    nv_ref[...] = seg(_SEG_NV, 2 * KV_W).astype(BF16).T
    ckv_ref[...] = seg(_SEG_CKV, 2 * KV_W).astype(BF16)
    misc_ref[...] = seg(_SEG_MISC, LANES)


def _in_proj(x2, g, w_all, tm, T):
    M = x2.shape[0]
    assert T % tm == 0
    row = lambda i: (i, 0)
    hrow = lambda i: (0, i, 0)
    return pl.pallas_call(
        functools.partial(_inproj_kernel, tm=tm, T=T),
        grid=(M // tm,),
        in_specs=[pl.BlockSpec((tm, D_MODEL), row),
                  pl.BlockSpec((1, D_MODEL), lambda i: (0, 0)),
                  pl.BlockSpec((D_MODEL, _W_ALL), lambda i: (0, 0))],
        out_specs=[pl.BlockSpec((FOX_HEADS, tm, LANES), hrow),
                   pl.BlockSpec((FOX_HEADS, tm, LANES), hrow),
                   pl.BlockSpec((NSA_HEADS, tm, LANES), hrow),
                   pl.BlockSpec((2 * NSA_GROUPS, tm, LANES), hrow),
                   pl.BlockSpec((FOX_W, tm), lambda i: (0, i)),
                   pl.BlockSpec((2 * KV_W, tm), lambda i: (0, i)),
                   pl.BlockSpec((tm, 2 * KV_W), row),
                   pl.BlockSpec((tm, LANES), row)],
        out_shape=[jax.ShapeDtypeStruct((FOX_HEADS, M, LANES), BF16),
                   jax.ShapeDtypeStruct((FOX_HEADS, M, LANES), BF16),
                   jax.ShapeDtypeStruct((NSA_HEADS, M, LANES), BF16),
                   jax.ShapeDtypeStruct((2 * NSA_GROUPS, M, LANES), BF16),
                   jax.ShapeDtypeStruct((FOX_W, M), BF16),
                   jax.ShapeDtypeStruct((2 * KV_W, M), BF16),
                   jax.ShapeDtypeStruct((M, 2 * KV_W), BF16),
                   jax.ShapeDtypeStruct((M, LANES), F32)],
        compiler_params=pltpu.CompilerParams(dimension_semantics=("parallel",), vmem_limit_bytes=VMEM_LIMIT),
        name="in_proj",
    )(x2, g, w_all)


_SCAN_BLK = 256


DECAY_LANES = (HEAD_DIM, HEAD_DIM + FOX_HEADS, HEAD_DIM + 2 * FOX_HEADS)


def _decay_kernel(misc_ref, bf_ref, c3_ref, *, T):
    r = lax.broadcasted_iota(jnp.int32, (_SCAN_BLK, _SCAN_BLK), 0)
    c = lax.broadcasted_iota(jnp.int32, (_SCAN_BLK, _SCAN_BLK), 1)
    tri = jnp.where(r >= c, 1.0, 0.0).astype(BF16)
    lane = lax.broadcasted_iota(jnp.int32, (1, LANES), 1)

    def blk(n, carry):
        s0 = pl.multiple_of(n * _SCAN_BLK, _SCAN_BLK)
        x = misc_ref[pl.ds(s0, _SCAN_BLK), :] + bf_ref[...]
        lf = jnp.minimum(x, 0.0) - jnp.log1p(jnp.exp(-jnp.abs(x)))
        hi, mid, lo = _split3(lf)
        cs = _dot(tri, hi) + _dot(tri, mid) + _dot(tri, lo) + carry
        hi, mid, lo = (t.astype(F32) for t in _split3(cs * LOG2E))
        c3_ref[pl.ds(s0, _SCAN_BLK), :] = jnp.where(
            lane < FOX_HEADS, hi,
            jnp.where(lane < 2 * FOX_HEADS, pltpu.roll(mid, FOX_HEADS, 1),
                      jnp.where(lane < 3 * FOX_HEADS, pltpu.roll(lo, 2 * FOX_HEADS, 1), 0.0)))
        return cs[_SCAN_BLK - 1:_SCAN_BLK, :]

    lax.fori_loop(0, T // _SCAN_BLK, blk, jnp.zeros((1, LANES), F32))


def _fox_decay(misc, bf_pad, B, T):
    return pl.pallas_call(
        functools.partial(_decay_kernel, T=T),
        grid=(B,),
        in_specs=[pl.BlockSpec((T, LANES), lambda b: (b, 0)),
                  pl.BlockSpec((1, LANES), lambda b: (0, 0))],
        out_specs=pl.BlockSpec((T, LANES), lambda b: (b, 0)),
        out_shape=jax.ShapeDtypeStruct((B * T, LANES), F32),
        compiler_params=pltpu.CompilerParams(dimension_semantics=("parallel",), vmem_limit_bytes=VMEM_LIMIT),
        name="fox_decay",
    )(misc, bf_pad)


_FOX_PREP_BLK = 512


SUM_ROWS = 16


def _with_sum_row(vt):
    r = lax.broadcasted_iota(jnp.int32, (SUM_ROWS, vt.shape[1]), 0)
    return jnp.concatenate([vt, jnp.where(r == 0, 1.0, 0.0).astype(vt.dtype)], axis=0)


def _stage_scores(slot_ref, s):
    slot_ref[...] = s
    return jnp.max(s, axis=0, keepdims=True)


def _softmax_step(slot_ref, smax, vt, carry):
    m, acc = carry
    m_new = jnp.maximum(m, smax)
    p = jnp.exp2(slot_ref[...] - m_new).astype(BF16)
    return m_new, jnp.exp2(m - m_new) * acc + _dot(_with_sum_row(vt), p)


def _fox_kernel(q_ref, k_ref, vt_ref, c3_ref, mask_ref, o_ref, kaug_sc, s_sc, qt_sc, *, tq, tk, T, nh):
    hg = pl.program_id(1)
    i = pl.program_id(2)
    lane = lax.broadcasted_iota(jnp.int32, (1, LANES), 1)
    decay_lane = (lane == DECAY_LANES[0]) | (lane == DECAY_LANES[1]) | (lane == DECAY_LANES[2])

    @pl.when(i == 0)
    def _():
        for hh in range(nh):
            def prep(n, _, hh=hh):
                r0 = pl.multiple_of(n * _FOX_PREP_BLK, _FOX_PREP_BLK)
                c3 = pltpu.roll(c3_ref[pl.ds(r0, _FOX_PREP_BLK), :], HEAD_DIM - (nh * hg + hh), 1)
                kaug_sc[hh, pl.ds(r0, _FOX_PREP_BLK), :] = jnp.where(
                    lane < HEAD_DIM, k_ref[hh, pl.ds(r0, _FOX_PREP_BLK), :],
                    jnp.where(decay_lane, c3, 0.0).astype(BF16))
                return 0

            lax.fori_loop(0, T // _FOX_PREP_BLK, prep, 0)

    for hh in range(nh):
        qt_sc[hh] = q_ref[hh].T

    n_diag = tq // tk
    n_far = i * n_diag
    no_mask = n_diag

    def stage(hh, j, slot, mask_idx):
        ks = pl.multiple_of(j * tk, tk)
        s = _dot(kaug_sc[hh, pl.ds(ks, tk), :], qt_sc[hh])
        if mask_idx is not None:
            s = s + mask_ref[mask_idx]
        return _stage_scores(s_sc.at[slot], s)

    def update(hh, j, slot, smax, carry):
        ks = pl.multiple_of(j * tk, tk)
        return _softmax_step(s_sc.at[slot], smax, vt_ref[hh * HEAD_DIM:(hh + 1) * HEAD_DIM, pl.ds(ks, tk)], carry)

    def far(j, carry):
        state, smax = list(carry[0]), carry[1]
        for hh in range(nh):
            if hh + 1 < nh:
                smax_next = stage(hh + 1, j, (hh + 1) % 2, None)
            else:
                smax_next = stage(0, j + 1, 0, jnp.where(j + 1 == n_far, 0, no_mask))
            state[hh] = update(hh, j, hh % 2, smax, state[hh])
            smax = smax_next
        return tuple(state), smax

    init = (jnp.full((1, tq), NEG_INF, F32), jnp.zeros((HEAD_DIM + SUM_ROWS, tq), F32))
    smax = stage(0, 0, 0, jnp.where(n_far == 0, 0, no_mask))
    state, smax = lax.fori_loop(0, n_far, far, ((init,) * nh, smax))
    state = list(state)
    steps = [(hh, d) for d in range(n_diag) for hh in range(nh)]
    for n, (hh, d) in enumerate(steps):
        if n + 1 < len(steps):
            hh2, d2 = steps[n + 1]
            smax_next = stage(hh2, n_far + d2, (n + 1) % 2, d2)
        state[hh] = update(hh, n_far + d, n % 2, smax, state[hh])
        smax = smax_next
    outs = [acc[0:HEAD_DIM] * (1.0 / acc[HEAD_DIM:HEAD_DIM + 1]) for _, acc in state]
    o_ref[...] = jnp.concatenate(outs, axis=0).T.astype(BF16)


def _fox_attention(fq, fk, fvt, c3, B, T, tq, tk, nh):
    assert nh % 2 == 0 and FOX_HEADS % nh == 0
    nT = T // tq
    n_diag = tq // tk
    kk = np.arange(tk)[None, :, None] + tk * np.arange(n_diag)[:, None, None]
    mask = np.where(kk <= np.arange(tq)[None, None, :], 0.0, NEG_INF).astype(np.float32)
    mask = jnp.asarray(np.concatenate([mask, np.zeros((1, tk, tq), np.float32)]))
    return pl.pallas_call(
        functools.partial(_fox_kernel, tq=tq, tk=tk, T=T, nh=nh),
        grid=(B, FOX_HEADS // nh, nT),
        in_specs=[pl.BlockSpec((nh, tq, LANES), lambda b, hp, i: (hp, b * nT + i, 0)),
                  pl.BlockSpec((nh, T, LANES), lambda b, hp, i: (hp, b, 0)),
                  pl.BlockSpec((nh * HEAD_DIM, T), lambda b, hp, i: (hp, b)),
                  pl.BlockSpec((T, LANES), lambda b, hp, i: (b, 0)),
                  pl.BlockSpec((n_diag + 1, tk, tq), lambda b, hp, i: (0, 0, 0))],
        out_specs=pl.BlockSpec((tq, nh * HEAD_DIM), lambda b, hp, i: (b * nT + i, hp)),
        out_shape=jax.ShapeDtypeStruct((B * T, FOX_W), BF16),
        scratch_shapes=[pltpu.VMEM((nh, T, LANES), BF16),
                        pltpu.VMEM((2, tk, tq), F32),
                        pltpu.VMEM((nh, LANES, tq), BF16)],
        compiler_params=pltpu.CompilerParams(dimension_semantics=("parallel", "parallel", "arbitrary"),
                                             vmem_limit_bytes=VMEM_LIMIT),
        name="fox_attn",
    )(fq, fk, fvt, c3, mask)


def _compress_kernel(r_ref, w1_ref, pe_ref, w2k_ref, w2v_ref, kc_ref, vc_ref, *, NR):
    half = CMP_STRIDE * HEAD_DIM
    vacc = jnp.zeros((NR, LANES), F32)
    for idx in range(2 * NSA_GROUPS):
        kind, g = divmod(idx, NSA_GROUPS)
        rm = r_ref[0, idx]
        a = _dot(rm, w1_ref[kind, 0:half, :])
        bm = _dot(rm, w1_ref[kind, half:2 * half, :])
        pe_term = _dot(pe_ref[kind], w1_ref[kind])[0:1, :]
        pre = a + pltpu.roll(bm, NR - 1, 0) + pe_term
        hid = (pre * jax.nn.sigmoid(pre)).astype(BF16)
        if kind == 0:
            kc_ref[0, g] = _dot(hid, w2k_ref[...]).astype(BF16)
        else:
            vacc = vacc + _dot(hid, w2v_ref[g])
    vc_ref[0] = vacc.astype(BF16)


def _compress(r4, w1, pe, w2k, w2v_pad, B, NR):
    return pl.pallas_call(
        functools.partial(_compress_kernel, NR=NR),
        grid=(B,),
        in_specs=[pl.BlockSpec((1, 2 * NSA_GROUPS, NR, CMP_STRIDE * HEAD_DIM), lambda b: (b, 0, 0, 0)),
                  pl.BlockSpec((2, CMP_BLOCK * HEAD_DIM, CMP_HIDDEN), lambda b: (0, 0, 0)),
                  pl.BlockSpec((2, 16, CMP_BLOCK * HEAD_DIM), lambda b: (0, 0, 0)),
                  pl.BlockSpec((CMP_HIDDEN, LANES), lambda b: (0, 0)),
                  pl.BlockSpec((NSA_GROUPS, CMP_HIDDEN, LANES), lambda b: (0, 0, 0))],
        out_specs=[pl.BlockSpec((1, NSA_GROUPS, NR, LANES), lambda b: (b, 0, 0, 0)),
                   pl.BlockSpec((1, NR, LANES), lambda b: (b, 0, 0))],
        out_shape=[jax.ShapeDtypeStruct((B, NSA_GROUPS, NR, LANES), BF16),
                   jax.ShapeDtypeStruct((B, NR, LANES), BF16)],
        compiler_params=pltpu.CompilerParams(dimension_semantics=("parallel",), vmem_limit_bytes=VMEM_LIMIT),
        name="compress",
    )(r4, w1, pe, w2k, w2v_pad)


def _cmp_kernel(q_ref, kc_ref, vc_ref, bt_ref, misc_ref, ocmp_ref, qaug_ref, bias_sc, val_sc, *, tq, NR):
    i = pl.program_id(0)
    b = pl.program_id(1)
    t0 = i * tq
    n_slc = LANES // 2
    gates = jax.nn.sigmoid(misc_ref[...])

    @pl.when(b == 0)
    def _():
        f = lax.broadcasted_iota(jnp.int32, (LANES, NR), 0)
        c = lax.broadcasted_iota(jnp.int32, (LANES, NR), 1)
        place = (((f % CMP_BAND) == (c - t0 // CMP_STRIDE + CMP_BAND // 2)) & (f < 3 * CMP_BAND))
        place = jnp.where(place, 1.0, 0.0).astype(BF16)
        for h in range(NSA_HEADS):
            bias_sc[h] = _dot(bt_ref[h], place)

    lane = lax.broadcasted_iota(jnp.int32, (1, LANES), 1)

    def gate_col(h):
        c0 = FOX_HEADS + 3 * h
        return gates[:, c0:c0 + 1]

    def tile(nc, nb):
        t = t0 + lax.broadcasted_iota(jnp.int32, (tq, 1), 0)
        c = lax.broadcasted_iota(jnp.int32, (1, nc), 1)
        cmask = (c * CMP_STRIDE + (CMP_BLOCK - 1)) <= t

        jj = lax.broadcasted_iota(jnp.int32, (LANES, nc), 0)
        cc = lax.broadcasted_iota(jnp.int32, (LANES, nc), 1)
        ov = ((cc * CMP_STRIDE < jj * SLC_BLOCK + SLC_BLOCK) & (cc * CMP_STRIDE + CMP_BLOCK > jj * SLC_BLOCK)
              & (jj < n_slc) & (cc < NR - 1))
        ov = jnp.where(ov, 1.0, 0.0).astype(BF16)

        jrow = lax.broadcasted_iota(jnp.int32, (nb, tq), 0)
        tt = t0 + lax.broadcasted_iota(jnp.int32, (nb, tq), 1)
        cur = tt // SLC_BLOCK
        forced = (jrow == 0) | (jrow == cur) | (jrow == cur - 1)
        valid = jrow * SLC_BLOCK <= tt

        outs = []
        for g in range(NSA_GROUPS):
            qs = q_ref[g * NSA_REP:(g + 1) * NSA_REP].reshape(NSA_REP * tq, LANES)
            s = (_dot_nt(qs, kc_ref[0, g, 0:nc, :]).reshape(NSA_REP, tq, nc)
                 + bias_sc[g * NSA_REP:(g + 1) * NSA_REP, :, 0:nc])
            s = jnp.where(cmask, s, NEG_INF)
            m = jnp.max(s, axis=-1, keepdims=True)
            e = jnp.where(cmask, jnp.exp2(s - m), 0.0)
            l = jnp.sum(e, axis=-1, keepdims=True)
            p = e * (1.0 / jnp.where(l > 0.0, l, 1.0))
            o = _dot(p.reshape(NSA_REP * tq, nc).astype(BF16), vc_ref[0, 0:nc, :])
            outs.append(o.reshape(NSA_REP, tq, LANES))

            if nb > SLC_TOPK:
                hi, mid, lo = _split3(jnp.sum(p, axis=0))
                imp = (_dot_nt(ov, hi) + _dot_nt(ov, mid) + _dot_nt(ov, lo))[0:nb, :]
                val_sc[0:nb, :] = jnp.where(forced, FORCED_SCORE, jnp.where(valid, imp, -1.0))
                vals = [val_sc[8 * a:8 * a + 8, :] for a in range(nb // 8)]
                ranks = [jnp.zeros((8, tq), F32) for _ in vals]
                j8 = lax.broadcasted_iota(jnp.int32, (8, 1), 0)
                for k in range(nb):
                    vk = val_sc[k:k + 1, :]
                    for a, va in enumerate(vals):
                        if 8 * a > k:
                            ahead = jnp.where(vk >= va, 1.0, 0.0)
                        elif 8 * a + 7 < k:
                            ahead = jnp.where(vk > va, 1.0, 0.0)
                        else:
                            tie = jnp.where(j8 > k - 8 * a, 1.0, 0.0)
                            ahead = jnp.where(vk > va, 1.0, jnp.where(vk == va, tie, 0.0))
                        ranks[a] = ranks[a] + ahead
                selneg = jnp.where(jnp.concatenate(ranks, axis=0) < float(SLC_TOPK), 0.0, NEG_INF)
            else:
                selneg = jnp.zeros((nb, tq), F32)
            pieces = [jnp.zeros((LANES - n_slc, tq), F32), selneg]
            if nb < n_slc:
                pieces.append(jnp.full((n_slc - nb, tq), NEG_INF, F32))
            selneg = jnp.concatenate(pieces, axis=0).T.astype(BF16)
            for r in range(NSA_REP):
                h = g * NSA_REP + r
                qaug_ref[h] = jnp.where(lane < HEAD_DIM, q_ref[h], selneg)

        for r in range(NSA_REP):
            ocmp_ref[:, r * LANES:(r + 1) * LANES] = jnp.where(
                lane < HEAD_DIM, outs[0][r] * gate_col(r), outs[1][r] * gate_col(NSA_REP + r))

    per = SLC_TOPK * SLC_BLOCK // tq
    variants = []
    for v in range(n_slc // SLC_TOPK):
        nb = SLC_TOPK * (v + 1)
        nc = min(NR, -(-(nb * SLC_BLOCK // CMP_STRIDE) // LANES) * LANES)
        variants.append(functools.partial(tile, nc, nb))
    lax.switch(jnp.minimum(i // per, len(variants) - 1), variants)


def _nsa_cmp(nq, kc, vc, bt, misc, B, T, NR):
    tq = NSA_TILE
    nT = T // tq
    M = B * T
    return pl.pallas_call(
        functools.partial(_cmp_kernel, tq=tq, NR=NR),
        grid=(nT, B),
        in_specs=[pl.BlockSpec((NSA_HEADS, tq, LANES), lambda i, b: (0, b * nT + i, 0)),
                  pl.BlockSpec((1, NSA_GROUPS, NR, LANES), lambda i, b: (b, 0, 0, 0)),
                  pl.BlockSpec((1, NR, LANES), lambda i, b: (b, 0, 0)),
                  pl.BlockSpec((NSA_HEADS, tq, LANES), lambda i, b: (0, 0, 0)),
                  pl.BlockSpec((tq, LANES), lambda i, b: (b * nT + i, 0))],
        out_specs=[pl.BlockSpec((tq, NSA_REP * LANES), lambda i, b: (b * nT + i, 0)),
                   pl.BlockSpec((NSA_HEADS, tq, LANES), lambda i, b: (0, b * nT + i, 0))],
        out_shape=[jax.ShapeDtypeStruct((M, NSA_REP * LANES), F32),
                   jax.ShapeDtypeStruct((NSA_HEADS, M, LANES), BF16)],
        scratch_shapes=[pltpu.VMEM((NSA_HEADS, tq, NR), F32),
                        pltpu.VMEM((LANES // 2, tq), F32)],
        compiler_params=pltpu.CompilerParams(dimension_semantics=("arbitrary", "arbitrary"),
                                             vmem_limit_bytes=VMEM_LIMIT),
        name="nsa_cmp",
    )(nq, kc, vc, bt, misc)


def _nsa_kernel(q_ref, k_ref, vt_ref, ocmp_ref, misc_ref, dist_ref, o_ref, s_sc, qt_sc, bias_ref, wmask_ref, *, tq):
    i = pl.program_id(1)
    tk = tq
    nq = NSA_REP * tq
    jp = jnp.maximum(i - 1, 0)
    jf = jnp.maximum(i - 2, 0)

    @pl.when((pl.program_id(0) == 0) & (i == 0))
    def _():
        kk = lax.broadcasted_iota(jnp.int32, (tk, tq), 0)
        qq = lax.broadcasted_iota(jnp.int32, (tk, tq), 1)
        for h in range(NSA_HEADS):
            g, r = divmod(h, NSA_REP)
            rows = jnp.broadcast_to(dist_ref[h:h + 1, :], (tk, 2 * tq))
            pair = pltpu.roll(rows, 0, 1, stride=1, stride_axis=0)
            bias_ref[g, 0, :, r * tq:(r + 1) * tq] = jnp.where(kk <= qq, pair[:, 0:tq], NEG_INF)
            bias_ref[g, 1, :, r * tq:(r + 1) * tq] = pair[:, tq:2 * tq]
        for g in range(NSA_GROUPS):
            bias_ref[g, 2] = jnp.full((tk, nq), NEG_INF, F32)
            bias_ref[g, 3] = jnp.zeros((tk, nq), F32)
        for r in range(NSA_REP):
            wmask_ref[0, :, r * tq:(r + 1) * tq] = jnp.where(kk > qq, 0.0, NEG_INF)
        wmask_ref[1] = jnp.full((tk, nq), NEG_INF, F32)

    for g in range(NSA_GROUPS):
        qt_sc[g] = q_ref[g * NSA_REP:(g + 1) * NSA_REP].reshape(nq, LANES).T

    def stage(g, branch, j, slot, bias):
        ks = pl.multiple_of(j * tk, tk)
        s = _dot(k_ref[branch * NSA_GROUPS + g, pl.ds(ks, tk), :], qt_sc[g])
        if bias is not None:
            s = s + bias
        return _stage_scores(s_sc.at[slot], s)

    def update(g, branch, j, slot, smax, carry):
        ks = pl.multiple_of(j * tk, tk)
        r0 = (branch * NSA_GROUPS + g) * HEAD_DIM
        return _softmax_step(s_sc.at[slot], smax, vt_ref[r0:r0 + HEAD_DIM, pl.ds(ks, tk)], carry)

    init = (jnp.full((1, nq), NEG_INF, F32), jnp.zeros((HEAD_DIM + SUM_ROWS, nq), F32))
    prev_tile = jnp.where(i >= 1, 1, 2)

    def far(j, carry):
        c0, c1, smax0 = carry
        smax1 = stage(1, 0, j, 1, None)
        c0 = update(0, 0, j, 0, smax0, c0)
        smax0 = stage(0, 0, j + 1, 0, bias_ref[0, jnp.where(j + 1 == jp, 1, 3)])
        c1 = update(1, 0, j, 1, smax1, c1)
        return c0, c1, smax0

    smax = stage(0, 0, 0, 0, bias_ref[0, jnp.where(i >= 2, 3, prev_tile)])
    c0, c1, smax = lax.fori_loop(0, jp, far, (init, init, smax))
    c_slc = [c0, c1]
    c_win = [init, init]

    near = [(0, jp, lambda g: bias_ref[g, prev_tile]),
            (0, i, lambda g: bias_ref[g, 0]),
            (1, jf, lambda g: wmask_ref[jnp.where(i >= 2, 0, 1)]),
            (1, jp, lambda g: bias_ref[g, prev_tile]),
            (1, i, lambda g: bias_ref[g, 0])]
    steps = [(g, branch, j, bias) for (branch, j, bias) in near for g in range(NSA_GROUPS)]
    for n, (g, branch, j, _) in enumerate(steps):
        if n + 1 < len(steps):
            g2, branch2, j2, bias2 = steps[n + 1]
            smax_next = stage(g2, branch2, j2, (n + 1) % 2, bias2(g2))
        state = c_slc if branch == 0 else c_win
        state[g] = update(g, branch, j, n % 2, smax, state[g])
        smax = smax_next

    gates = jax.nn.sigmoid(misc_ref[...].T)
    for r in range(NSA_REP):
        halves = []
        for g in range(NSA_GROUPS):
            c0 = FOX_HEADS + 3 * (g * NSA_REP + r)
            sl = slice(r * tq, (r + 1) * tq)
            (_, accs), (_, accw) = c_slc[g], c_win[g]
            halves.append(
                accs[0:HEAD_DIM, sl] * (gates[c0 + 1:c0 + 2, :] / accs[HEAD_DIM:HEAD_DIM + 1, sl])
                + accw[0:HEAD_DIM, sl] * (gates[c0 + 2:c0 + 3, :] / accw[HEAD_DIM:HEAD_DIM + 1, sl]))
        o = jnp.concatenate(halves, axis=0).T + ocmp_ref[:, r * LANES:(r + 1) * LANES]
        o_ref[:, r * LANES:(r + 1) * LANES] = o.astype(BF16)


def _nsa_attention(qaug, nk, nvt, ocmp, misc, dist_bias, B, T):
    tq = NSA_TILE
    assert WINDOW == 2 * tq
    nT = T // tq
    M = B * T
    return pl.pallas_call(
        functools.partial(_nsa_kernel, tq=tq),
        grid=(B, nT),
        in_specs=[pl.BlockSpec((NSA_HEADS, tq, LANES), lambda b, i: (0, b * nT + i, 0)),
                  pl.BlockSpec((2 * NSA_GROUPS, T, LANES), lambda b, i: (0, b, 0)),
                  pl.BlockSpec((2 * KV_W, T), lambda b, i: (0, b)),
                  pl.BlockSpec((tq, NSA_REP * LANES), lambda b, i: (b * nT + i, 0)),
                  pl.BlockSpec((tq, LANES), lambda b, i: (b * nT + i, 0)),
                  pl.BlockSpec((NSA_HEADS, 2 * tq), lambda b, i: (0, 0))],
        out_specs=pl.BlockSpec((tq, NSA_REP * LANES), lambda b, i: (b * nT + i, 0)),
        out_shape=jax.ShapeDtypeStruct((M, NSA_W), BF16),
        scratch_shapes=[pltpu.VMEM((2, tq, NSA_REP * tq), F32),
                        pltpu.VMEM((NSA_GROUPS, LANES, NSA_REP * tq), BF16),
                        pltpu.VMEM((NSA_GROUPS, 4, tq, NSA_REP * tq), F32),
                        pltpu.VMEM((2, tq, NSA_REP * tq), F32)],
        compiler_params=pltpu.CompilerParams(dimension_semantics=("arbitrary", "arbitrary"),
                                             vmem_limit_bytes=VMEM_LIMIT),
        name="nsa_attn",
    )(qaug, nk, nvt, ocmp, misc, dist_bias)


def _merge_kernel(x_ref, g_ref, of_ref, on_ref, wga_ref, wgb_ref, wbf_ref, wbn_ref, wo_ref, x1_ref, *, sub):
    for r0 in range(0, x_ref.shape[0], sub):
        rows = slice(r0, r0 + sub)
        x = x_ref[rows, :]
        h = _rms(x, g_ref[...]).astype(BF16)
        ga = jax.nn.sigmoid(_dot(h, wga_ref[...]))
        gb = jax.nn.sigmoid(_dot(h, wgb_ref[...]))
        merged = ga * _dot(of_ref[rows, :], wbf_ref[...]) + gb * _dot(on_ref[rows, :], wbn_ref[...])
        x1_ref[rows, :] = x + _dot(merged.astype(BF16), wo_ref[...])


def _merge(x2, g, o_fox, o_nsa, wga, wgb, wbf, wbn, wo, tm):
    M = x2.shape[0]
    row = lambda i: (i, 0)
    full = lambda i: (0, 0)
    return pl.pallas_call(
        functools.partial(_merge_kernel, sub=tm // 2),
        grid=(M // tm,),
        in_specs=[pl.BlockSpec((tm, D_MODEL), row),
                  pl.BlockSpec((1, D_MODEL), full),
                  pl.BlockSpec((tm, FOX_W), row),
                  pl.BlockSpec((tm, NSA_W), row),
                  pl.BlockSpec((D_MODEL, D_MODEL), full),
                  pl.BlockSpec((D_MODEL, D_MODEL), full),
                  pl.BlockSpec((FOX_W, D_MODEL), full),
                  pl.BlockSpec((NSA_W, D_MODEL), full),
                  pl.BlockSpec((D_MODEL, D_MODEL), full)],
        out_specs=pl.BlockSpec((tm, D_MODEL), row),
        out_shape=jax.ShapeDtypeStruct((M, D_MODEL), F32),
        compiler_params=pltpu.CompilerParams(dimension_semantics=("parallel",), vmem_limit_bytes=VMEM_LIMIT),
        name="merge",
    )(x2, g, o_fox, o_nsa, wga, wgb, wbf, wbn, wo)


def _mlp_kernel(x_ref, g_ref, w1_ref, w2_ref, gf_ref, o_ref, *, final, sub):
    for r0 in range(0, x_ref.shape[0], sub):
        x = x_ref[r0:r0 + sub, :]
        h = _rms(x, g_ref[...]).astype(BF16)
        u = jnp.maximum(_dot(h, w1_ref[...]), 0.0)
        y = x + _dot((u * u).astype(BF16), w2_ref[...])
        o_ref[r0:r0 + sub, :] = _rms(y, gf_ref[...]) if final else y


def _mlp(x1, g, w1, w2, gf, tm, sub, final):
    M = x1.shape[0]
    row = lambda i: (i, 0)
    full = lambda i: (0, 0)
    once = pl.Buffered(1)
    return pl.pallas_call(
        functools.partial(_mlp_kernel, final=final, sub=sub),
        grid=(M // tm,),
        in_specs=[pl.BlockSpec((tm, D_MODEL), row),
                  pl.BlockSpec((1, D_MODEL), full),
                  pl.BlockSpec((D_MODEL, D_FF), full, pipeline_mode=once),
                  pl.BlockSpec((D_FF, D_MODEL), full, pipeline_mode=once),
                  pl.BlockSpec((1, D_MODEL), full)],
        out_specs=pl.BlockSpec((tm, D_MODEL), row),
        out_shape=jax.ShapeDtypeStruct((M, D_MODEL), F32),
        compiler_params=pltpu.CompilerParams(dimension_semantics=("parallel",), vmem_limit_bytes=VMEM_LIMIT),
        name="mlp",
    )(x1, g, w1, w2, gf)


def _bias_tables(rel_tbl, tq):
    L = 2 * tq
    g = (rel_tbl - rel_tbl[REL_BUCKETS - 1]).T[:, _rel_bucket_np(np.arange(L))] * LOG2E
    base = CMP_STRIDE * (CMP_BAND // 2) - (CMP_BLOCK - 1)
    left = CMP_STRIDE * (CMP_BAND - 1) - base
    gp = jnp.concatenate([jnp.zeros((NSA_HEADS, left), F32), g], axis=1)
    band = jnp.stack([gp[:, left + base - CMP_STRIDE * f:left + base - CMP_STRIDE * f + tq] for f in range(CMP_BAND)],
                     axis=-1)
    return g, band


def _layer(x2, B, T, rel_tbl, g_attn, w_in, b_forget, pe_k, w1_k, w2_k, pe_v, w1_v, w2_v,
           w_br_fox, w_br_nsa, w_out, g_mlp, w_ff1, w_ff2, g_final, final):
    M = B * T
    NR = T // CMP_STRIDE
    offs = np.cumsum((FOX_W, FOX_W, FOX_W, FOX_HEADS, NSA_W, KV_W, KV_W, KV_W, KV_W, KV_W, KV_W,
                      3 * NSA_HEADS, D_MODEL, D_MODEL))
    (w_fq, w_fk, w_fv, w_fl, w_nq, w_kc, w_vc, w_ksl, w_vsl, w_kwn, w_vwn, w_ng, w_ga, w_gb) = jnp.split(
        w_in, offs[:-1].tolist(), axis=-1)
    w_all = jnp.concatenate(
        [w_fq * SCALE, w_fk, w_nq * SCALE, w_ksl, w_kwn, w_fv, w_vsl, w_vwn, w_kc, w_vc, w_fl, w_ng,
         jnp.zeros((D_MODEL, _W_ALL - _SEG_MISC - FOX_HEADS - 3 * NSA_HEADS), F32)], axis=-1).astype(BF16)

    fq, fk, nq, nk, fv, nv, ckv, misc = _in_proj(x2, g_attn.reshape(1, D_MODEL), w_all, tm=512, T=T)

    bf_pad = jnp.zeros((1, LANES), F32).at[0, :FOX_HEADS].set(b_forget)
    c3 = _fox_decay(misc, bf_pad, B, T)
    o_fox = _fox_attention(fq, fk, fv, c3, B, T, tq=512, tk=512, nh=8)

    r4 = ckv.reshape(B, NR, CMP_STRIDE, 2 * NSA_GROUPS, HEAD_DIM).transpose(0, 3, 1, 2, 4)
    r4 = r4.reshape(B, 2 * NSA_GROUPS, NR, CMP_STRIDE * HEAD_DIM)
    w1 = jnp.stack([w1_k, w1_v]).astype(BF16)
    pe = jnp.stack([pe_k.reshape(1, -1), pe_v.reshape(1, -1)]).astype(BF16)
    pe = jnp.broadcast_to(pe, (2, 16, CMP_BLOCK * HEAD_DIM))
    zpad = jnp.zeros((CMP_HIDDEN, HEAD_DIM), F32)
    w2v_pad = jnp.stack([jnp.concatenate([w2_v, zpad], axis=1), jnp.concatenate([zpad, w2_v], axis=1)]).astype(BF16)
    w2k_pad = jnp.concatenate([w2_k, zpad], axis=1).astype(BF16)
    kc, vc = _compress(r4, w1, pe, w2k_pad, w2v_pad, B, NR)

    dist_bias, band = _bias_tables(rel_tbl, NSA_TILE)
    hi, mid, lo = _split3(band)
    bt = jnp.concatenate([hi, mid, lo, jnp.zeros_like(hi)], axis=-1)

    ocmp, qaug = _nsa_cmp(nq, kc, vc, bt, misc, B, T, NR)
    o_nsa = _nsa_attention(qaug, nk, nv, ocmp, misc, dist_bias, B, T)

    wbn = w_br_nsa.reshape(NSA_GROUPS, NSA_REP, HEAD_DIM, D_MODEL).transpose(1, 0, 2, 3).reshape(NSA_W, D_MODEL)
    x1 = _merge(x2, g_attn.reshape(1, D_MODEL), o_fox, o_nsa, w_ga.astype(BF16), w_gb.astype(BF16),
                w_br_fox.astype(BF16), wbn.astype(BF16), w_out.astype(BF16), tm=512)
    return _mlp(x1, g_mlp.reshape(1, D_MODEL), w_ff1.astype(BF16), w_ff2.astype(BF16),
                g_final.reshape(1, D_MODEL), tm=512, sub=256, final=final)


def kernel(x, rel_bias_table, g_attn, w_in, b_forget, cmp_pe_k, cmp_w1_k, cmp_w2_k, cmp_pe_v, cmp_w1_v, cmp_w2_v,
           w_br_fox, w_br_nsa, w_out, g_mlp, w_ff1, w_ff2, g_final):
    B, T, _ = x.shape
    depth = g_attn.shape[0]
    x2 = x.reshape(B * T, D_MODEL)
    for l in range(depth):
        x2 = _layer(x2, B, T, rel_bias_table, g_attn[l], w_in[l], b_forget[l], cmp_pe_k[l], cmp_w1_k[l],
                    cmp_w2_k[l], cmp_pe_v[l], cmp_w1_v[l], cmp_w2_v[l], w_br_fox[l], w_br_nsa[l], w_out[l],
                    g_mlp[l], w_ff1[l], w_ff2[l], g_final, final=(l == depth - 1))
    return x2.reshape(B, T, D_MODEL)
```

```python
import functools
import math

import numpy as np
import jax
import jax.numpy as jnp
from jax import lax
from jax.experimental import pallas as pl
from jax.experimental.pallas import tpu as pltpu

F32 = jnp.float32
BF16 = jnp.bfloat16

D_MODEL = 1024
HEAD_DIM = 64
FOX_HEADS = 8
NSA_HEADS = 8
NSA_GROUPS = 2
NSA_REP = NSA_HEADS // NSA_GROUPS
CMP_BLOCK = 32
CMP_STRIDE = 16
CMP_HIDDEN = 256
SLC_BLOCK = 64
SLC_TOPK = 16
WINDOW = 512
REL_BUCKETS = 32
REL_MAX_DIST = 128
D_FF = 4 * D_MODEL
RMS_EPS = 1e-6
NEG_INF = -1e30
FORCED_SCORE = 1e4
SCALE = HEAD_DIM ** -0.5
LOG2E = math.log2(math.e)

FOX_W = FOX_HEADS * HEAD_DIM
NSA_W = NSA_HEADS * HEAD_DIM
KV_W = NSA_GROUPS * HEAD_DIM
LANES = 128
NSA_TILE = 256
REL_FAR = 113
CMP_BAND = 32

VMEM_LIMIT = 56 * 1024 * 1024

_NT = (((1,), (1,)), ((), ()))


def _dot(a, b):
    return jnp.dot(a, b, preferred_element_type=F32)


def _dot_nt(a, b):
    return lax.dot_general(a, b, _NT, preferred_element_type=F32)


def _split3(x):
    hi = x.astype(BF16)
    r1 = x - hi.astype(F32)
    mid = r1.astype(BF16)
    lo = (r1 - mid.astype(F32)).astype(BF16)
    return hi, mid, lo


def _rms(x, g):
    return x * lax.rsqrt(jnp.mean(x * x, axis=-1, keepdims=True) + RMS_EPS) * g


def _rel_bucket_np(n):
    exact = REL_BUCKETS // 2
    nf = np.maximum(n, exact).astype(np.float64)
    log_b = exact + (np.log(nf / exact) / math.log(REL_MAX_DIST / exact) * (REL_BUCKETS - exact)).astype(np.int64)
    return np.where(n < exact, n, np.minimum(log_b, REL_BUCKETS - 1))


_SEG_FQ, _SEG_FK, _SEG_NQ, _SEG_NK, _SEG_FV, _SEG_NV, _SEG_CKV, _SEG_MISC = (
    0, 512, 1024, 1536, 1792, 2304, 2560, 2816)
_W_ALL = 2944


def _inproj_kernel(x_ref, g_ref, w_ref, fq_ref, fk_ref, nq_ref, nk_ref, fv_ref, nv_ref, ckv_ref, misc_ref, *, tm, T):
    h = _rms(x_ref[...], g_ref[...]).astype(BF16)

    def seg(a, n):
        return _dot(h, w_ref[:, a:a + n])

    lane = lax.broadcasted_iota(jnp.int32, (1, LANES), 1)

    def heads_wide(ref, a, n_heads, mult, upper):
        r = seg(a, n_heads * HEAD_DIM)
        for j in range(n_heads):
            pair = r[:, (j // 2) * LANES:(j // 2 + 1) * LANES]
            if j % 2:
                pair = pltpu.roll(pair, HEAD_DIM, 1)
            ref[j] = jnp.where(lane < HEAD_DIM, pair * mult, upper(j)).astype(BF16)

    q_upper = jnp.where((lane == DECAY_LANES[0]) | (lane == DECAY_LANES[1]) | (lane == DECAY_LANES[2]), -1.0, 0.0)
    heads_wide(fq_ref, _SEG_FQ, FOX_HEADS, LOG2E, lambda j: q_upper)
    heads_wide(fk_ref, _SEG_FK, FOX_HEADS, 1.0, lambda j: 0.0)
    heads_wide(nq_ref, _SEG_NQ, NSA_HEADS, LOG2E, lambda j: 0.0)
    t = (pl.program_id(0) * tm + lax.broadcasted_iota(jnp.int32, (tm, 1), 0)) % T
    block_onehot = jnp.where(lane - HEAD_DIM == t // SLC_BLOCK, 1.0, 0.0)
    heads_wide(nk_ref, _SEG_NK, 2 * NSA_GROUPS, 1.0, lambda j: block_onehot if j < NSA_GROUPS else 0.0)
    fv_ref[...] = seg(_SEG_FV, FOX_W).astype(BF16).T
    nv_ref[...] = seg(_SEG_NV, 2 * KV_W).astype(BF16).T
    ckv_ref[...] = seg(_SEG_CKV, 2 * KV_W).astype(BF16)
    misc_ref[...] = seg(_SEG_MISC, LANES)


def _in_proj(x2, g, w_all, tm, T):
    M = x2.shape[0]
    assert T % tm == 0
    row = lambda i: (i, 0)
    hrow = lambda i: (0, i, 0)
    return pl.pallas_call(
        functools.partial(_inproj_kernel, tm=tm, T=T),
        grid=(M // tm,),
        in_specs=[pl.BlockSpec((tm, D_MODEL), row),
                  pl.BlockSpec((1, D_MODEL), lambda i: (0, 0)),
                  pl.BlockSpec((D_MODEL, _W_ALL), lambda i: (0, 0))],
        out_specs=[pl.BlockSpec((FOX_HEADS, tm, LANES), hrow),
                   pl.BlockSpec((FOX_HEADS, tm, LANES), hrow),
                   pl.BlockSpec((NSA_HEADS, tm, LANES), hrow),
                   pl.BlockSpec((2 * NSA_GROUPS, tm, LANES), hrow),
                   pl.BlockSpec((FOX_W, tm), lambda i: (0, i)),
                   pl.BlockSpec((2 * KV_W, tm), lambda i: (0, i)),
                   pl.BlockSpec((tm, 2 * KV_W), row),
                   pl.BlockSpec((tm, LANES), row)],
        out_shape=[jax.ShapeDtypeStruct((FOX_HEADS, M, LANES), BF16),
                   jax.ShapeDtypeStruct((FOX_HEADS, M, LANES), BF16),
                   jax.ShapeDtypeStruct((NSA_HEADS, M, LANES), BF16),
                   jax.ShapeDtypeStruct((2 * NSA_GROUPS, M, LANES), BF16),
                   jax.ShapeDtypeStruct((FOX_W, M), BF16),
                   jax.ShapeDtypeStruct((2 * KV_W, M), BF16),
                   jax.ShapeDtypeStruct((M, 2 * KV_W), BF16),
                   jax.ShapeDtypeStruct((M, LANES), F32)],
        compiler_params=pltpu.CompilerParams(dimension_semantics=("parallel",), vmem_limit_bytes=VMEM_LIMIT),
        name="in_proj",
    )(x2, g, w_all)


_SCAN_BLK = 256


DECAY_LANES = (HEAD_DIM, HEAD_DIM + FOX_HEADS, HEAD_DIM + 2 * FOX_HEADS)


def _decay_kernel(misc_ref, bf_ref, c3_ref, *, T):
    r = lax.broadcasted_iota(jnp.int32, (_SCAN_BLK, _SCAN_BLK), 0)
    c = lax.broadcasted_iota(jnp.int32, (_SCAN_BLK, _SCAN_BLK), 1)
    tri = jnp.where(r >= c, 1.0, 0.0).astype(BF16)
    lane = lax.broadcasted_iota(jnp.int32, (1, LANES), 1)

    def blk(n, carry):
        s0 = pl.multiple_of(n * _SCAN_BLK, _SCAN_BLK)
        x = misc_ref[pl.ds(s0, _SCAN_BLK), :] + bf_ref[...]
        lf = jnp.minimum(x, 0.0) - jnp.log1p(jnp.exp(-jnp.abs(x)))
        hi, mid, lo = _split3(lf)
        cs = _dot(tri, hi) + _dot(tri, mid) + _dot(tri, lo) + carry
        hi, mid, lo = (t.astype(F32) for t in _split3(cs * LOG2E))
        c3_ref[pl.ds(s0, _SCAN_BLK), :] = jnp.where(
            lane < FOX_HEADS, hi,
            jnp.where(lane < 2 * FOX_HEADS, pltpu.roll(mid, FOX_HEADS, 1),
                      jnp.where(lane < 3 * FOX_HEADS, pltpu.roll(lo, 2 * FOX_HEADS, 1), 0.0)))
        return cs[_SCAN_BLK - 1:_SCAN_BLK, :]

    lax.fori_loop(0, T // _SCAN_BLK, blk, jnp.zeros((1, LANES), F32))


def _fox_decay(misc, bf_pad, B, T):
    return pl.pallas_call(
        functools.partial(_decay_kernel, T=T),
        grid=(B,),
        in_specs=[pl.BlockSpec((T, LANES), lambda b: (b, 0)),
                  pl.BlockSpec((1, LANES), lambda b: (0, 0))],
        out_specs=pl.BlockSpec((T, LANES), lambda b: (b, 0)),
        out_shape=jax.ShapeDtypeStruct((B * T, LANES), F32),
        compiler_params=pltpu.CompilerParams(dimension_semantics=("parallel",), vmem_limit_bytes=VMEM_LIMIT),
        name="fox_decay",
    )(misc, bf_pad)


_FOX_PREP_BLK = 512


SUM_ROWS = 16


def _with_sum_row(vt):
    r = lax.broadcasted_iota(jnp.int32, (SUM_ROWS, vt.shape[1]), 0)
    return jnp.concatenate([vt, jnp.where(r == 0, 1.0, 0.0).astype(vt.dtype)], axis=0)


def _stage_scores(slot_ref, s):
    slot_ref[...] = s
    return jnp.max(s, axis=0, keepdims=True)


def _softmax_step(slot_ref, smax, vt, carry):
    m, acc = carry
    m_new = jnp.maximum(m, smax)
    p = jnp.exp2(slot_ref[...] - m_new).astype(BF16)
    return m_new, jnp.exp2(m - m_new) * acc + _dot(_with_sum_row(vt), p)


def _fox_kernel(q_ref, k_ref, vt_ref, c3_ref, mask_ref, o_ref, kaug_sc, s_sc, qt_sc, *, tq, tk, T, nh):
    hg = pl.program_id(1)
    i = pl.program_id(2)
    lane = lax.broadcasted_iota(jnp.int32, (1, LANES), 1)
    decay_lane = (lane == DECAY_LANES[0]) | (lane == DECAY_LANES[1]) | (lane == DECAY_LANES[2])

    @pl.when(i == 0)
    def _():
        for hh in range(nh):
            def prep(n, _, hh=hh):
                r0 = pl.multiple_of(n * _FOX_PREP_BLK, _FOX_PREP_BLK)
                c3 = pltpu.roll(c3_ref[pl.ds(r0, _FOX_PREP_BLK), :], HEAD_DIM - (nh * hg + hh), 1)
                kaug_sc[hh, pl.ds(r0, _FOX_PREP_BLK), :] = jnp.where(
                    lane < HEAD_DIM, k_ref[hh, pl.ds(r0, _FOX_PREP_BLK), :],
                    jnp.where(decay_lane, c3, 0.0).astype(BF16))
                return 0

            lax.fori_loop(0, T // _FOX_PREP_BLK, prep, 0)

    for hh in range(nh):
        qt_sc[hh] = q_ref[hh].T

    n_diag = tq // tk
    n_far = i * n_diag
    no_mask = n_diag

    def stage(hh, j, slot, mask_idx):
        ks = pl.multiple_of(j * tk, tk)
        s = _dot(kaug_sc[hh, pl.ds(ks, tk), :], qt_sc[hh])
        if mask_idx is not None:
            s = s + mask_ref[mask_idx]
        return _stage_scores(s_sc.at[slot], s)

    def update(hh, j, slot, smax, carry):
        ks = pl.multiple_of(j * tk, tk)
        return _softmax_step(s_sc.at[slot], smax, vt_ref[hh * HEAD_DIM:(hh + 1) * HEAD_DIM, pl.ds(ks, tk)], carry)

    def far(j, carry):
        state, smax = list(carry[0]), carry[1]
        for hh in range(nh):
            if hh + 1 < nh:
                smax_next = stage(hh + 1, j, (hh + 1) % 2, None)
            else:
                smax_next = stage(0, j + 1, 0, jnp.where(j + 1 == n_far, 0, no_mask))
            state[hh] = update(hh, j, hh % 2, smax, state[hh])
            smax = smax_next
        return tuple(state), smax

    init = (jnp.full((1, tq), NEG_INF, F32), jnp.zeros((HEAD_DIM + SUM_ROWS, tq), F32))
    smax = stage(0, 0, 0, jnp.where(n_far == 0, 0, no_mask))
    state, smax = lax.fori_loop(0, n_far, far, ((init,) * nh, smax))
    state = list(state)
    steps = [(hh, d) for d in range(n_diag) for hh in range(nh)]
    for n, (hh, d) in enumerate(steps):
        if n + 1 < len(steps):
            hh2, d2 = steps[n + 1]
            smax_next = stage(hh2, n_far + d2, (n + 1) % 2, d2)
        state[hh] = update(hh, n_far + d, n % 2, smax, state[hh])
        smax = smax_next
    outs = [acc[0:HEAD_DIM] * (1.0 / acc[HEAD_DIM:HEAD_DIM + 1]) for _, acc in state]
    o_ref[...] = jnp.concatenate(outs, axis=0).T.astype(BF16)


def _fox_attention(fq, fk, fvt, c3, B, T, tq, tk, nh):
    assert nh % 2 == 0 and FOX_HEADS % nh == 0
    nT = T // tq
    n_diag = tq // tk
    kk = np.arange(tk)[None, :, None] + tk * np.arange(n_diag)[:, None, None]
    mask = np.where(kk <= np.arange(tq)[None, None, :], 0.0, NEG_INF).astype(np.float32)
    mask = jnp.asarray(np.concatenate([mask, np.zeros((1, tk, tq), np.float32)]))
    return pl.pallas_call(
        functools.partial(_fox_kernel, tq=tq, tk=tk, T=T, nh=nh),
        grid=(B, FOX_HEADS // nh, nT),
        in_specs=[pl.BlockSpec((nh, tq, LANES), lambda b, hp, i: (hp, b * nT + i, 0)),
                  pl.BlockSpec((nh, T, LANES), lambda b, hp, i: (hp, b, 0)),
                  pl.BlockSpec((nh * HEAD_DIM, T), lambda b, hp, i: (hp, b)),
                  pl.BlockSpec((T, LANES), lambda b, hp, i: (b, 0)),
                  pl.BlockSpec((n_diag + 1, tk, tq), lambda b, hp, i: (0, 0, 0))],
        out_specs=pl.BlockSpec((tq, nh * HEAD_DIM), lambda b, hp, i: (b * nT + i, hp)),
        out_shape=jax.ShapeDtypeStruct((B * T, FOX_W), BF16),
        scratch_shapes=[pltpu.VMEM((nh, T, LANES), BF16),
                        pltpu.VMEM((2, tk, tq), F32),
                        pltpu.VMEM((nh, LANES, tq), BF16)],
        compiler_params=pltpu.CompilerParams(dimension_semantics=("parallel", "parallel", "arbitrary"),
                                             vmem_limit_bytes=VMEM_LIMIT),
        name="fox_attn",
    )(fq, fk, fvt, c3, mask)


def _compress_kernel(r_ref, w1_ref, pe_ref, w2k_ref, w2v_ref, kc_ref, vc_ref, *, NR):
    half = CMP_STRIDE * HEAD_DIM
    vacc = jnp.zeros((NR, LANES), F32)
    for idx in range(2 * NSA_GROUPS):
        kind, g = divmod(idx, NSA_GROUPS)
        rm = r_ref[0, idx]
        a = _dot(rm, w1_ref[kind, 0:half, :])
        bm = _dot(rm, w1_ref[kind, half:2 * half, :])
        pe_term = _dot(pe_ref[kind], w1_ref[kind])[0:1, :]
        pre = a + pltpu.roll(bm, NR - 1, 0) + pe_term
        hid = (pre * jax.nn.sigmoid(pre)).astype(BF16)
        if kind == 0:
            kc_ref[0, g] = _dot(hid, w2k_ref[...]).astype(BF16)
        else:
            vacc = vacc + _dot(hid, w2v_ref[g])
    vc_ref[0] = vacc.astype(BF16)


def _compress(r4, w1, pe, w2k, w2v_pad, B, NR):
    return pl.pallas_call(
        functools.partial(_compress_kernel, NR=NR),
        grid=(B,),
        in_specs=[pl.BlockSpec((1, 2 * NSA_GROUPS, NR, CMP_STRIDE * HEAD_DIM), lambda b: (b, 0, 0, 0)),
                  pl.BlockSpec((2, CMP_BLOCK * HEAD_DIM, CMP_HIDDEN), lambda b: (0, 0, 0)),
                  pl.BlockSpec((2, 16, CMP_BLOCK * HEAD_DIM), lambda b: (0, 0, 0)),
                  pl.BlockSpec((CMP_HIDDEN, LANES), lambda b: (0, 0)),
                  pl.BlockSpec((NSA_GROUPS, CMP_HIDDEN, LANES), lambda b: (0, 0, 0))],
        out_specs=[pl.BlockSpec((1, NSA_GROUPS, NR, LANES), lambda b: (b, 0, 0, 0)),
                   pl.BlockSpec((1, NR, LANES), lambda b: (b, 0, 0))],
        out_shape=[jax.ShapeDtypeStruct((B, NSA_GROUPS, NR, LANES), BF16),
                   jax.ShapeDtypeStruct((B, NR, LANES), BF16)],
        compiler_params=pltpu.CompilerParams(dimension_semantics=("parallel",), vmem_limit_bytes=VMEM_LIMIT),
        name="compress",
    )(r4, w1, pe, w2k, w2v_pad)


def _cmp_kernel(q_ref, kc_ref, vc_ref, bt_ref, misc_ref, ocmp_ref, qaug_ref, bias_sc, val_sc, *, tq, NR):
    i = pl.program_id(0)
    b = pl.program_id(1)
    t0 = i * tq
    n_slc = LANES // 2
    gates = jax.nn.sigmoid(misc_ref[...])

    @pl.when(b == 0)
    def _():
        f = lax.broadcasted_iota(jnp.int32, (LANES, NR), 0)
        c = lax.broadcasted_iota(jnp.int32, (LANES, NR), 1)
        place = (((f % CMP_BAND) == (c - t0 // CMP_STRIDE + CMP_BAND // 2)) & (f < 3 * CMP_BAND))
        place = jnp.where(place, 1.0, 0.0).astype(BF16)
        for h in range(NSA_HEADS):
            bias_sc[h] = _dot(bt_ref[h], place)

    lane = lax.broadcasted_iota(jnp.int32, (1, LANES), 1)

    def gate_col(h):
        c0 = FOX_HEADS + 3 * h
        return gates[:, c0:c0 + 1]

    def tile(nc, nb):
        t = t0 + lax.broadcasted_iota(jnp.int32, (tq, 1), 0)
        c = lax.broadcasted_iota(jnp.int32, (1, nc), 1)
        cmask = (c * CMP_STRIDE + (CMP_BLOCK - 1)) <= t

        jj = lax.broadcasted_iota(jnp.int32, (LANES, nc), 0)
        cc = lax.broadcasted_iota(jnp.int32, (LANES, nc), 1)
        ov = ((cc * CMP_STRIDE < jj * SLC_BLOCK + SLC_BLOCK) & (cc * CMP_STRIDE + CMP_BLOCK > jj * SLC_BLOCK)
              & (jj < n_slc) & (cc < NR - 1))
        ov = jnp.where(ov, 1.0, 0.0).astype(BF16)

        jrow = lax.broadcasted_iota(jnp.int32, (nb, tq), 0)
        tt = t0 + lax.broadcasted_iota(jnp.int32, (nb, tq), 1)
        cur = tt // SLC_BLOCK
        forced = (jrow == 0) | (jrow == cur) | (jrow == cur - 1)
        valid = jrow * SLC_BLOCK <= tt

        outs = []
        for g in range(NSA_GROUPS):
            qs = q_ref[g * NSA_REP:(g + 1) * NSA_REP].reshape(NSA_REP * tq, LANES)
            s = (_dot_nt(qs, kc_ref[0, g, 0:nc, :]).reshape(NSA_REP, tq, nc)
                 + bias_sc[g * NSA_REP:(g + 1) * NSA_REP, :, 0:nc])
            s = jnp.where(cmask, s, NEG_INF)
            m = jnp.max(s, axis=-1, keepdims=True)
            e = jnp.where(cmask, jnp.exp2(s - m), 0.0)
            l = jnp.sum(e, axis=-1, keepdims=True)
            p = e * (1.0 / jnp.where(l > 0.0, l, 1.0))
            o = _dot(p.reshape(NSA_REP * tq, nc).astype(BF16), vc_ref[0, 0:nc, :])
            outs.append(o.reshape(NSA_REP, tq, LANES))

            if nb > SLC_TOPK:
                hi, mid, lo = _split3(jnp.sum(p, axis=0))
                imp = (_dot_nt(ov, hi) + _dot_nt(ov, mid) + _dot_nt(ov, lo))[0:nb, :]
                val_sc[0:nb, :] = jnp.where(forced, FORCED_SCORE, jnp.where(valid, imp, -1.0))
                vals = [val_sc[8 * a:8 * a + 8, :] for a in range(nb // 8)]
                ranks = [jnp.zeros((8, tq), F32) for _ in vals]
                j8 = lax.broadcasted_iota(jnp.int32, (8, 1), 0)
                for k in range(nb):
                    vk = val_sc[k:k + 1, :]
                    for a, va in enumerate(vals):
                        if 8 * a > k:
                            ahead = jnp.where(vk >= va, 1.0, 0.0)
                        elif 8 * a + 7 < k:
                            ahead = jnp.where(vk > va, 1.0, 0.0)
                        else:
                            tie = jnp.where(j8 > k - 8 * a, 1.0, 0.0)
                            ahead = jnp.where(vk > va, 1.0, jnp.where(vk == va, tie, 0.0))
                        ranks[a] = ranks[a] + ahead
                selneg = jnp.where(jnp.concatenate(ranks, axis=0) < float(SLC_TOPK), 0.0, NEG_INF)
            else:
                selneg = jnp.zeros((nb, tq), F32)
            pieces = [jnp.zeros((LANES - n_slc, tq), F32), selneg]
            if nb < n_slc:
                pieces.append(jnp.full((n_slc - nb, tq), NEG_INF, F32))
            selneg = jnp.concatenate(pieces, axis=0).T.astype(BF16)
            for r in range(NSA_REP):
                h = g * NSA_REP + r
                qaug_ref[h] = jnp.where(lane < HEAD_DIM, q_ref[h], selneg)

        for r in range(NSA_REP):
            ocmp_ref[:, r * LANES:(r + 1) * LANES] = jnp.where(
                lane < HEAD_DIM, outs[0][r] * gate_col(r), outs[1][r] * gate_col(NSA_REP + r))

    per = SLC_TOPK * SLC_BLOCK // tq
    variants = []
    for v in range(n_slc // SLC_TOPK):
        nb = SLC_TOPK * (v + 1)
        nc = min(NR, -(-(nb * SLC_BLOCK // CMP_STRIDE) // LANES) * LANES)
        variants.append(functools.partial(tile, nc, nb))
    lax.switch(jnp.minimum(i // per, len(variants) - 1), variants)


def _nsa_cmp(nq, kc, vc, bt, misc, B, T, NR):
    tq = NSA_TILE
    nT = T // tq
    M = B * T
    return pl.pallas_call(
        functools.partial(_cmp_kernel, tq=tq, NR=NR),
        grid=(nT, B),
        in_specs=[pl.BlockSpec((NSA_HEADS, tq, LANES), lambda i, b: (0, b * nT + i, 0)),
                  pl.BlockSpec((1, NSA_GROUPS, NR, LANES), lambda i, b: (b, 0, 0, 0)),
                  pl.BlockSpec((1, NR, LANES), lambda i, b: (b, 0, 0)),
                  pl.BlockSpec((NSA_HEADS, tq, LANES), lambda i, b: (0, 0, 0)),
                  pl.BlockSpec((tq, LANES), lambda i, b: (b * nT + i, 0))],
        out_specs=[pl.BlockSpec((tq, NSA_REP * LANES), lambda i, b: (b * nT + i, 0)),
                   pl.BlockSpec((NSA_HEADS, tq, LANES), lambda i, b: (0, b * nT + i, 0))],
        out_shape=[jax.ShapeDtypeStruct((M, NSA_REP * LANES), F32),
                   jax.ShapeDtypeStruct((NSA_HEADS, M, LANES), BF16)],
        scratch_shapes=[pltpu.VMEM((NSA_HEADS, tq, NR), F32),
                        pltpu.VMEM((LANES // 2, tq), F32)],
        compiler_params=pltpu.CompilerParams(dimension_semantics=("arbitrary", "arbitrary"),
                                             vmem_limit_bytes=VMEM_LIMIT),
        name="nsa_cmp",
    )(nq, kc, vc, bt, misc)


def _nsa_kernel(q_ref, k_ref, vt_ref, ocmp_ref, misc_ref, dist_ref, o_ref, s_sc, qt_sc, bias_ref, wmask_ref, *,
                tq, T, nb):
    i = pl.program_id(1)
    tk = tq
    nq = NSA_REP * tq
    jp = jnp.maximum(i - 1, 0)
    jf = jnp.maximum(i - 2, 0)

    @pl.when((pl.program_id(0) == 0) & (i == 0))
    def _():
        kk = lax.broadcasted_iota(jnp.int32, (tk, tq), 0)
        qq = lax.broadcasted_iota(jnp.int32, (tk, tq), 1)
        for h in range(NSA_HEADS):
            g, r = divmod(h, NSA_REP)
            rows = jnp.broadcast_to(dist_ref[h:h + 1, :], (tk, 2 * tq))
            pair = pltpu.roll(rows, 0, 1, stride=1, stride_axis=0)
            bias_ref[g, 0, :, r * tq:(r + 1) * tq] = jnp.where(kk <= qq, pair[:, 0:tq], NEG_INF)
            bias_ref[g, 1, :, r * tq:(r + 1) * tq] = pair[:, tq:2 * tq]
        for g in range(NSA_GROUPS):
            bias_ref[g, 2] = jnp.full((tk, nq), NEG_INF, F32)
            bias_ref[g, 3] = jnp.zeros((tk, nq), F32)
        for r in range(NSA_REP):
            wmask_ref[0, :, r * tq:(r + 1) * tq] = jnp.where(kk > qq, 0.0, NEG_INF)
        wmask_ref[1] = jnp.full((tk, nq), NEG_INF, F32)

    streams = [(bb, g) for bb in range(nb) for g in range(NSA_GROUPS)]
    ns = len(streams)

    for st, (bb, g) in enumerate(streams):
        qt_sc[st] = q_ref[g * NSA_REP:(g + 1) * NSA_REP, bb].reshape(nq, LANES).T

    def stage(st, branch, j, slot, bias):
        bb, g = streams[st]
        ks = pl.multiple_of(j * tk, tk)
        s = _dot(k_ref[branch * NSA_GROUPS + g, bb, pl.ds(ks, tk), :], qt_sc[st])
        if bias is not None:
            s = s + bias
        return _stage_scores(s_sc.at[slot], s)

    def update(st, branch, j, slot, smax, carry):
        bb, g = streams[st]
        ks = pl.multiple_of(bb * T + j * tk, tk)
        r0 = (branch * NSA_GROUPS + g) * HEAD_DIM
        return _softmax_step(s_sc.at[slot], smax, vt_ref[r0:r0 + HEAD_DIM, pl.ds(ks, tk)], carry)

    init = (jnp.full((1, nq), NEG_INF, F32), jnp.zeros((HEAD_DIM + SUM_ROWS, nq), F32))
    prev_tile = jnp.where(i >= 1, 1, 2)

    def far(j, carry):
        state, smax = list(carry[0]), carry[1]
        for st in range(ns):
            if st + 1 < ns:
                smax_next = stage(st + 1, 0, j, (st + 1) % 2, None)
            else:
                smax_next = stage(0, 0, j + 1, 0, bias_ref[0, jnp.where(j + 1 == jp, 1, 3)])
            state[st] = update(st, 0, j, st % 2, smax, state[st])
            smax = smax_next
        return tuple(state), smax

    smax = stage(0, 0, 0, 0, bias_ref[0, jnp.where(i >= 2, 3, prev_tile)])
    c_slc, smax = lax.fori_loop(0, jp, far, ((init,) * ns, smax))
    c_slc = list(c_slc)
    c_win = [init] * ns

    near = [(0, jp, lambda g: bias_ref[g, prev_tile]),
            (0, i, lambda g: bias_ref[g, 0]),
            (1, jf, lambda g: wmask_ref[jnp.where(i >= 2, 0, 1)]),
            (1, jp, lambda g: bias_ref[g, prev_tile]),
            (1, i, lambda g: bias_ref[g, 0])]
    steps = [(st, branch, j, bias) for (branch, j, bias) in near for st in range(ns)]
    for n, (st, branch, j, _) in enumerate(steps):
        if n + 1 < len(steps):
            st2, branch2, j2, bias2 = steps[n + 1]
            smax_next = stage(st2, branch2, j2, (n + 1) % 2, bias2(streams[st2][1]))
        state = c_slc if branch == 0 else c_win
        state[st] = update(st, branch, j, n % 2, smax, state[st])
        smax = smax_next

    for bb in range(nb):
        gates = jax.nn.sigmoid(misc_ref[bb].T)
        for r in range(NSA_REP):
            halves = []
            for g in range(NSA_GROUPS):
                c0 = FOX_HEADS + 3 * (g * NSA_REP + r)
                sl = slice(r * tq, (r + 1) * tq)
                (_, accs), (_, accw) = c_slc[bb * NSA_GROUPS + g], c_win[bb * NSA_GROUPS + g]
                halves.append(
                    accs[0:HEAD_DIM, sl] * (gates[c0 + 1:c0 + 2, :] / accs[HEAD_DIM:HEAD_DIM + 1, sl])
                    + accw[0:HEAD_DIM, sl] * (gates[c0 + 2:c0 + 3, :] / accw[HEAD_DIM:HEAD_DIM + 1, sl]))
            o = jnp.concatenate(halves, axis=0).T + ocmp_ref[bb, :, r * LANES:(r + 1) * LANES]
            o_ref[bb, :, r * LANES:(r + 1) * LANES] = o.astype(BF16)


def _nsa_attention(qaug, nk, nvt, ocmp, misc, dist_bias, B, T, nb):
    tq = NSA_TILE
    assert WINDOW == 2 * tq and B % nb == 0
    nT = T // tq
    nq = NSA_REP * tq
    out = pl.pallas_call(
        functools.partial(_nsa_kernel, tq=tq, T=T, nb=nb),
        grid=(B // nb, nT),
        in_specs=[pl.BlockSpec((NSA_HEADS, nb, tq, LANES), lambda b, i: (0, b, i, 0)),
                  pl.BlockSpec((2 * NSA_GROUPS, nb, T, LANES), lambda b, i: (0, b, 0, 0)),
                  pl.BlockSpec((2 * KV_W, nb * T), lambda b, i: (0, b)),
                  pl.BlockSpec((nb, tq, NSA_REP * LANES), lambda b, i: (b, i, 0)),
                  pl.BlockSpec((nb, tq, LANES), lambda b, i: (b, i, 0)),
                  pl.BlockSpec((NSA_HEADS, 2 * tq), lambda b, i: (0, 0))],
        out_specs=pl.BlockSpec((nb, tq, NSA_REP * LANES), lambda b, i: (b, i, 0)),
        out_shape=jax.ShapeDtypeStruct((B, T, NSA_W), BF16),
        scratch_shapes=[pltpu.VMEM((2, tq, nq), F32),
                        pltpu.VMEM((nb * NSA_GROUPS, LANES, nq), BF16),
                        pltpu.VMEM((NSA_GROUPS, 4, tq, nq), F32),
                        pltpu.VMEM((2, tq, nq), F32)],
        compiler_params=pltpu.CompilerParams(dimension_semantics=("arbitrary", "arbitrary"),
                                             vmem_limit_bytes=VMEM_LIMIT),
        name="nsa_attn",
    )(qaug.reshape(NSA_HEADS, B, T, LANES), nk.reshape(2 * NSA_GROUPS, B, T, LANES), nvt,
      ocmp.reshape(B, T, NSA_REP * LANES), misc.reshape(B, T, LANES), dist_bias)
    return out.reshape(B * T, NSA_W)


def _merge_kernel(x_ref, g_ref, of_ref, on_ref, wga_ref, wgb_ref, wbf_ref, wbn_ref, wo_ref, x1_ref, *, sub):
    for r0 in range(0, x_ref.shape[0], sub):
        rows = slice(r0, r0 + sub)
        x = x_ref[rows, :]
        h = _rms(x, g_ref[...]).astype(BF16)
        ga = jax.nn.sigmoid(_dot(h, wga_ref[...]))
        gb = jax.nn.sigmoid(_dot(h, wgb_ref[...]))
        merged = ga * _dot(of_ref[rows, :], wbf_ref[...]) + gb * _dot(on_ref[rows, :], wbn_ref[...])
        x1_ref[rows, :] = x + _dot(merged.astype(BF16), wo_ref[...])


def _merge(x2, g, o_fox, o_nsa, wga, wgb, wbf, wbn, wo, tm):
    M = x2.shape[0]
    row = lambda i: (i, 0)
    full = lambda i: (0, 0)
    return pl.pallas_call(
        functools.partial(_merge_kernel, sub=tm // 2),
        grid=(M // tm,),
        in_specs=[pl.BlockSpec((tm, D_MODEL), row),
                  pl.BlockSpec((1, D_MODEL), full),
                  pl.BlockSpec((tm, FOX_W), row),
                  pl.BlockSpec((tm, NSA_W), row),
                  pl.BlockSpec((D_MODEL, D_MODEL), full),
                  pl.BlockSpec((D_MODEL, D_MODEL), full),
                  pl.BlockSpec((FOX_W, D_MODEL), full),
                  pl.BlockSpec((NSA_W, D_MODEL), full),
                  pl.BlockSpec((D_MODEL, D_MODEL), full)],
        out_specs=pl.BlockSpec((tm, D_MODEL), row),
        out_shape=jax.ShapeDtypeStruct((M, D_MODEL), F32),
        compiler_params=pltpu.CompilerParams(dimension_semantics=("parallel",), vmem_limit_bytes=VMEM_LIMIT),
        name="merge",
    )(x2, g, o_fox, o_nsa, wga, wgb, wbf, wbn, wo)


def _mlp_kernel(x_ref, g_ref, w1_ref, w2_ref, gf_ref, o_ref, *, final, sub):
    for r0 in range(0, x_ref.shape[0], sub):
        x = x_ref[r0:r0 + sub, :]
        h = _rms(x, g_ref[...]).astype(BF16)
        u = jnp.maximum(_dot(h, w1_ref[...]), 0.0)
        y = x + _dot((u * u).astype(BF16), w2_ref[...])
        o_ref[r0:r0 + sub, :] = _rms(y, gf_ref[...]) if final else y


def _mlp(x1, g, w1, w2, gf, tm, sub, final):
    M = x1.shape[0]
    row = lambda i: (i, 0)
    full = lambda i: (0, 0)
    once = pl.Buffered(1)
    return pl.pallas_call(
        functools.partial(_mlp_kernel, final=final, sub=sub),
        grid=(M // tm,),
        in_specs=[pl.BlockSpec((tm, D_MODEL), row),
                  pl.BlockSpec((1, D_MODEL), full),
                  pl.BlockSpec((D_MODEL, D_FF), full, pipeline_mode=once),
                  pl.BlockSpec((D_FF, D_MODEL), full, pipeline_mode=once),
                  pl.BlockSpec((1, D_MODEL), full)],
        out_specs=pl.BlockSpec((tm, D_MODEL), row),
        out_shape=jax.ShapeDtypeStruct((M, D_MODEL), F32),
        compiler_params=pltpu.CompilerParams(dimension_semantics=("parallel",), vmem_limit_bytes=VMEM_LIMIT),
        name="mlp",
    )(x1, g, w1, w2, gf)


def _bias_tables(rel_tbl, tq):
    L = 2 * tq
    g = (rel_tbl - rel_tbl[REL_BUCKETS - 1]).T[:, _rel_bucket_np(np.arange(L))] * LOG2E
    base = CMP_STRIDE * (CMP_BAND // 2) - (CMP_BLOCK - 1)
    left = CMP_STRIDE * (CMP_BAND - 1) - base
    gp = jnp.concatenate([jnp.zeros((NSA_HEADS, left), F32), g], axis=1)
    band = jnp.stack([gp[:, left + base - CMP_STRIDE * f:left + base - CMP_STRIDE * f + tq] for f in range(CMP_BAND)],
                     axis=-1)
    return g, band


def _layer(x2, B, T, rel_tbl, g_attn, w_in, b_forget, pe_k, w1_k, w2_k, pe_v, w1_v, w2_v,
           w_br_fox, w_br_nsa, w_out, g_mlp, w_ff1, w_ff2, g_final, final):
    M = B * T
    NR = T // CMP_STRIDE
    offs = np.cumsum((FOX_W, FOX_W, FOX_W, FOX_HEADS, NSA_W, KV_W, KV_W, KV_W, KV_W, KV_W, KV_W,
                      3 * NSA_HEADS, D_MODEL, D_MODEL))
    (w_fq, w_fk, w_fv, w_fl, w_nq, w_kc, w_vc, w_ksl, w_vsl, w_kwn, w_vwn, w_ng, w_ga, w_gb) = jnp.split(
        w_in, offs[:-1].tolist(), axis=-1)
    w_all = jnp.concatenate(
        [w_fq * SCALE, w_fk, w_nq * SCALE, w_ksl, w_kwn, w_fv, w_vsl, w_vwn, w_kc, w_vc, w_fl, w_ng,
         jnp.zeros((D_MODEL, _W_ALL - _SEG_MISC - FOX_HEADS - 3 * NSA_HEADS), F32)], axis=-1).astype(BF16)

    fq, fk, nq, nk, fv, nv, ckv, misc = _in_proj(x2, g_attn.reshape(1, D_MODEL), w_all, tm=512, T=T)

    bf_pad = jnp.zeros((1, LANES), F32).at[0, :FOX_HEADS].set(b_forget)
    c3 = _fox_decay(misc, bf_pad, B, T)
    o_fox = _fox_attention(fq, fk, fv, c3, B, T, tq=512, tk=512, nh=8)

    r4 = ckv.reshape(B, NR, CMP_STRIDE, 2 * NSA_GROUPS, HEAD_DIM).transpose(0, 3, 1, 2, 4)
    r4 = r4.reshape(B, 2 * NSA_GROUPS, NR, CMP_STRIDE * HEAD_DIM)
    w1 = jnp.stack([w1_k, w1_v]).astype(BF16)
    pe = jnp.stack([pe_k.reshape(1, -1), pe_v.reshape(1, -1)]).astype(BF16)
    pe = jnp.broadcast_to(pe, (2, 16, CMP_BLOCK * HEAD_DIM))
    zpad = jnp.zeros((CMP_HIDDEN, HEAD_DIM), F32)
    w2v_pad = jnp.stack([jnp.concatenate([w2_v, zpad], axis=1), jnp.concatenate([zpad, w2_v], axis=1)]).astype(BF16)
    w2k_pad = jnp.concatenate([w2_k, zpad], axis=1).astype(BF16)
    kc, vc = _compress(r4, w1, pe, w2k_pad, w2v_pad, B, NR)

    dist_bias, band = _bias_tables(rel_tbl, NSA_TILE)
    hi, mid, lo = _split3(band)
    bt = jnp.concatenate([hi, mid, lo, jnp.zeros_like(hi)], axis=-1)

    ocmp, qaug = _nsa_cmp(nq, kc, vc, bt, misc, B, T, NR)
    o_nsa = _nsa_attention(qaug, nk, nv, ocmp, misc, dist_bias, B, T, nb=2 if B % 2 == 0 else 1)

    wbn = w_br_nsa.reshape(NSA_GROUPS, NSA_REP, HEAD_DIM, D_MODEL).transpose(1, 0, 2, 3).reshape(NSA_W, D_MODEL)
    x1 = _merge(x2, g_attn.reshape(1, D_MODEL), o_fox, o_nsa, w_ga.astype(BF16), w_gb.astype(BF16),
                w_br_fox.astype(BF16), wbn.astype(BF16), w_out.astype(BF16), tm=512)
    return _mlp(x1, g_mlp.reshape(1, D_MODEL), w_ff1.astype(BF16), w_ff2.astype(BF16),
                g_final.reshape(1, D_MODEL), tm=512, sub=256, final=final)


def kernel(x, rel_bias_table, g_attn, w_in, b_forget, cmp_pe_k, cmp_w1_k, cmp_w2_k, cmp_pe_v, cmp_w1_v, cmp_w2_v,
           w_br_fox, w_br_nsa, w_out, g_mlp, w_ff1, w_ff2, g_final):
    B, T, _ = x.shape
    depth = g_attn.shape[0]
    x2 = x.reshape(B * T, D_MODEL)
    for l in range(depth):
        x2 = _layer(x2, B, T, rel_bias_table, g_attn[l], w_in[l], b_forget[l], cmp_pe_k[l], cmp_w1_k[l],
                    cmp_w2_k[l], cmp_pe_v[l], cmp_w1_v[l], cmp_w2_v[l], w_br_fox[l], w_br_nsa[l], w_out[l],
                    g_mlp[l], w_ff1[l], w_ff2[l], g_final, final=(l == depth - 1))
    return x2.reshape(B, T, D_MODEL)
```

```python
import functools
import math

import numpy as np
import jax
import jax.numpy as jnp
from jax import lax
from jax.experimental import pallas as pl
from jax.experimental.pallas import tpu as pltpu

F32 = jnp.float32
BF16 = jnp.bfloat16

D_MODEL = 1024
HEAD_DIM = 64
FOX_HEADS = 8
NSA_HEADS = 8
NSA_GROUPS = 2
NSA_REP = NSA_HEADS // NSA_GROUPS
CMP_BLOCK = 32
CMP_STRIDE = 16
CMP_HIDDEN = 256
SLC_BLOCK = 64
SLC_TOPK = 16
WINDOW = 512
REL_BUCKETS = 32
REL_MAX_DIST = 128
D_FF = 4 * D_MODEL
RMS_EPS = 1e-6
NEG_INF = -1e30
FORCED_SCORE = 1e4
SCALE = HEAD_DIM ** -0.5
LOG2E = math.log2(math.e)

FOX_W = FOX_HEADS * HEAD_DIM
NSA_W = NSA_HEADS * HEAD_DIM
KV_W = NSA_GROUPS * HEAD_DIM
LANES = 128
NSA_TILE = 256
REL_FAR = 113
CMP_BAND = 32

VMEM_LIMIT = 56 * 1024 * 1024

_NT = (((1,), (1,)), ((), ()))


def _dot(a, b):
    return jnp.dot(a, b, preferred_element_type=F32)


def _dot_nt(a, b):
    return lax.dot_general(a, b, _NT, preferred_element_type=F32)


def _split3(x):
    hi = x.astype(BF16)
    r1 = x - hi.astype(F32)
    mid = r1.astype(BF16)
    lo = (r1 - mid.astype(F32)).astype(BF16)
    return hi, mid, lo


def _rms(x, g):
    return x * lax.rsqrt(jnp.mean(x * x, axis=-1, keepdims=True) + RMS_EPS) * g


def _rel_bucket_np(n):
    exact = REL_BUCKETS // 2
    nf = np.maximum(n, exact).astype(np.float64)
    log_b = exact + (np.log(nf / exact) / math.log(REL_MAX_DIST / exact) * (REL_BUCKETS - exact)).astype(np.int64)
    return np.where(n < exact, n, np.minimum(log_b, REL_BUCKETS - 1))


_SEG_FQ, _SEG_FK, _SEG_NQ, _SEG_NK, _SEG_FV, _SEG_NV, _SEG_CKV, _SEG_MISC = (
    0, 512, 1024, 1536, 1792, 2304, 2560, 2816)
_W_ALL = 2944


_W_PREP_ROWS = 256


def _inproj_kernel(x_ref, g_ref, win_ref, fq_ref, fk_ref, nq_ref, nk_ref, fv_ref, nv_ref, ckv_ref, misc_ref,
                   ck_sc, cv_sc, w_ref, *, tm, T, segments):
    @pl.when(pl.program_id(0) == 0)
    def _():
        def rows(n, _):
            r0 = pl.multiple_of(n * _W_PREP_ROWS, _W_PREP_ROWS)
            for dst, src, width, scale in segments:
                w = win_ref[pl.ds(r0, _W_PREP_ROWS), src:src + width]
                w_ref[pl.ds(r0, _W_PREP_ROWS), dst:dst + width] = (w if scale == 1.0 else w * scale).astype(BF16)
            pad0 = segments[-1][0] + segments[-1][2]
            w_ref[pl.ds(r0, _W_PREP_ROWS), pad0:_W_ALL] = jnp.zeros((_W_PREP_ROWS, _W_ALL - pad0), BF16)
            return 0

        lax.fori_loop(0, D_MODEL // _W_PREP_ROWS, rows, 0)

    h = _rms(x_ref[...], g_ref[...]).astype(BF16)

    def seg(a, n):
        return _dot(h, w_ref[:, a:a + n])

    lane = lax.broadcasted_iota(jnp.int32, (1, LANES), 1)

    def heads_wide(ref, a, n_heads, mult, upper):
        r = seg(a, n_heads * HEAD_DIM)
        for j in range(n_heads):
            pair = r[:, (j // 2) * LANES:(j // 2 + 1) * LANES]
            if j % 2:
                pair = pltpu.roll(pair, HEAD_DIM, 1)
            ref[j] = jnp.where(lane < HEAD_DIM, pair * mult, upper(j)).astype(BF16)

    q_upper = jnp.where((lane == DECAY_LANES[0]) | (lane == DECAY_LANES[1]) | (lane == DECAY_LANES[2]), -1.0, 0.0)
    heads_wide(fq_ref, _SEG_FQ, FOX_HEADS, LOG2E, lambda j: q_upper)
    heads_wide(fk_ref, _SEG_FK, FOX_HEADS, 1.0, lambda j: 0.0)
    heads_wide(nq_ref, _SEG_NQ, NSA_HEADS, LOG2E, lambda j: 0.0)
    t = (pl.program_id(0) * tm + lax.broadcasted_iota(jnp.int32, (tm, 1), 0)) % T
    block_onehot = jnp.where(lane - HEAD_DIM == t // SLC_BLOCK, 1.0, 0.0)
    heads_wide(nk_ref, _SEG_NK, 2 * NSA_GROUPS, 1.0, lambda j: block_onehot if j < NSA_GROUPS else 0.0)
    fv_ref[...] = seg(_SEG_FV, FOX_W).astype(BF16).T
    nv_ref[...] = seg(_SEG_NV, 2 * KV_W).astype(BF16).T
    ckv = seg(_SEG_CKV, 2 * KV_W)
    for kind, sc in enumerate((ck_sc, cv_sc)):
        sc[...] = ckv[:, kind * LANES:(kind + 1) * LANES]
        for l in range(CMP_STRIDE):
            rows_l = sc[pl.ds(l, tm // CMP_STRIDE, stride=CMP_STRIDE), :]
            for g in range(NSA_GROUPS):
                ckv_ref[kind * NSA_GROUPS + g, :, l * HEAD_DIM:(l + 1) * HEAD_DIM] = (
                    rows_l[:, g * HEAD_DIM:(g + 1) * HEAD_DIM].astype(BF16))
    misc_ref[...] = seg(_SEG_MISC, LANES)


def _in_proj(x2, g, w_in, tm, T):
    M = x2.shape[0]
    assert T % tm == 0
    row = lambda i: (i, 0)
    hrow = lambda i: (0, i, 0)
    widths = (FOX_W, FOX_W, FOX_W, FOX_HEADS, NSA_W, KV_W, KV_W, KV_W, KV_W, KV_W, KV_W, 3 * NSA_HEADS)
    (fq, fk, fv, fl, nq, kc, vc, ksl, vsl, kwn, vwn, ng) = np.concatenate([[0], np.cumsum(widths)[:-1]]).tolist()
    order = [(fq, FOX_W, SCALE), (fk, FOX_W, 1.0), (nq, NSA_W, SCALE), (ksl, KV_W, 1.0), (kwn, KV_W, 1.0),
             (fv, FOX_W, 1.0), (vsl, KV_W, 1.0), (vwn, KV_W, 1.0), (kc, KV_W, 1.0), (vc, KV_W, 1.0),
             (fl, FOX_HEADS, 1.0), (ng, 3 * NSA_HEADS, 1.0)]
    segments, dst = [], 0
    for src, width, scale in order:
        segments.append((dst, src, width, scale))
        dst += width
    assert segments[2][0] == _SEG_NQ and segments[5][0] == _SEG_FV and segments[10][0] == _SEG_MISC
    return pl.pallas_call(
        functools.partial(_inproj_kernel, tm=tm, T=T, segments=tuple(segments)),
        grid=(M // tm,),
        in_specs=[pl.BlockSpec((tm, D_MODEL), row),
                  pl.BlockSpec((1, D_MODEL), lambda i: (0, 0)),
                  pl.BlockSpec((D_MODEL, w_in.shape[1]), lambda i: (0, 0), pipeline_mode=pl.Buffered(1))],
        out_specs=[pl.BlockSpec((FOX_HEADS, tm, LANES), hrow),
                   pl.BlockSpec((FOX_HEADS, tm, LANES), hrow),
                   pl.BlockSpec((NSA_HEADS, tm, LANES), hrow),
                   pl.BlockSpec((2 * NSA_GROUPS, tm, LANES), hrow),
                   pl.BlockSpec((FOX_W, tm), lambda i: (0, i)),
                   pl.BlockSpec((2 * KV_W, tm), lambda i: (0, i)),
                   pl.BlockSpec((2 * NSA_GROUPS, tm // CMP_STRIDE, CMP_STRIDE * HEAD_DIM), hrow),
                   pl.BlockSpec((tm, LANES), row)],
        out_shape=[jax.ShapeDtypeStruct((FOX_HEADS, M, LANES), BF16),
                   jax.ShapeDtypeStruct((FOX_HEADS, M, LANES), BF16),
                   jax.ShapeDtypeStruct((NSA_HEADS, M, LANES), BF16),
                   jax.ShapeDtypeStruct((2 * NSA_GROUPS, M, LANES), BF16),
                   jax.ShapeDtypeStruct((FOX_W, M), BF16),
                   jax.ShapeDtypeStruct((2 * KV_W, M), BF16),
                   jax.ShapeDtypeStruct((2 * NSA_GROUPS, M // CMP_STRIDE, CMP_STRIDE * HEAD_DIM), BF16),
                   jax.ShapeDtypeStruct((M, LANES), F32)],
        scratch_shapes=[pltpu.VMEM((tm, LANES), F32), pltpu.VMEM((tm, LANES), F32),
                        pltpu.VMEM((D_MODEL, _W_ALL), BF16)],
        compiler_params=pltpu.CompilerParams(dimension_semantics=("arbitrary",), vmem_limit_bytes=VMEM_LIMIT),
        name="in_proj",
    )(x2, g, w_in)


_SCAN_BLK = 256


DECAY_LANES = (HEAD_DIM, HEAD_DIM + FOX_HEADS, HEAD_DIM + 2 * FOX_HEADS)


def _decay_kernel(misc_ref, bf_ref, c3_ref, *, T):
    r = lax.broadcasted_iota(jnp.int32, (_SCAN_BLK, _SCAN_BLK), 0)
    c = lax.broadcasted_iota(jnp.int32, (_SCAN_BLK, _SCAN_BLK), 1)
    tri = jnp.where(r >= c, 1.0, 0.0).astype(BF16)
    lane = lax.broadcasted_iota(jnp.int32, (1, LANES), 1)

    def blk(n, carry):
        s0 = pl.multiple_of(n * _SCAN_BLK, _SCAN_BLK)
        x = misc_ref[pl.ds(s0, _SCAN_BLK), :] + bf_ref[...]
        lf = jnp.minimum(x, 0.0) - jnp.log1p(jnp.exp(-jnp.abs(x)))
        hi, mid, lo = _split3(lf)
        cs = _dot(tri, hi) + _dot(tri, mid) + _dot(tri, lo) + carry
        hi, mid, lo = (t.astype(F32) for t in _split3(cs * LOG2E))
        c3_ref[pl.ds(s0, _SCAN_BLK), :] = jnp.where(
            lane < FOX_HEADS, hi,
            jnp.where(lane < 2 * FOX_HEADS, pltpu.roll(mid, FOX_HEADS, 1),
                      jnp.where(lane < 3 * FOX_HEADS, pltpu.roll(lo, 2 * FOX_HEADS, 1), 0.0)))
        return cs[_SCAN_BLK - 1:_SCAN_BLK, :]

    lax.fori_loop(0, T // _SCAN_BLK, blk, jnp.zeros((1, LANES), F32))


def _fox_decay(misc, bf_pad, B, T):
    return pl.pallas_call(
        functools.partial(_decay_kernel, T=T),
        grid=(B,),
        in_specs=[pl.BlockSpec((T, LANES), lambda b: (b, 0)),
                  pl.BlockSpec((1, LANES), lambda b: (0, 0))],
        out_specs=pl.BlockSpec((T, LANES), lambda b: (b, 0)),
        out_shape=jax.ShapeDtypeStruct((B * T, LANES), F32),
        compiler_params=pltpu.CompilerParams(dimension_semantics=("parallel",), vmem_limit_bytes=VMEM_LIMIT),
        name="fox_decay",
    )(misc, bf_pad)


_FOX_PREP_BLK = 512


SUM_ROWS = 16


def _with_sum_row(vt):
    r = lax.broadcasted_iota(jnp.int32, (SUM_ROWS, vt.shape[1]), 0)
    return jnp.concatenate([vt, jnp.where(r == 0, 1.0, 0.0).astype(vt.dtype)], axis=0)


def _stage_scores(slot_ref, s):
    slot_ref[...] = s
    return jnp.max(s, axis=0, keepdims=True)


def _softmax_step(slot_ref, smax, vt, carry):
    m, acc = carry
    m_new = jnp.maximum(m, smax)
    p = jnp.exp2(slot_ref[...] - m_new).astype(BF16)
    return m_new, jnp.exp2(m - m_new) * acc + _dot(_with_sum_row(vt), p)


def _fox_kernel(q_ref, k_ref, vt_ref, c3_ref, mask_ref, o_ref, kaug_sc, s_sc, qt_sc, *, tq, tk, T, nh):
    hg = pl.program_id(1)
    i = pl.program_id(2)
    lane = lax.broadcasted_iota(jnp.int32, (1, LANES), 1)
    decay_lane = (lane == DECAY_LANES[0]) | (lane == DECAY_LANES[1]) | (lane == DECAY_LANES[2])

    @pl.when(i == 0)
    def _():
        for hh in range(nh):
            def prep(n, _, hh=hh):
                r0 = pl.multiple_of(n * _FOX_PREP_BLK, _FOX_PREP_BLK)
                c3 = pltpu.roll(c3_ref[pl.ds(r0, _FOX_PREP_BLK), :], HEAD_DIM - (nh * hg + hh), 1)
                kaug_sc[hh, pl.ds(r0, _FOX_PREP_BLK), :] = jnp.where(
                    lane < HEAD_DIM, k_ref[hh, pl.ds(r0, _FOX_PREP_BLK), :],
                    jnp.where(decay_lane, c3, 0.0).astype(BF16))
                return 0

            lax.fori_loop(0, T // _FOX_PREP_BLK, prep, 0)

    for hh in range(nh):
        qt_sc[hh] = q_ref[hh].T

    n_diag = tq // tk
    n_far = i * n_diag
    no_mask = n_diag

    def stage(hh, j, slot, mask_idx):
        ks = pl.multiple_of(j * tk, tk)
        s = _dot(kaug_sc[hh, pl.ds(ks, tk), :], qt_sc[hh])
        if mask_idx is not None:
            s = s + mask_ref[mask_idx]
        return _stage_scores(s_sc.at[slot], s)

    def update(hh, j, slot, smax, carry):
        ks = pl.multiple_of(j * tk, tk)
        return _softmax_step(s_sc.at[slot], smax, vt_ref[hh * HEAD_DIM:(hh + 1) * HEAD_DIM, pl.ds(ks, tk)], carry)

    def far(j, carry):
        state, smax = list(carry[0]), carry[1]
        for hh in range(nh):
            if hh + 1 < nh:
                smax_next = stage(hh + 1, j, (hh + 1) % 2, None)
            else:
                smax_next = stage(0, j + 1, 0, jnp.where(j + 1 == n_far, 0, no_mask))
            state[hh] = update(hh, j, hh % 2, smax, state[hh])
            smax = smax_next
        return tuple(state), smax

    init = (jnp.full((1, tq), NEG_INF, F32), jnp.zeros((HEAD_DIM + SUM_ROWS, tq), F32))
    smax = stage(0, 0, 0, jnp.where(n_far == 0, 0, no_mask))
    state, smax = lax.fori_loop(0, n_far, far, ((init,) * nh, smax))
    state = list(state)
    steps = [(hh, d) for d in range(n_diag) for hh in range(nh)]
    for n, (hh, d) in enumerate(steps):
        if n + 1 < len(steps):
            hh2, d2 = steps[n + 1]
            smax_next = stage(hh2, n_far + d2, (n + 1) % 2, d2)
        state[hh] = update(hh, n_far + d, n % 2, smax, state[hh])
        smax = smax_next
    outs = [acc[0:HEAD_DIM] * (1.0 / acc[HEAD_DIM:HEAD_DIM + 1]) for _, acc in state]
    o_ref[...] = jnp.concatenate(outs, axis=0).T.astype(BF16)


def _fox_attention(fq, fk, fvt, c3, B, T, tq, tk, nh):
    assert nh % 2 == 0 and FOX_HEADS % nh == 0
    nT = T // tq
    n_diag = tq // tk
    kk = np.arange(tk)[None, :, None] + tk * np.arange(n_diag)[:, None, None]
    mask = np.where(kk <= np.arange(tq)[None, None, :], 0.0, NEG_INF).astype(np.float32)
    mask = jnp.asarray(np.concatenate([mask, np.zeros((1, tk, tq), np.float32)]))
    return pl.pallas_call(
        functools.partial(_fox_kernel, tq=tq, tk=tk, T=T, nh=nh),
        grid=(B, FOX_HEADS // nh, nT),
        in_specs=[pl.BlockSpec((nh, tq, LANES), lambda b, hp, i: (hp, b * nT + i, 0)),
                  pl.BlockSpec((nh, T, LANES), lambda b, hp, i: (hp, b, 0)),
                  pl.BlockSpec((nh * HEAD_DIM, T), lambda b, hp, i: (hp, b)),
                  pl.BlockSpec((T, LANES), lambda b, hp, i: (b, 0)),
                  pl.BlockSpec((n_diag + 1, tk, tq), lambda b, hp, i: (0, 0, 0))],
        out_specs=pl.BlockSpec((tq, nh * HEAD_DIM), lambda b, hp, i: (b * nT + i, hp)),
        out_shape=jax.ShapeDtypeStruct((B * T, FOX_W), BF16),
        scratch_shapes=[pltpu.VMEM((nh, T, LANES), BF16),
                        pltpu.VMEM((2, tk, tq), F32),
                        pltpu.VMEM((nh, LANES, tq), BF16)],
        compiler_params=pltpu.CompilerParams(dimension_semantics=("parallel", "parallel", "arbitrary"),
                                             vmem_limit_bytes=VMEM_LIMIT),
        name="fox_attn",
    )(fq, fk, fvt, c3, mask)


def _compress_kernel(r_ref, w1_ref, pe_ref, w2k_ref, w2v_ref, kc_ref, vc_ref, *, NR):
    half = CMP_STRIDE * HEAD_DIM
    vacc = jnp.zeros((NR, LANES), F32)
    for idx in range(2 * NSA_GROUPS):
        kind, g = divmod(idx, NSA_GROUPS)
        rm = r_ref[idx]
        a = _dot(rm, w1_ref[kind, 0:half, :])
        bm = _dot(rm, w1_ref[kind, half:2 * half, :])
        pe_term = _dot(pe_ref[kind], w1_ref[kind])[0:1, :]
        pre = a + pltpu.roll(bm, NR - 1, 0) + pe_term
        hid = (pre * jax.nn.sigmoid(pre)).astype(BF16)
        if kind == 0:
            kc_ref[0, g] = _dot(hid, w2k_ref[...]).astype(BF16)
        else:
            vacc = vacc + _dot(hid, w2v_ref[g])
    vc_ref[0] = vacc.astype(BF16)


def _compress(r4, w1, pe, w2k, w2v_pad, B, NR):
    return pl.pallas_call(
        functools.partial(_compress_kernel, NR=NR),
        grid=(B,),
        in_specs=[pl.BlockSpec((2 * NSA_GROUPS, NR, CMP_STRIDE * HEAD_DIM), lambda b: (0, b, 0)),
                  pl.BlockSpec((2, CMP_BLOCK * HEAD_DIM, CMP_HIDDEN), lambda b: (0, 0, 0)),
                  pl.BlockSpec((2, 16, CMP_BLOCK * HEAD_DIM), lambda b: (0, 0, 0)),
                  pl.BlockSpec((CMP_HIDDEN, LANES), lambda b: (0, 0)),
                  pl.BlockSpec((NSA_GROUPS, CMP_HIDDEN, LANES), lambda b: (0, 0, 0))],
        out_specs=[pl.BlockSpec((1, NSA_GROUPS, NR, LANES), lambda b: (b, 0, 0, 0)),
                   pl.BlockSpec((1, NR, LANES), lambda b: (b, 0, 0))],
        out_shape=[jax.ShapeDtypeStruct((B, NSA_GROUPS, NR, LANES), BF16),
                   jax.ShapeDtypeStruct((B, NR, LANES), BF16)],
        compiler_params=pltpu.CompilerParams(dimension_semantics=("parallel",), vmem_limit_bytes=VMEM_LIMIT),
        name="compress",
    )(r4, w1, pe, w2k, w2v_pad)


def _cmp_kernel(q_ref, kc_ref, vc_ref, bt_ref, misc_ref, ocmp_ref, qaug_ref, bias_sc, val_sc, *, tq, NR):
    i = pl.program_id(0)
    b = pl.program_id(1)
    t0 = i * tq
    n_slc = LANES // 2
    gates = jax.nn.sigmoid(misc_ref[...])

    @pl.when(b == 0)
    def _():
        f = lax.broadcasted_iota(jnp.int32, (LANES, NR), 0)
        c = lax.broadcasted_iota(jnp.int32, (LANES, NR), 1)
        place = (((f % CMP_BAND) == (c - t0 // CMP_STRIDE + CMP_BAND // 2)) & (f < 3 * CMP_BAND))
        place = jnp.where(place, 1.0, 0.0).astype(BF16)
        for h in range(NSA_HEADS):
            bias_sc[h] = _dot(bt_ref[h], place)

    lane = lax.broadcasted_iota(jnp.int32, (1, LANES), 1)

    def gate_col(h):
        c0 = FOX_HEADS + 3 * h
        return gates[:, c0:c0 + 1]

    def tile(nc, nb):
        t = t0 + lax.broadcasted_iota(jnp.int32, (tq, 1), 0)
        c = lax.broadcasted_iota(jnp.int32, (1, nc), 1)
        cmask = (c * CMP_STRIDE + (CMP_BLOCK - 1)) <= t

        jj = lax.broadcasted_iota(jnp.int32, (LANES, nc), 0)
        cc = lax.broadcasted_iota(jnp.int32, (LANES, nc), 1)
        ov = ((cc * CMP_STRIDE < jj * SLC_BLOCK + SLC_BLOCK) & (cc * CMP_STRIDE + CMP_BLOCK > jj * SLC_BLOCK)
              & (jj < n_slc) & (cc < NR - 1))
        ov = jnp.where(ov, 1.0, 0.0).astype(BF16)

        jrow = lax.broadcasted_iota(jnp.int32, (nb, tq), 0)
        tt = t0 + lax.broadcasted_iota(jnp.int32, (nb, tq), 1)
        cur = tt // SLC_BLOCK
        forced = (jrow == 0) | (jrow == cur) | (jrow == cur - 1)
        valid = jrow * SLC_BLOCK <= tt

        outs = []
        for g in range(NSA_GROUPS):
            qs = q_ref[g * NSA_REP:(g + 1) * NSA_REP].reshape(NSA_REP * tq, LANES)
            s = (_dot_nt(qs, kc_ref[0, g, 0:nc, :]).reshape(NSA_REP, tq, nc)
                 + bias_sc[g * NSA_REP:(g + 1) * NSA_REP, :, 0:nc])
            s = jnp.where(cmask, s, NEG_INF)
            m = jnp.max(s, axis=-1, keepdims=True)
            e = jnp.where(cmask, jnp.exp2(s - m), 0.0)
            l = jnp.sum(e, axis=-1, keepdims=True)
            p = e * (1.0 / jnp.where(l > 0.0, l, 1.0))
            o = _dot(p.reshape(NSA_REP * tq, nc).astype(BF16), vc_ref[0, 0:nc, :])
            outs.append(o.reshape(NSA_REP, tq, LANES))

            if nb > SLC_TOPK:
                hi, mid, lo = _split3(jnp.sum(p, axis=0))
                imp = (_dot_nt(ov, hi) + _dot_nt(ov, mid) + _dot_nt(ov, lo))[0:nb, :]
                val_sc[0:nb, :] = jnp.where(forced, FORCED_SCORE, jnp.where(valid, imp, -1.0))
                vals = [val_sc[8 * a:8 * a + 8, :] for a in range(nb // 8)]
                ranks = [jnp.zeros((8, tq), F32) for _ in vals]
                j8 = lax.broadcasted_iota(jnp.int32, (8, 1), 0)
                for k in range(nb):
                    vk = val_sc[k:k + 1, :]
                    for a, va in enumerate(vals):
                        if 8 * a > k:
                            ahead = jnp.where(vk >= va, 1.0, 0.0)
                        elif 8 * a + 7 < k:
                            ahead = jnp.where(vk > va, 1.0, 0.0)
                        else:
                            tie = jnp.where(j8 > k - 8 * a, 1.0, 0.0)
                            ahead = jnp.where(vk > va, 1.0, jnp.where(vk == va, tie, 0.0))
                        ranks[a] = ranks[a] + ahead
                selneg = jnp.where(jnp.concatenate(ranks, axis=0) < float(SLC_TOPK), 0.0, NEG_INF)
            else:
                selneg = jnp.zeros((nb, tq), F32)
            pieces = [jnp.zeros((LANES - n_slc, tq), F32), selneg]
            if nb < n_slc:
                pieces.append(jnp.full((n_slc - nb, tq), NEG_INF, F32))
            selneg = jnp.concatenate(pieces, axis=0).T.astype(BF16)
            for r in range(NSA_REP):
                h = g * NSA_REP + r
                qaug_ref[h] = jnp.where(lane < HEAD_DIM, q_ref[h], selneg)

        for r in range(NSA_REP):
            ocmp_ref[:, r * LANES:(r + 1) * LANES] = jnp.where(
                lane < HEAD_DIM, outs[0][r] * gate_col(r), outs[1][r] * gate_col(NSA_REP + r))

    per = SLC_TOPK * SLC_BLOCK // tq
    variants = []
    for v in range(n_slc // SLC_TOPK):
        nb = SLC_TOPK * (v + 1)
        nc = min(NR, -(-(nb * SLC_BLOCK // CMP_STRIDE) // LANES) * LANES)
        variants.append(functools.partial(tile, nc, nb))
    lax.switch(jnp.minimum(i // per, len(variants) - 1), variants)


def _nsa_cmp(nq, kc, vc, bt, misc, B, T, NR):
    tq = NSA_TILE
    nT = T // tq
    M = B * T
    return pl.pallas_call(
        functools.partial(_cmp_kernel, tq=tq, NR=NR),
        grid=(nT, B),
        in_specs=[pl.BlockSpec((NSA_HEADS, tq, LANES), lambda i, b: (0, b * nT + i, 0)),
                  pl.BlockSpec((1, NSA_GROUPS, NR, LANES), lambda i, b: (b, 0, 0, 0)),
                  pl.BlockSpec((1, NR, LANES), lambda i, b: (b, 0, 0)),
                  pl.BlockSpec((NSA_HEADS, tq, LANES), lambda i, b: (0, 0, 0)),
                  pl.BlockSpec((tq, LANES), lambda i, b: (b * nT + i, 0))],
        out_specs=[pl.BlockSpec((tq, NSA_REP * LANES), lambda i, b: (b * nT + i, 0)),
                   pl.BlockSpec((NSA_HEADS, tq, LANES), lambda i, b: (0, b * nT + i, 0))],
        out_shape=[jax.ShapeDtypeStruct((M, NSA_REP * LANES), F32),
                   jax.ShapeDtypeStruct((NSA_HEADS, M, LANES), BF16)],
        scratch_shapes=[pltpu.VMEM((NSA_HEADS, tq, NR), F32),
                        pltpu.VMEM((LANES // 2, tq), F32)],
        compiler_params=pltpu.CompilerParams(dimension_semantics=("arbitrary", "arbitrary"),
                                             vmem_limit_bytes=VMEM_LIMIT),
        name="nsa_cmp",
    )(nq, kc, vc, bt, misc)


def _nsa_kernel(q_ref, k_ref, vt_ref, ocmp_ref, misc_ref, dist_ref, o_ref, s_sc, qt_sc, bias_ref, wmask_ref, *,
                tq, T, nb):
    i = pl.program_id(1)
    tk = tq
    nq = NSA_REP * tq
    jp = jnp.maximum(i - 1, 0)
    jf = jnp.maximum(i - 2, 0)

    @pl.when((pl.program_id(0) == 0) & (i == 0))
    def _():
        kk = lax.broadcasted_iota(jnp.int32, (tk, tq), 0)
        qq = lax.broadcasted_iota(jnp.int32, (tk, tq), 1)
        for h in range(NSA_HEADS):
            g, r = divmod(h, NSA_REP)
            rows = jnp.broadcast_to(dist_ref[h:h + 1, :], (tk, 2 * tq))
            pair = pltpu.roll(rows, 0, 1, stride=1, stride_axis=0)
            bias_ref[g, 0, :, r * tq:(r + 1) * tq] = jnp.where(kk <= qq, pair[:, 0:tq], NEG_INF)
            bias_ref[g, 1, :, r * tq:(r + 1) * tq] = pair[:, tq:2 * tq]
        for g in range(NSA_GROUPS):
            bias_ref[g, 2] = jnp.full((tk, nq), NEG_INF, F32)
            bias_ref[g, 3] = jnp.zeros((tk, nq), F32)
        for r in range(NSA_REP):
            wmask_ref[0, :, r * tq:(r + 1) * tq] = jnp.where(kk > qq, 0.0, NEG_INF)
        wmask_ref[1] = jnp.full((tk, nq), NEG_INF, F32)

    streams = [(bb, g) for bb in range(nb) for g in range(NSA_GROUPS)]
    ns = len(streams)

    for st, (bb, g) in enumerate(streams):
        qt_sc[st] = q_ref[g * NSA_REP:(g + 1) * NSA_REP, bb].reshape(nq, LANES).T

    def stage(st, branch, j, slot, bias):
        bb, g = streams[st]
        ks = pl.multiple_of(j * tk, tk)
        s = _dot(k_ref[branch * NSA_GROUPS + g, bb, pl.ds(ks, tk), :], qt_sc[st])
        if bias is not None:
            s = s + bias
        return _stage_scores(s_sc.at[slot], s)

    def update(st, branch, j, slot, smax, carry):
        bb, g = streams[st]
        ks = pl.multiple_of(bb * T + j * tk, tk)
        r0 = (branch * NSA_GROUPS + g) * HEAD_DIM
        return _softmax_step(s_sc.at[slot], smax, vt_ref[r0:r0 + HEAD_DIM, pl.ds(ks, tk)], carry)

    init = (jnp.full((1, nq), NEG_INF, F32), jnp.zeros((HEAD_DIM + SUM_ROWS, nq), F32))
    prev_tile = jnp.where(i >= 1, 1, 2)

    def far(j, carry):
        state, smax = list(carry[0]), carry[1]
        for st in range(ns):
            if st + 1 < ns:
                smax_next = stage(st + 1, 0, j, (st + 1) % 2, None)
            else:
                smax_next = stage(0, 0, j + 1, 0, bias_ref[0, jnp.where(j + 1 == jp, 1, 3)])
            state[st] = update(st, 0, j, st % 2, smax, state[st])
            smax = smax_next
        return tuple(state), smax

    smax = stage(0, 0, 0, 0, bias_ref[0, jnp.where(i >= 2, 3, prev_tile)])
    c_slc, smax = lax.fori_loop(0, jp, far, ((init,) * ns, smax))
    c_slc = list(c_slc)
    c_win = [init] * ns

    near = [(0, jp, lambda g: bias_ref[g, prev_tile]),
            (0, i, lambda g: bias_ref[g, 0]),
            (1, jf, lambda g: wmask_ref[jnp.where(i >= 2, 0, 1)]),
            (1, jp, lambda g: bias_ref[g, prev_tile]),
            (1, i, lambda g: bias_ref[g, 0])]
    steps = [(st, branch, j, bias) for (branch, j, bias) in near for st in range(ns)]
    for n, (st, branch, j, _) in enumerate(steps):
        if n + 1 < len(steps):
            st2, branch2, j2, bias2 = steps[n + 1]
            smax_next = stage(st2, branch2, j2, (n + 1) % 2, bias2(streams[st2][1]))
        state = c_slc if branch == 0 else c_win
        state[st] = update(st, branch, j, n % 2, smax, state[st])
        smax = smax_next

    for bb in range(nb):
        gates = jax.nn.sigmoid(misc_ref[bb].T)
        for r in range(NSA_REP):
            halves = []
            for g in range(NSA_GROUPS):
                c0 = FOX_HEADS + 3 * (g * NSA_REP + r)
                sl = slice(r * tq, (r + 1) * tq)
                (_, accs), (_, accw) = c_slc[bb * NSA_GROUPS + g], c_win[bb * NSA_GROUPS + g]
                halves.append(
                    accs[0:HEAD_DIM, sl] * (gates[c0 + 1:c0 + 2, :] / accs[HEAD_DIM:HEAD_DIM + 1, sl])
                    + accw[0:HEAD_DIM, sl] * (gates[c0 + 2:c0 + 3, :] / accw[HEAD_DIM:HEAD_DIM + 1, sl]))
            o = jnp.concatenate(halves, axis=0).T + ocmp_ref[bb, :, r * LANES:(r + 1) * LANES]
            o_ref[bb, :, r * LANES:(r + 1) * LANES] = o.astype(BF16)


def _nsa_attention(qaug, nk, nvt, ocmp, misc, dist_bias, B, T, nb):
    tq = NSA_TILE
    assert WINDOW == 2 * tq and B % nb == 0
    nT = T // tq
    nq = NSA_REP * tq
    out = pl.pallas_call(
        functools.partial(_nsa_kernel, tq=tq, T=T, nb=nb),
        grid=(B // nb, nT),
        in_specs=[pl.BlockSpec((NSA_HEADS, nb, tq, LANES), lambda b, i: (0, b, i, 0)),
                  pl.BlockSpec((2 * NSA_GROUPS, nb, T, LANES), lambda b, i: (0, b, 0, 0)),
                  pl.BlockSpec((2 * KV_W, nb * T), lambda b, i: (0, b)),
                  pl.BlockSpec((nb, tq, NSA_REP * LANES), lambda b, i: (b, i, 0)),
                  pl.BlockSpec((nb, tq, LANES), lambda b, i: (b, i, 0)),
                  pl.BlockSpec((NSA_HEADS, 2 * tq), lambda b, i: (0, 0))],
        out_specs=pl.BlockSpec((nb, tq, NSA_REP * LANES), lambda b, i: (b, i, 0)),
        out_shape=jax.ShapeDtypeStruct((B, T, NSA_W), BF16),
        scratch_shapes=[pltpu.VMEM((2, tq, nq), F32),
                        pltpu.VMEM((nb * NSA_GROUPS, LANES, nq), BF16),
                        pltpu.VMEM((NSA_GROUPS, 4, tq, nq), F32),
                        pltpu.VMEM((2, tq, nq), F32)],
        compiler_params=pltpu.CompilerParams(dimension_semantics=("arbitrary", "arbitrary"),
                                             vmem_limit_bytes=VMEM_LIMIT),
        name="nsa_attn",
    )(qaug.reshape(NSA_HEADS, B, T, LANES), nk.reshape(2 * NSA_GROUPS, B, T, LANES), nvt,
      ocmp.reshape(B, T, NSA_REP * LANES), misc.reshape(B, T, LANES), dist_bias)
    return out.reshape(B * T, NSA_W)


def _merge_kernel(x_ref, g_ref, of_ref, on_ref, wga_ref, wgb_ref, wbf_ref, wbn_nat_ref, wo_ref, x1_ref, wbn_ref, *,
                  sub):
    @pl.when(pl.program_id(0) == 0)
    def _():
        for g in range(NSA_GROUPS):
            for r in range(NSA_REP):
                src = (g * NSA_REP + r) * HEAD_DIM
                dst = (r * NSA_GROUPS + g) * HEAD_DIM
                wbn_ref[dst:dst + HEAD_DIM, :] = wbn_nat_ref[src:src + HEAD_DIM, :]

    for r0 in range(0, x_ref.shape[0], sub):
        rows = slice(r0, r0 + sub)
        x = x_ref[rows, :]
        h = _rms(x, g_ref[...]).astype(BF16)
        ga = jax.nn.sigmoid(_dot(h, wga_ref[...]))
        gb = jax.nn.sigmoid(_dot(h, wgb_ref[...]))
        merged = ga * _dot(of_ref[rows, :], wbf_ref[...]) + gb * _dot(on_ref[rows, :], wbn_ref[...])
        x1_ref[rows, :] = x + _dot(merged.astype(BF16), wo_ref[...])


def _merge(x2, g, o_fox, o_nsa, wga, wgb, wbf, wbn, wo, tm):
    M = x2.shape[0]
    row = lambda i: (i, 0)
    full = lambda i: (0, 0)
    return pl.pallas_call(
        functools.partial(_merge_kernel, sub=tm // 2),
        grid=(M // tm,),
        in_specs=[pl.BlockSpec((tm, D_MODEL), row),
                  pl.BlockSpec((1, D_MODEL), full),
                  pl.BlockSpec((tm, FOX_W), row),
                  pl.BlockSpec((tm, NSA_W), row),
                  pl.BlockSpec((D_MODEL, D_MODEL), full),
                  pl.BlockSpec((D_MODEL, D_MODEL), full),
                  pl.BlockSpec((FOX_W, D_MODEL), full),
                  pl.BlockSpec((NSA_W, D_MODEL), full),
                  pl.BlockSpec((D_MODEL, D_MODEL), full)],
        out_specs=pl.BlockSpec((tm, D_MODEL), row),
        out_shape=jax.ShapeDtypeStruct((M, D_MODEL), F32),
        scratch_shapes=[pltpu.VMEM((NSA_W, D_MODEL), BF16)],
        compiler_params=pltpu.CompilerParams(dimension_semantics=("arbitrary",), vmem_limit_bytes=VMEM_LIMIT),
        name="merge",
    )(x2, g, o_fox, o_nsa, wga, wgb, wbf, wbn, wo)


def _mlp_kernel(x_ref, g_ref, w1_ref, w2_ref, gf_ref, o_ref, *, final, sub):
    for r0 in range(0, x_ref.shape[0], sub):
        x = x_ref[r0:r0 + sub, :]
        h = _rms(x, g_ref[...]).astype(BF16)
        u = jnp.maximum(_dot(h, w1_ref[...]), 0.0)
        y = x + _dot((u * u).astype(BF16), w2_ref[...])
        o_ref[r0:r0 + sub, :] = _rms(y, gf_ref[...]) if final else y


def _mlp(x1, g, w1, w2, gf, tm, sub, final):
    M = x1.shape[0]
    row = lambda i: (i, 0)
    full = lambda i: (0, 0)
    once = pl.Buffered(1)
    return pl.pallas_call(
        functools.partial(_mlp_kernel, final=final, sub=sub),
        grid=(M // tm,),
        in_specs=[pl.BlockSpec((tm, D_MODEL), row),
                  pl.BlockSpec((1, D_MODEL), full),
                  pl.BlockSpec((D_MODEL, D_FF), full, pipeline_mode=once),
                  pl.BlockSpec((D_FF, D_MODEL), full, pipeline_mode=once),
                  pl.BlockSpec((1, D_MODEL), full)],
        out_specs=pl.BlockSpec((tm, D_MODEL), row),
        out_shape=jax.ShapeDtypeStruct((M, D_MODEL), F32),
        compiler_params=pltpu.CompilerParams(dimension_semantics=("parallel",), vmem_limit_bytes=VMEM_LIMIT),
        name="mlp",
    )(x1, g, w1, w2, gf)


def _bias_tables(rel_tbl, tq):
    L = 2 * tq
    g = (rel_tbl - rel_tbl[REL_BUCKETS - 1]).T[:, _rel_bucket_np(np.arange(L))] * LOG2E
    base = CMP_STRIDE * (CMP_BAND // 2) - (CMP_BLOCK - 1)
    left = CMP_STRIDE * (CMP_BAND - 1) - base
    gp = jnp.concatenate([jnp.zeros((NSA_HEADS, left), F32), g], axis=1)
    n = left + L
    shifted = jnp.tile(gp, (1, CMP_BAND))[:, :CMP_BAND * (n - CMP_STRIDE)].reshape(NSA_HEADS, CMP_BAND, n - CMP_STRIDE)
    band = jnp.swapaxes(shifted[:, :, left + base:left + base + tq], 1, 2)
    return g, band


def _layer(x2, B, T, rel_tbl, g_attn, w_in, b_forget, pe_k, w1_k, w2_k, pe_v, w1_v, w2_v,
           w_br_fox, w_br_nsa, w_out, g_mlp, w_ff1, w_ff2, g_final, final):
    M = B * T
    NR = T // CMP_STRIDE
    gates0 = w_in.shape[1] - 2 * D_MODEL
    w_ga, w_gb = w_in[:, gates0:gates0 + D_MODEL], w_in[:, gates0 + D_MODEL:]

    fq, fk, nq, nk, fv, nv, r4, misc = _in_proj(x2, g_attn.reshape(1, D_MODEL), w_in, tm=512, T=T)

    bf_pad = jnp.zeros((1, LANES), F32).at[0, :FOX_HEADS].set(b_forget)
    c3 = _fox_decay(misc, bf_pad, B, T)
    o_fox = _fox_attention(fq, fk, fv, c3, B, T, tq=512, tk=512, nh=8)

    w1 = jnp.stack([w1_k, w1_v]).astype(BF16)
    pe = jnp.stack([pe_k.reshape(1, -1), pe_v.reshape(1, -1)]).astype(BF16)
    pe = jnp.broadcast_to(pe, (2, 16, CMP_BLOCK * HEAD_DIM))
    zpad = jnp.zeros((CMP_HIDDEN, HEAD_DIM), F32)
    w2v_pad = jnp.stack([jnp.concatenate([w2_v, zpad], axis=1), jnp.concatenate([zpad, w2_v], axis=1)]).astype(BF16)
    w2k_pad = jnp.concatenate([w2_k, zpad], axis=1).astype(BF16)
    kc, vc = _compress(r4, w1, pe, w2k_pad, w2v_pad, B, NR)

    dist_bias, band = _bias_tables(rel_tbl, NSA_TILE)
    hi, mid, lo = _split3(band)
    bt = jnp.concatenate([hi, mid, lo, jnp.zeros_like(hi)], axis=-1)

    ocmp, qaug = _nsa_cmp(nq, kc, vc, bt, misc, B, T, NR)
    o_nsa = _nsa_attention(qaug, nk, nv, ocmp, misc, dist_bias, B, T, nb=2 if B % 2 == 0 else 1)

    x1 = _merge(x2, g_attn.reshape(1, D_MODEL), o_fox, o_nsa, w_ga.astype(BF16), w_gb.astype(BF16),
                w_br_fox.astype(BF16), w_br_nsa.astype(BF16), w_out.astype(BF16), tm=512)
    return _mlp(x1, g_mlp.reshape(1, D_MODEL), w_ff1.astype(BF16), w_ff2.astype(BF16),
                g_final.reshape(1, D_MODEL), tm=512, sub=256, final=final)


def kernel(x, rel_bias_table, g_attn, w_in, b_forget, cmp_pe_k, cmp_w1_k, cmp_w2_k, cmp_pe_v, cmp_w1_v, cmp_w2_v,
           w_br_fox, w_br_nsa, w_out, g_mlp, w_ff1, w_ff2, g_final):
    B, T, _ = x.shape
    depth = g_attn.shape[0]
    x2 = x.reshape(B * T, D_MODEL)
    for l in range(depth):
        x2 = _layer(x2, B, T, rel_bias_table, g_attn[l], w_in[l], b_forget[l], cmp_pe_k[l], cmp_w1_k[l],
                    cmp_w2_k[l], cmp_pe_v[l], cmp_w1_v[l], cmp_w2_v[l], w_br_fox[l], w_br_nsa[l], w_out[l],
                    g_mlp[l], w_ff1[l], w_ff2[l], g_final, final=(l == depth - 1))
    return x2.reshape(B, T, D_MODEL)
```

```python
import functools
import math

import numpy as np
import jax
import jax.numpy as jnp
from jax import lax
from jax.experimental import pallas as pl
from jax.experimental.pallas import tpu as pltpu

F32 = jnp.float32
BF16 = jnp.bfloat16

D_MODEL = 1024
HEAD_DIM = 64
FOX_HEADS = 8
NSA_HEADS = 8
NSA_GROUPS = 2
NSA_REP = NSA_HEADS // NSA_GROUPS
CMP_BLOCK = 32
CMP_STRIDE = 16
CMP_HIDDEN = 256
SLC_BLOCK = 64
SLC_TOPK = 16
WINDOW = 512
REL_BUCKETS = 32
REL_MAX_DIST = 128
D_FF = 4 * D_MODEL
RMS_EPS = 1e-6
NEG_INF = -1e30
FORCED_SCORE = 1e4
SCALE = HEAD_DIM ** -0.5
LOG2E = math.log2(math.e)

FOX_W = FOX_HEADS * HEAD_DIM
NSA_W = NSA_HEADS * HEAD_DIM
KV_W = NSA_GROUPS * HEAD_DIM
LANES = 128
NSA_TILE = 256
REL_FAR = 113
CMP_BAND = 32

VMEM_LIMIT = 56 * 1024 * 1024

_NT = (((1,), (1,)), ((), ()))


def _dot(a, b):
    return jnp.dot(a, b, preferred_element_type=F32)


def _dot_nt(a, b):
    return lax.dot_general(a, b, _NT, preferred_element_type=F32)


def _split3(x):
    hi = x.astype(BF16)
    r1 = x - hi.astype(F32)
    mid = r1.astype(BF16)
    lo = (r1 - mid.astype(F32)).astype(BF16)
    return hi, mid, lo


def _rms(x, g):
    return x * lax.rsqrt(jnp.mean(x * x, axis=-1, keepdims=True) + RMS_EPS) * g


def _rel_bucket_np(n):
    exact = REL_BUCKETS // 2
    nf = np.maximum(n, exact).astype(np.float64)
    log_b = exact + (np.log(nf / exact) / math.log(REL_MAX_DIST / exact) * (REL_BUCKETS - exact)).astype(np.int64)
    return np.where(n < exact, n, np.minimum(log_b, REL_BUCKETS - 1))


_SEG_FQ, _SEG_FK, _SEG_NQ, _SEG_NK, _SEG_FV, _SEG_NV, _SEG_CKV, _SEG_MISC = (
    0, 512, 1024, 1536, 1792, 2304, 2560, 2816)
_W_ALL = 2944


def _inproj_kernel(x_ref, g_ref, wt_ref, fq_ref, fk_ref, nq_ref, nk_ref, fv_ref, nv_ref, ckv_ref, misc_ref,
                   wga_ref, wgb_ref, ck_sc, cv_sc, w_ref, *, tm, T, segments):
    @pl.when(pl.program_id(0) == 0)
    def _():
        narrow = []
        for dst, src, width, scale in segments:
            if width % LANES:
                narrow.append(wt_ref[src:src + width, :])
                continue
            for c0 in range(0, width, LANES):
                slab = wt_ref[src + c0:src + c0 + LANES, :]
                w_ref[:, dst + c0:dst + c0 + LANES] = (slab if scale == 1.0 else slab * scale).T.astype(BF16)
        pad = LANES - sum(s.shape[0] for s in narrow)
        slab = jnp.concatenate(narrow + [jnp.zeros((pad, D_MODEL), F32)], axis=0)
        w_ref[:, _SEG_MISC:_SEG_MISC + LANES] = slab.T.astype(BF16)
        gates0 = wt_ref.shape[0] - 2 * D_MODEL
        for n, ref in enumerate((wga_ref, wgb_ref)):
            for c0 in range(0, D_MODEL, LANES):
                src = gates0 + n * D_MODEL + c0
                ref[:, c0:c0 + LANES] = wt_ref[src:src + LANES, :].T.astype(BF16)

    h = _rms(x_ref[...], g_ref[...]).astype(BF16)

    def seg(a, n):
        return _dot(h, w_ref[:, a:a + n])

    lane = lax.broadcasted_iota(jnp.int32, (1, LANES), 1)

    def heads_wide(ref, a, n_heads, mult, upper):
        r = seg(a, n_heads * HEAD_DIM)
        for j in range(n_heads):
            pair = r[:, (j // 2) * LANES:(j // 2 + 1) * LANES]
            if j % 2:
                pair = pltpu.roll(pair, HEAD_DIM, 1)
            ref[j] = jnp.where(lane < HEAD_DIM, pair * mult, upper(j)).astype(BF16)

    q_upper = jnp.where((lane == DECAY_LANES[0]) | (lane == DECAY_LANES[1]) | (lane == DECAY_LANES[2]), -1.0, 0.0)
    heads_wide(fq_ref, _SEG_FQ, FOX_HEADS, LOG2E, lambda j: q_upper)
    heads_wide(fk_ref, _SEG_FK, FOX_HEADS, 1.0, lambda j: 0.0)
    heads_wide(nq_ref, _SEG_NQ, NSA_HEADS, LOG2E, lambda j: 0.0)
    t = (pl.program_id(0) * tm + lax.broadcasted_iota(jnp.int32, (tm, 1), 0)) % T
    block_onehot = jnp.where(lane - HEAD_DIM == t // SLC_BLOCK, 1.0, 0.0)
    heads_wide(nk_ref, _SEG_NK, 2 * NSA_GROUPS, 1.0, lambda j: block_onehot if j < NSA_GROUPS else 0.0)
    fv_ref[...] = seg(_SEG_FV, FOX_W).astype(BF16).T
    nv_ref[...] = seg(_SEG_NV, 2 * KV_W).astype(BF16).T
    ckv = seg(_SEG_CKV, 2 * KV_W)
    for kind, sc in enumerate((ck_sc, cv_sc)):
        sc[...] = ckv[:, kind * LANES:(kind + 1) * LANES]
        for l in range(CMP_STRIDE):
            rows_l = sc[pl.ds(l, tm // CMP_STRIDE, stride=CMP_STRIDE), :]
            for g in range(NSA_GROUPS):
                ckv_ref[kind * NSA_GROUPS + g, :, l * HEAD_DIM:(l + 1) * HEAD_DIM] = (
                    rows_l[:, g * HEAD_DIM:(g + 1) * HEAD_DIM].astype(BF16))
    misc_ref[...] = seg(_SEG_MISC, LANES)


def _in_proj(x2, g, w_in, tm, T):
    M = x2.shape[0]
    assert T % tm == 0
    row = lambda i: (i, 0)
    hrow = lambda i: (0, i, 0)
    widths = (FOX_W, FOX_W, FOX_W, FOX_HEADS, NSA_W, KV_W, KV_W, KV_W, KV_W, KV_W, KV_W, 3 * NSA_HEADS)
    (fq, fk, fv, fl, nq, kc, vc, ksl, vsl, kwn, vwn, ng) = np.concatenate([[0], np.cumsum(widths)[:-1]]).tolist()
    order = [(fq, FOX_W, SCALE), (fk, FOX_W, 1.0), (nq, NSA_W, SCALE), (ksl, KV_W, 1.0), (kwn, KV_W, 1.0),
             (fv, FOX_W, 1.0), (vsl, KV_W, 1.0), (vwn, KV_W, 1.0), (kc, KV_W, 1.0), (vc, KV_W, 1.0),
             (fl, FOX_HEADS, 1.0), (ng, 3 * NSA_HEADS, 1.0)]
    segments, dst = [], 0
    for src, width, scale in order:
        segments.append((dst, src, width, scale))
        dst += width
    assert segments[2][0] == _SEG_NQ and segments[5][0] == _SEG_FV and segments[10][0] == _SEG_MISC
    return pl.pallas_call(
        functools.partial(_inproj_kernel, tm=tm, T=T, segments=tuple(segments)),
        grid=(M // tm,),
        in_specs=[pl.BlockSpec((tm, D_MODEL), row),
                  pl.BlockSpec((1, D_MODEL), lambda i: (0, 0)),
                  pl.BlockSpec((w_in.shape[1], D_MODEL), lambda i: (0, 0), pipeline_mode=pl.Buffered(1))],
        out_specs=[pl.BlockSpec((FOX_HEADS, tm, LANES), hrow),
                   pl.BlockSpec((FOX_HEADS, tm, LANES), hrow),
                   pl.BlockSpec((NSA_HEADS, tm, LANES), hrow),
                   pl.BlockSpec((2 * NSA_GROUPS, tm, LANES), hrow),
                   pl.BlockSpec((FOX_W, tm), lambda i: (0, i)),
                   pl.BlockSpec((2 * KV_W, tm), lambda i: (0, i)),
                   pl.BlockSpec((2 * NSA_GROUPS, tm // CMP_STRIDE, CMP_STRIDE * HEAD_DIM), hrow),
                   pl.BlockSpec((tm, LANES), row),
                   pl.BlockSpec((D_MODEL, D_MODEL), lambda i: (0, 0)),
                   pl.BlockSpec((D_MODEL, D_MODEL), lambda i: (0, 0))],
        out_shape=[jax.ShapeDtypeStruct((FOX_HEADS, M, LANES), BF16),
                   jax.ShapeDtypeStruct((FOX_HEADS, M, LANES), BF16),
                   jax.ShapeDtypeStruct((NSA_HEADS, M, LANES), BF16),
                   jax.ShapeDtypeStruct((2 * NSA_GROUPS, M, LANES), BF16),
                   jax.ShapeDtypeStruct((FOX_W, M), BF16),
                   jax.ShapeDtypeStruct((2 * KV_W, M), BF16),
                   jax.ShapeDtypeStruct((2 * NSA_GROUPS, M // CMP_STRIDE, CMP_STRIDE * HEAD_DIM), BF16),
                   jax.ShapeDtypeStruct((M, LANES), F32),
                   jax.ShapeDtypeStruct((D_MODEL, D_MODEL), BF16),
                   jax.ShapeDtypeStruct((D_MODEL, D_MODEL), BF16)],
        scratch_shapes=[pltpu.VMEM((tm, LANES), F32), pltpu.VMEM((tm, LANES), F32),
                        pltpu.VMEM((D_MODEL, _W_ALL), BF16)],
        compiler_params=pltpu.CompilerParams(dimension_semantics=("arbitrary",), vmem_limit_bytes=VMEM_LIMIT),
        name="in_proj",
    )(x2, g, w_in.T)


_SCAN_BLK = 256


DECAY_LANES = (HEAD_DIM, HEAD_DIM + FOX_HEADS, HEAD_DIM + 2 * FOX_HEADS)


def _decay_kernel(misc_ref, bf_ref, c3_ref, *, T):
    r = lax.broadcasted_iota(jnp.int32, (_SCAN_BLK, _SCAN_BLK), 0)
    c = lax.broadcasted_iota(jnp.int32, (_SCAN_BLK, _SCAN_BLK), 1)
    tri = jnp.where(r >= c, 1.0, 0.0).astype(BF16)
    lane = lax.broadcasted_iota(jnp.int32, (1, LANES), 1)

    def blk(n, carry):
        s0 = pl.multiple_of(n * _SCAN_BLK, _SCAN_BLK)
        x = misc_ref[pl.ds(s0, _SCAN_BLK), :] + bf_ref[...]
        lf = jnp.minimum(x, 0.0) - jnp.log1p(jnp.exp(-jnp.abs(x)))
        hi, mid, lo = _split3(lf)
        cs = _dot(tri, hi) + _dot(tri, mid) + _dot(tri, lo) + carry
        hi, mid, lo = (t.astype(F32) for t in _split3(cs * LOG2E))
        c3_ref[pl.ds(s0, _SCAN_BLK), :] = jnp.where(
            lane < FOX_HEADS, hi,
            jnp.where(lane < 2 * FOX_HEADS, pltpu.roll(mid, FOX_HEADS, 1),
                      jnp.where(lane < 3 * FOX_HEADS, pltpu.roll(lo, 2 * FOX_HEADS, 1), 0.0)))
        return cs[_SCAN_BLK - 1:_SCAN_BLK, :]

    lax.fori_loop(0, T // _SCAN_BLK, blk, jnp.zeros((1, LANES), F32))


def _fox_decay(misc, bf_pad, B, T):
    return pl.pallas_call(
        functools.partial(_decay_kernel, T=T),
        grid=(B,),
        in_specs=[pl.BlockSpec((T, LANES), lambda b: (b, 0)),
                  pl.BlockSpec((1, LANES), lambda b: (0, 0))],
        out_specs=pl.BlockSpec((T, LANES), lambda b: (b, 0)),
        out_shape=jax.ShapeDtypeStruct((B * T, LANES), F32),
        compiler_params=pltpu.CompilerParams(dimension_semantics=("parallel",), vmem_limit_bytes=VMEM_LIMIT),
        name="fox_decay",
    )(misc, bf_pad)


_FOX_PREP_BLK = 512


SUM_ROWS = 16


def _with_sum_row(vt):
    r = lax.broadcasted_iota(jnp.int32, (SUM_ROWS, vt.shape[1]), 0)
    return jnp.concatenate([vt, jnp.where(r == 0, 1.0, 0.0).astype(vt.dtype)], axis=0)


def _stage_scores(slot_ref, s):
    slot_ref[...] = s
    return jnp.max(s, axis=0, keepdims=True)


def _softmax_step(slot_ref, smax, vt, carry):
    m, acc = carry
    m_new = jnp.maximum(m, smax)
    p = jnp.exp2(slot_ref[...] - m_new).astype(BF16)
    return m_new, jnp.exp2(m - m_new) * acc + _dot(_with_sum_row(vt), p)


def _fox_kernel(q_ref, k_ref, vt_ref, c3_ref, mask_ref, o_ref, kaug_sc, s_sc, qt_sc, *, tq, tk, T, nh):
    hg = pl.program_id(1)
    i = pl.program_id(2)
    lane = lax.broadcasted_iota(jnp.int32, (1, LANES), 1)
    decay_lane = (lane == DECAY_LANES[0]) | (lane == DECAY_LANES[1]) | (lane == DECAY_LANES[2])

    @pl.when(i == 0)
    def _():
        for hh in range(nh):
            def prep(n, _, hh=hh):
                r0 = pl.multiple_of(n * _FOX_PREP_BLK, _FOX_PREP_BLK)
                c3 = pltpu.roll(c3_ref[pl.ds(r0, _FOX_PREP_BLK), :], HEAD_DIM - (nh * hg + hh), 1)
                kaug_sc[hh, pl.ds(r0, _FOX_PREP_BLK), :] = jnp.where(
                    lane < HEAD_DIM, k_ref[hh, pl.ds(r0, _FOX_PREP_BLK), :],
                    jnp.where(decay_lane, c3, 0.0).astype(BF16))
                return 0

            lax.fori_loop(0, T // _FOX_PREP_BLK, prep, 0)

    for hh in range(nh):
        qt_sc[hh] = q_ref[hh].T

    n_diag = tq // tk
    n_far = i * n_diag
    no_mask = n_diag

    def stage(hh, j, slot, mask_idx):
        ks = pl.multiple_of(j * tk, tk)
        s = _dot(kaug_sc[hh, pl.ds(ks, tk), :], qt_sc[hh])
        if mask_idx is not None:
            s = s + mask_ref[mask_idx]
        return _stage_scores(s_sc.at[slot], s)

    def update(hh, j, slot, smax, carry):
        ks = pl.multiple_of(j * tk, tk)
        return _softmax_step(s_sc.at[slot], smax, vt_ref[hh * HEAD_DIM:(hh + 1) * HEAD_DIM, pl.ds(ks, tk)], carry)

    def far(j, carry):
        state, smax = list(carry[0]), carry[1]
        for hh in range(nh):
            if hh + 1 < nh:
                smax_next = stage(hh + 1, j, (hh + 1) % 2, None)
            else:
                smax_next = stage(0, j + 1, 0, jnp.where(j + 1 == n_far, 0, no_mask))
            state[hh] = update(hh, j, hh % 2, smax, state[hh])
            smax = smax_next
        return tuple(state), smax

    init = (jnp.full((1, tq), NEG_INF, F32), jnp.zeros((HEAD_DIM + SUM_ROWS, tq), F32))
    smax = stage(0, 0, 0, jnp.where(n_far == 0, 0, no_mask))
    state, smax = lax.fori_loop(0, n_far, far, ((init,) * nh, smax))
    state = list(state)
    steps = [(hh, d) for d in range(n_diag) for hh in range(nh)]
    for n, (hh, d) in enumerate(steps):
        if n + 1 < len(steps):
            hh2, d2 = steps[n + 1]
            smax_next = stage(hh2, n_far + d2, (n + 1) % 2, d2)
        state[hh] = update(hh, n_far + d, n % 2, smax, state[hh])
        smax = smax_next
    outs = [acc[0:HEAD_DIM] * (1.0 / acc[HEAD_DIM:HEAD_DIM + 1]) for _, acc in state]
    o_ref[...] = jnp.concatenate(outs, axis=0).T.astype(BF16)


def _fox_attention(fq, fk, fvt, c3, B, T, tq, tk, nh):
    assert nh % 2 == 0 and FOX_HEADS % nh == 0
    nT = T // tq
    n_diag = tq // tk
    kk = np.arange(tk)[None, :, None] + tk * np.arange(n_diag)[:, None, None]
    mask = np.where(kk <= np.arange(tq)[None, None, :], 0.0, NEG_INF).astype(np.float32)
    mask = jnp.asarray(np.concatenate([mask, np.zeros((1, tk, tq), np.float32)]))
    return pl.pallas_call(
        functools.partial(_fox_kernel, tq=tq, tk=tk, T=T, nh=nh),
        grid=(B, FOX_HEADS // nh, nT),
        in_specs=[pl.BlockSpec((nh, tq, LANES), lambda b, hp, i: (hp, b * nT + i, 0)),
                  pl.BlockSpec((nh, T, LANES), lambda b, hp, i: (hp, b, 0)),
                  pl.BlockSpec((nh * HEAD_DIM, T), lambda b, hp, i: (hp, b)),
                  pl.BlockSpec((T, LANES), lambda b, hp, i: (b, 0)),
                  pl.BlockSpec((n_diag + 1, tk, tq), lambda b, hp, i: (0, 0, 0))],
        out_specs=pl.BlockSpec((tq, nh * HEAD_DIM), lambda b, hp, i: (b * nT + i, hp)),
        out_shape=jax.ShapeDtypeStruct((B * T, FOX_W), BF16),
        scratch_shapes=[pltpu.VMEM((nh, T, LANES), BF16),
                        pltpu.VMEM((2, tk, tq), F32),
                        pltpu.VMEM((nh, LANES, tq), BF16)],
        compiler_params=pltpu.CompilerParams(dimension_semantics=("parallel", "parallel", "arbitrary"),
                                             vmem_limit_bytes=VMEM_LIMIT),
        name="fox_attn",
    )(fq, fk, fvt, c3, mask)


def _compress_kernel(r_ref, w1_ref, pe_ref, w2k_ref, w2v_ref, kc_ref, vc_ref, *, NR):
    half = CMP_STRIDE * HEAD_DIM
    vacc = jnp.zeros((NR, LANES), F32)
    for idx in range(2 * NSA_GROUPS):
        kind, g = divmod(idx, NSA_GROUPS)
        rm = r_ref[idx]
        a = _dot(rm, w1_ref[kind, 0:half, :])
        bm = _dot(rm, w1_ref[kind, half:2 * half, :])
        pe_term = _dot(pe_ref[kind], w1_ref[kind])[0:1, :]
        pre = a + pltpu.roll(bm, NR - 1, 0) + pe_term
        hid = (pre * jax.nn.sigmoid(pre)).astype(BF16)
        if kind == 0:
            kc_ref[0, g] = _dot(hid, w2k_ref[...]).astype(BF16)
        else:
            vacc = vacc + _dot(hid, w2v_ref[g])
    vc_ref[0] = vacc.astype(BF16)


def _compress(r4, w1, pe, w2k, w2v_pad, B, NR):
    return pl.pallas_call(
        functools.partial(_compress_kernel, NR=NR),
        grid=(B,),
        in_specs=[pl.BlockSpec((2 * NSA_GROUPS, NR, CMP_STRIDE * HEAD_DIM), lambda b: (0, b, 0)),
                  pl.BlockSpec((2, CMP_BLOCK * HEAD_DIM, CMP_HIDDEN), lambda b: (0, 0, 0)),
                  pl.BlockSpec((2, 16, CMP_BLOCK * HEAD_DIM), lambda b: (0, 0, 0)),
                  pl.BlockSpec((CMP_HIDDEN, LANES), lambda b: (0, 0)),
                  pl.BlockSpec((NSA_GROUPS, CMP_HIDDEN, LANES), lambda b: (0, 0, 0))],
        out_specs=[pl.BlockSpec((1, NSA_GROUPS, NR, LANES), lambda b: (b, 0, 0, 0)),
                   pl.BlockSpec((1, NR, LANES), lambda b: (b, 0, 0))],
        out_shape=[jax.ShapeDtypeStruct((B, NSA_GROUPS, NR, LANES), BF16),
                   jax.ShapeDtypeStruct((B, NR, LANES), BF16)],
        compiler_params=pltpu.CompilerParams(dimension_semantics=("parallel",), vmem_limit_bytes=VMEM_LIMIT),
        name="compress",
    )(r4, w1, pe, w2k, w2v_pad)


def _cmp_kernel(q_ref, kc_ref, vc_ref, bt_ref, misc_ref, ocmp_ref, qaug_ref, bias_sc, val_sc, *, tq, NR):
    i = pl.program_id(0)
    b = pl.program_id(1)
    t0 = i * tq
    n_slc = LANES // 2
    gates = jax.nn.sigmoid(misc_ref[...])

    @pl.when(b == 0)
    def _():
        f = lax.broadcasted_iota(jnp.int32, (LANES, NR), 0)
        c = lax.broadcasted_iota(jnp.int32, (LANES, NR), 1)
        place = (((f % CMP_BAND) == (c - t0 // CMP_STRIDE + CMP_BAND // 2)) & (f < 3 * CMP_BAND))
        place = jnp.where(place, 1.0, 0.0).astype(BF16)
        for h in range(NSA_HEADS):
            bias_sc[h] = _dot(bt_ref[h], place)

    lane = lax.broadcasted_iota(jnp.int32, (1, LANES), 1)

    def gate_col(h):
        c0 = FOX_HEADS + 3 * h
        return gates[:, c0:c0 + 1]

    def tile(nc, nb):
        t = t0 + lax.broadcasted_iota(jnp.int32, (tq, 1), 0)
        c = lax.broadcasted_iota(jnp.int32, (1, nc), 1)
        cmask = (c * CMP_STRIDE + (CMP_BLOCK - 1)) <= t

        jj = lax.broadcasted_iota(jnp.int32, (LANES, nc), 0)
        cc = lax.broadcasted_iota(jnp.int32, (LANES, nc), 1)
        ov = ((cc * CMP_STRIDE < jj * SLC_BLOCK + SLC_BLOCK) & (cc * CMP_STRIDE + CMP_BLOCK > jj * SLC_BLOCK)
              & (jj < n_slc) & (cc < NR - 1))
        ov = jnp.where(ov, 1.0, 0.0).astype(BF16)

        jrow = lax.broadcasted_iota(jnp.int32, (nb, tq), 0)
        tt = t0 + lax.broadcasted_iota(jnp.int32, (nb, tq), 1)
        cur = tt // SLC_BLOCK
        forced = (jrow == 0) | (jrow == cur) | (jrow == cur - 1)
        valid = jrow * SLC_BLOCK <= tt

        outs = []
        for g in range(NSA_GROUPS):
            qs = q_ref[g * NSA_REP:(g + 1) * NSA_REP].reshape(NSA_REP * tq, LANES)
            s = (_dot_nt(qs, kc_ref[0, g, 0:nc, :]).reshape(NSA_REP, tq, nc)
                 + bias_sc[g * NSA_REP:(g + 1) * NSA_REP, :, 0:nc])
            s = jnp.where(cmask, s, NEG_INF)
            m = jnp.max(s, axis=-1, keepdims=True)
            e = jnp.where(cmask, jnp.exp2(s - m), 0.0)
            l = jnp.sum(e, axis=-1, keepdims=True)
            p = e * (1.0 / jnp.where(l > 0.0, l, 1.0))
            o = _dot(p.reshape(NSA_REP * tq, nc).astype(BF16), vc_ref[0, 0:nc, :])
            outs.append(o.reshape(NSA_REP, tq, LANES))

            if nb > SLC_TOPK:
                hi, mid, lo = _split3(jnp.sum(p, axis=0))
                imp = (_dot_nt(ov, hi) + _dot_nt(ov, mid) + _dot_nt(ov, lo))[0:nb, :]
                val_sc[0:nb, :] = jnp.where(forced, FORCED_SCORE, jnp.where(valid, imp, -1.0))
                vals = [val_sc[8 * a:8 * a + 8, :] for a in range(nb // 8)]
                ranks = [jnp.zeros((8, tq), F32) for _ in vals]
                j8 = lax.broadcasted_iota(jnp.int32, (8, 1), 0)
                for k in range(nb):
                    vk = val_sc[k:k + 1, :]
                    for a, va in enumerate(vals):
                        if 8 * a > k:
                            ahead = jnp.where(vk >= va, 1.0, 0.0)
                        elif 8 * a + 7 < k:
                            ahead = jnp.where(vk > va, 1.0, 0.0)
                        else:
                            tie = jnp.where(j8 > k - 8 * a, 1.0, 0.0)
                            ahead = jnp.where(vk > va, 1.0, jnp.where(vk == va, tie, 0.0))
                        ranks[a] = ranks[a] + ahead
                selneg = jnp.where(jnp.concatenate(ranks, axis=0) < float(SLC_TOPK), 0.0, NEG_INF)
            else:
                selneg = jnp.zeros((nb, tq), F32)
            pieces = [jnp.zeros((LANES - n_slc, tq), F32), selneg]
            if nb < n_slc:
                pieces.append(jnp.full((n_slc - nb, tq), NEG_INF, F32))
            selneg = jnp.concatenate(pieces, axis=0).T.astype(BF16)
            for r in range(NSA_REP):
                h = g * NSA_REP + r
                qaug_ref[h] = jnp.where(lane < HEAD_DIM, q_ref[h], selneg)

        for r in range(NSA_REP):
            ocmp_ref[:, r * LANES:(r + 1) * LANES] = jnp.where(
                lane < HEAD_DIM, outs[0][r] * gate_col(r), outs[1][r] * gate_col(NSA_REP + r))

    per = SLC_TOPK * SLC_BLOCK // tq
    variants = []
    for v in range(n_slc // SLC_TOPK):
        nb = SLC_TOPK * (v + 1)
        nc = min(NR, -(-(nb * SLC_BLOCK // CMP_STRIDE) // LANES) * LANES)
        variants.append(functools.partial(tile, nc, nb))
    lax.switch(jnp.minimum(i // per, len(variants) - 1), variants)


def _nsa_cmp(nq, kc, vc, bt, misc, B, T, NR):
    tq = NSA_TILE
    nT = T // tq
    M = B * T
    return pl.pallas_call(
        functools.partial(_cmp_kernel, tq=tq, NR=NR),
        grid=(nT, B),
        in_specs=[pl.BlockSpec((NSA_HEADS, tq, LANES), lambda i, b: (0, b * nT + i, 0)),
                  pl.BlockSpec((1, NSA_GROUPS, NR, LANES), lambda i, b: (b, 0, 0, 0)),
                  pl.BlockSpec((1, NR, LANES), lambda i, b: (b, 0, 0)),
                  pl.BlockSpec((NSA_HEADS, tq, LANES), lambda i, b: (0, 0, 0)),
                  pl.BlockSpec((tq, LANES), lambda i, b: (b * nT + i, 0))],
        out_specs=[pl.BlockSpec((tq, NSA_REP * LANES), lambda i, b: (b * nT + i, 0)),
                   pl.BlockSpec((NSA_HEADS, tq, LANES), lambda i, b: (0, b * nT + i, 0))],
        out_shape=[jax.ShapeDtypeStruct((M, NSA_REP * LANES), F32),
                   jax.ShapeDtypeStruct((NSA_HEADS, M, LANES), BF16)],
        scratch_shapes=[pltpu.VMEM((NSA_HEADS, tq, NR), F32),
                        pltpu.VMEM((LANES // 2, tq), F32)],
        compiler_params=pltpu.CompilerParams(dimension_semantics=("arbitrary", "arbitrary"),
                                             vmem_limit_bytes=VMEM_LIMIT),
        name="nsa_cmp",
    )(nq, kc, vc, bt, misc)


def _nsa_kernel(q_ref, k_ref, vt_ref, ocmp_ref, misc_ref, dist_ref, o_ref, s_sc, qt_sc, bias_ref, wmask_ref, *,
                tq, T, nb):
    i = pl.program_id(1)
    tk = tq
    nq = NSA_REP * tq
    jp = jnp.maximum(i - 1, 0)
    jf = jnp.maximum(i - 2, 0)

    @pl.when((pl.program_id(0) == 0) & (i == 0))
    def _():
        kk = lax.broadcasted_iota(jnp.int32, (tk, tq), 0)
        qq = lax.broadcasted_iota(jnp.int32, (tk, tq), 1)
        for h in range(NSA_HEADS):
            g, r = divmod(h, NSA_REP)
            rows = jnp.broadcast_to(dist_ref[h:h + 1, :], (tk, 2 * tq))
            pair = pltpu.roll(rows, 0, 1, stride=1, stride_axis=0)
            bias_ref[g, 0, :, r * tq:(r + 1) * tq] = jnp.where(kk <= qq, pair[:, 0:tq], NEG_INF)
            bias_ref[g, 1, :, r * tq:(r + 1) * tq] = pair[:, tq:2 * tq]
        for g in range(NSA_GROUPS):
            bias_ref[g, 2] = jnp.full((tk, nq), NEG_INF, F32)
            bias_ref[g, 3] = jnp.zeros((tk, nq), F32)
        for r in range(NSA_REP):
            wmask_ref[0, :, r * tq:(r + 1) * tq] = jnp.where(kk > qq, 0.0, NEG_INF)
        wmask_ref[1] = jnp.full((tk, nq), NEG_INF, F32)

    streams = [(bb, g) for bb in range(nb) for g in range(NSA_GROUPS)]
    ns = len(streams)

    for st, (bb, g) in enumerate(streams):
        qt_sc[st] = q_ref[g * NSA_REP:(g + 1) * NSA_REP, bb].reshape(nq, LANES).T

    def stage(st, branch, j, slot, bias):
        bb, g = streams[st]
        ks = pl.multiple_of(j * tk, tk)
        s = _dot(k_ref[branch * NSA_GROUPS + g, bb, pl.ds(ks, tk), :], qt_sc[st])
        if bias is not None:
            s = s + bias
        return _stage_scores(s_sc.at[slot], s)

    def update(st, branch, j, slot, smax, carry):
        bb, g = streams[st]
        ks = pl.multiple_of(bb * T + j * tk, tk)
        r0 = (branch * NSA_GROUPS + g) * HEAD_DIM
        return _softmax_step(s_sc.at[slot], smax, vt_ref[r0:r0 + HEAD_DIM, pl.ds(ks, tk)], carry)

    init = (jnp.full((1, nq), NEG_INF, F32), jnp.zeros((HEAD_DIM + SUM_ROWS, nq), F32))
    prev_tile = jnp.where(i >= 1, 1, 2)

    def far(j, carry):
        state, smax = list(carry[0]), carry[1]
        for st in range(ns):
            if st + 1 < ns:
                smax_next = stage(st + 1, 0, j, (st + 1) % 2, None)
            else:
                smax_next = stage(0, 0, j + 1, 0, bias_ref[0, jnp.where(j + 1 == jp, 1, 3)])
            state[st] = update(st, 0, j, st % 2, smax, state[st])
            smax = smax_next
        return tuple(state), smax

    smax = stage(0, 0, 0, 0, bias_ref[0, jnp.where(i >= 2, 3, prev_tile)])
    c_slc, smax = lax.fori_loop(0, jp, far, ((init,) * ns, smax))
    c_slc = list(c_slc)
    c_win = [init] * ns

    near = [(0, jp, lambda g: bias_ref[g, prev_tile]),
            (0, i, lambda g: bias_ref[g, 0]),
            (1, jf, lambda g: wmask_ref[jnp.where(i >= 2, 0, 1)]),
            (1, jp, lambda g: bias_ref[g, prev_tile]),
            (1, i, lambda g: bias_ref[g, 0])]
    steps = [(st, branch, j, bias) for (branch, j, bias) in near for st in range(ns)]
    for n, (st, branch, j, _) in enumerate(steps):
        if n + 1 < len(steps):
            st2, branch2, j2, bias2 = steps[n + 1]
            smax_next = stage(st2, branch2, j2, (n + 1) % 2, bias2(streams[st2][1]))
        state = c_slc if branch == 0 else c_win
        state[st] = update(st, branch, j, n % 2, smax, state[st])
        smax = smax_next

    for bb in range(nb):
        gates = jax.nn.sigmoid(misc_ref[bb].T)
        for r in range(NSA_REP):
            halves = []
            for g in range(NSA_GROUPS):
                c0 = FOX_HEADS + 3 * (g * NSA_REP + r)
                sl = slice(r * tq, (r + 1) * tq)
                (_, accs), (_, accw) = c_slc[bb * NSA_GROUPS + g], c_win[bb * NSA_GROUPS + g]
                halves.append(
                    accs[0:HEAD_DIM, sl] * (gates[c0 + 1:c0 + 2, :] / accs[HEAD_DIM:HEAD_DIM + 1, sl])
                    + accw[0:HEAD_DIM, sl] * (gates[c0 + 2:c0 + 3, :] / accw[HEAD_DIM:HEAD_DIM + 1, sl]))
            o = jnp.concatenate(halves, axis=0).T + ocmp_ref[bb, :, r * LANES:(r + 1) * LANES]
            o_ref[bb, :, r * LANES:(r + 1) * LANES] = o.astype(BF16)


def _nsa_attention(qaug, nk, nvt, ocmp, misc, dist_bias, B, T, nb):
    tq = NSA_TILE
    assert WINDOW == 2 * tq and B % nb == 0
    nT = T // tq
    nq = NSA_REP * tq
    out = pl.pallas_call(
        functools.partial(_nsa_kernel, tq=tq, T=T, nb=nb),
        grid=(B // nb, nT),
        in_specs=[pl.BlockSpec((NSA_HEADS, nb, tq, LANES), lambda b, i: (0, b, i, 0)),
                  pl.BlockSpec((2 * NSA_GROUPS, nb, T, LANES), lambda b, i: (0, b, 0, 0)),
                  pl.BlockSpec((2 * KV_W, nb * T), lambda b, i: (0, b)),
                  pl.BlockSpec((nb, tq, NSA_REP * LANES), lambda b, i: (b, i, 0)),
                  pl.BlockSpec((nb, tq, LANES), lambda b, i: (b, i, 0)),
                  pl.BlockSpec((NSA_HEADS, 2 * tq), lambda b, i: (0, 0))],
        out_specs=pl.BlockSpec((nb, tq, NSA_REP * LANES), lambda b, i: (b, i, 0)),
        out_shape=jax.ShapeDtypeStruct((B, T, NSA_W), BF16),
        scratch_shapes=[pltpu.VMEM((2, tq, nq), F32),
                        pltpu.VMEM((nb * NSA_GROUPS, LANES, nq), BF16),
                        pltpu.VMEM((NSA_GROUPS, 4, tq, nq), F32),
                        pltpu.VMEM((2, tq, nq), F32)],
        compiler_params=pltpu.CompilerParams(dimension_semantics=("arbitrary", "arbitrary"),
                                             vmem_limit_bytes=VMEM_LIMIT),
        name="nsa_attn",
    )(qaug.reshape(NSA_HEADS, B, T, LANES), nk.reshape(2 * NSA_GROUPS, B, T, LANES), nvt,
      ocmp.reshape(B, T, NSA_REP * LANES), misc.reshape(B, T, LANES), dist_bias)
    return out.reshape(B * T, NSA_W)


def _merge_kernel(x_ref, g_ref, of_ref, on_ref, wga_ref, wgb_ref, wbf_ref, wbn_nat_ref, wo_ref, x1_ref, wbn_ref, *,
                  sub):
    @pl.when(pl.program_id(0) == 0)
    def _():
        for g in range(NSA_GROUPS):
            for r in range(NSA_REP):
                src = (g * NSA_REP + r) * HEAD_DIM
                dst = (r * NSA_GROUPS + g) * HEAD_DIM
                wbn_ref[dst:dst + HEAD_DIM, :] = wbn_nat_ref[src:src + HEAD_DIM, :]

    for r0 in range(0, x_ref.shape[0], sub):
        rows = slice(r0, r0 + sub)
        x = x_ref[rows, :]
        h = _rms(x, g_ref[...]).astype(BF16)
        ga = jax.nn.sigmoid(_dot(h, wga_ref[...]))
        gb = jax.nn.sigmoid(_dot(h, wgb_ref[...]))
        merged = ga * _dot(of_ref[rows, :], wbf_ref[...]) + gb * _dot(on_ref[rows, :], wbn_ref[...])
        x1_ref[rows, :] = x + _dot(merged.astype(BF16), wo_ref[...])


def _merge(x2, g, o_fox, o_nsa, wga, wgb, wbf, wbn, wo, tm):
    M = x2.shape[0]
    row = lambda i: (i, 0)
    full = lambda i: (0, 0)
    return pl.pallas_call(
        functools.partial(_merge_kernel, sub=tm // 2),
        grid=(M // tm,),
        in_specs=[pl.BlockSpec((tm, D_MODEL), row),
                  pl.BlockSpec((1, D_MODEL), full),
                  pl.BlockSpec((tm, FOX_W), row),
                  pl.BlockSpec((tm, NSA_W), row),
                  pl.BlockSpec((D_MODEL, D_MODEL), full),
                  pl.BlockSpec((D_MODEL, D_MODEL), full),
                  pl.BlockSpec((FOX_W, D_MODEL), full),
                  pl.BlockSpec((NSA_W, D_MODEL), full),
                  pl.BlockSpec((D_MODEL, D_MODEL), full)],
        out_specs=pl.BlockSpec((tm, D_MODEL), row),
        out_shape=jax.ShapeDtypeStruct((M, D_MODEL), F32),
        scratch_shapes=[pltpu.VMEM((NSA_W, D_MODEL), BF16)],
        compiler_params=pltpu.CompilerParams(dimension_semantics=("arbitrary",), vmem_limit_bytes=VMEM_LIMIT),
        name="merge",
    )(x2, g, o_fox, o_nsa, wga, wgb, wbf, wbn, wo)


def _mlp_kernel(x_ref, g_ref, w1_ref, w2_ref, gf_ref, o_ref, *, final, sub):
    for r0 in range(0, x_ref.shape[0], sub):
        x = x_ref[r0:r0 + sub, :]
        h = _rms(x, g_ref[...]).astype(BF16)
        u = jnp.maximum(_dot(h, w1_ref[...]), 0.0)
        y = x + _dot((u * u).astype(BF16), w2_ref[...])
        o_ref[r0:r0 + sub, :] = _rms(y, gf_ref[...]) if final else y


def _mlp(x1, g, w1, w2, gf, tm, sub, final):
    M = x1.shape[0]
    row = lambda i: (i, 0)
    full = lambda i: (0, 0)
    once = pl.Buffered(1)
    return pl.pallas_call(
        functools.partial(_mlp_kernel, final=final, sub=sub),
        grid=(M // tm,),
        in_specs=[pl.BlockSpec((tm, D_MODEL), row),
                  pl.BlockSpec((1, D_MODEL), full),
                  pl.BlockSpec((D_MODEL, D_FF), full, pipeline_mode=once),
                  pl.BlockSpec((D_FF, D_MODEL), full, pipeline_mode=once),
                  pl.BlockSpec((1, D_MODEL), full)],
        out_specs=pl.BlockSpec((tm, D_MODEL), row),
        out_shape=jax.ShapeDtypeStruct((M, D_MODEL), F32),
        compiler_params=pltpu.CompilerParams(dimension_semantics=("parallel",), vmem_limit_bytes=VMEM_LIMIT),
        name="mlp",
    )(x1, g, w1, w2, gf)


def _bias_tables(rel_tbl, tq):
    L = 2 * tq
    g = (rel_tbl - rel_tbl[REL_BUCKETS - 1]).T[:, _rel_bucket_np(np.arange(L))] * LOG2E
    base = CMP_STRIDE * (CMP_BAND // 2) - (CMP_BLOCK - 1)
    left = CMP_STRIDE * (CMP_BAND - 1) - base
    gp = jnp.concatenate([jnp.zeros((NSA_HEADS, left), F32), g], axis=1)
    n = left + L
    shifted = jnp.tile(gp, (1, CMP_BAND))[:, :CMP_BAND * (n - CMP_STRIDE)].reshape(NSA_HEADS, CMP_BAND, n - CMP_STRIDE)
    band = jnp.swapaxes(shifted[:, :, left + base:left + base + tq], 1, 2)
    return g, band


def _layer(x2, B, T, rel_tbl, g_attn, w_in, b_forget, pe_k, w1_k, w2_k, pe_v, w1_v, w2_v,
           w_br_fox, w_br_nsa, w_out, g_mlp, w_ff1, w_ff2, g_final, final):
    M = B * T
    NR = T // CMP_STRIDE
    fq, fk, nq, nk, fv, nv, r4, misc, w_ga, w_gb = _in_proj(x2, g_attn.reshape(1, D_MODEL), w_in, tm=512, T=T)

    bf_pad = jnp.zeros((1, LANES), F32).at[0, :FOX_HEADS].set(b_forget)
    c3 = _fox_decay(misc, bf_pad, B, T)
    o_fox = _fox_attention(fq, fk, fv, c3, B, T, tq=512, tk=512, nh=8)

    w1 = jnp.stack([w1_k, w1_v]).astype(BF16)
    pe = jnp.stack([pe_k.reshape(1, -1), pe_v.reshape(1, -1)]).astype(BF16)
    pe = jnp.broadcast_to(pe, (2, 16, CMP_BLOCK * HEAD_DIM))
    zpad = jnp.zeros((CMP_HIDDEN, HEAD_DIM), F32)
    w2v_pad = jnp.stack([jnp.concatenate([w2_v, zpad], axis=1), jnp.concatenate([zpad, w2_v], axis=1)]).astype(BF16)
    w2k_pad = jnp.concatenate([w2_k, zpad], axis=1).astype(BF16)
    kc, vc = _compress(r4, w1, pe, w2k_pad, w2v_pad, B, NR)

    dist_bias, band = _bias_tables(rel_tbl, NSA_TILE)
    hi, mid, lo = _split3(band)
    bt = jnp.concatenate([hi, mid, lo, jnp.zeros_like(hi)], axis=-1)

    ocmp, qaug = _nsa_cmp(nq, kc, vc, bt, misc, B, T, NR)
    o_nsa = _nsa_attention(qaug, nk, nv, ocmp, misc, dist_bias, B, T, nb=2 if B % 2 == 0 else 1)

    x1 = _merge(x2, g_attn.reshape(1, D_MODEL), o_fox, o_nsa, w_ga, w_gb,
                w_br_fox.astype(BF16), w_br_nsa.astype(BF16), w_out.astype(BF16), tm=512)
    return _mlp(x1, g_mlp.reshape(1, D_MODEL), w_ff1.astype(BF16), w_ff2.astype(BF16),
                g_final.reshape(1, D_MODEL), tm=512, sub=256, final=final)


def kernel(x, rel_bias_table, g_attn, w_in, b_forget, cmp_pe_k, cmp_w1_k, cmp_w2_k, cmp_pe_v, cmp_w1_v, cmp_w2_v,
           w_br_fox, w_br_nsa, w_out, g_mlp, w_ff1, w_ff2, g_final):
    B, T, _ = x.shape
    depth = g_attn.shape[0]
    x2 = x.reshape(B * T, D_MODEL)
    for l in range(depth):
        x2 = _layer(x2, B, T, rel_bias_table, g_attn[l], w_in[l], b_forget[l], cmp_pe_k[l], cmp_w1_k[l],
                    cmp_w2_k[l], cmp_pe_v[l], cmp_w1_v[l], cmp_w2_v[l], w_br_fox[l], w_br_nsa[l], w_out[l],
                    g_mlp[l], w_ff1[l], w_ff2[l], g_final, final=(l == depth - 1))
    return x2.reshape(B, T, D_MODEL)
```

```python
import functools
import math

import numpy as np
import jax
import jax.numpy as jnp
from jax import lax
from jax.experimental import pallas as pl
from jax.experimental.pallas import tpu as pltpu

F32 = jnp.float32
BF16 = jnp.bfloat16

D_MODEL = 1024
HEAD_DIM = 64
FOX_HEADS = 8
NSA_HEADS = 8
NSA_GROUPS = 2
NSA_REP = NSA_HEADS // NSA_GROUPS
CMP_BLOCK = 32
CMP_STRIDE = 16
CMP_HIDDEN = 256
SLC_BLOCK = 64
SLC_TOPK = 16
WINDOW = 512
REL_BUCKETS = 32
REL_MAX_DIST = 128
D_FF = 4 * D_MODEL
RMS_EPS = 1e-6
NEG_INF = -1e30
FORCED_SCORE = 1e4
SCALE = HEAD_DIM ** -0.5
LOG2E = math.log2(math.e)

FOX_W = FOX_HEADS * HEAD_DIM
NSA_W = NSA_HEADS * HEAD_DIM
KV_W = NSA_GROUPS * HEAD_DIM
LANES = 128
NSA_TILE = 256
REL_FAR = 113
CMP_BAND = 32

VMEM_LIMIT = 56 * 1024 * 1024

_NT = (((1,), (1,)), ((), ()))


def _dot(a, b):
    return jnp.dot(a, b, preferred_element_type=F32)


def _dot_nt(a, b):
    return lax.dot_general(a, b, _NT, preferred_element_type=F32)


def _split3(x):
    hi = x.astype(BF16)
    r1 = x - hi.astype(F32)
    mid = r1.astype(BF16)
    lo = (r1 - mid.astype(F32)).astype(BF16)
    return hi, mid, lo


def _rms(x, g):
    return x * lax.rsqrt(jnp.mean(x * x, axis=-1, keepdims=True) + RMS_EPS) * g


def _rel_bucket_np(n):
    exact = REL_BUCKETS // 2
    nf = np.maximum(n, exact).astype(np.float64)
    log_b = exact + (np.log(nf / exact) / math.log(REL_MAX_DIST / exact) * (REL_BUCKETS - exact)).astype(np.int64)
    return np.where(n < exact, n, np.minimum(log_b, REL_BUCKETS - 1))


_SEG_FQ, _SEG_FK, _SEG_NQ, _SEG_NK, _SEG_FV, _SEG_NV, _SEG_CKV, _SEG_MISC = (
    0, 512, 1024, 1536, 1792, 2304, 2560, 2816)
_W_ALL = 2944


def _inproj_kernel(x_ref, g_ref, wt_ref, fq_ref, fk_ref, nq_ref, nk_ref, fv_ref, nv_ref, ckv_ref, misc_ref,
                   wga_ref, wgb_ref, ck_sc, cv_sc, w_ref, *, tm, T, segments):
    @pl.when(pl.program_id(0) == 0)
    def _():
        narrow = []
        for dst, src, width, scale in segments:
            if width % LANES:
                narrow.append(wt_ref[src:src + width, :])
                continue
            for c0 in range(0, width, LANES):
                slab = wt_ref[src + c0:src + c0 + LANES, :]
                w_ref[:, dst + c0:dst + c0 + LANES] = (slab if scale == 1.0 else slab * scale).T.astype(BF16)
        pad = LANES - sum(s.shape[0] for s in narrow)
        slab = jnp.concatenate(narrow + [jnp.zeros((pad, D_MODEL), F32)], axis=0)
        w_ref[:, _SEG_MISC:_SEG_MISC + LANES] = slab.T.astype(BF16)
        gates0 = wt_ref.shape[0] - 2 * D_MODEL
        for n, ref in enumerate((wga_ref, wgb_ref)):
            for c0 in range(0, D_MODEL, LANES):
                src = gates0 + n * D_MODEL + c0
                ref[:, c0:c0 + LANES] = wt_ref[src:src + LANES, :].T.astype(BF16)

    h = _rms(x_ref[...], g_ref[...]).astype(BF16)

    def seg(a, n):
        return _dot(h, w_ref[:, a:a + n])

    lane = lax.broadcasted_iota(jnp.int32, (1, LANES), 1)

    def heads_wide(ref, a, n_heads, mult, upper):
        r = seg(a, n_heads * HEAD_DIM)
        for j in range(n_heads):
            pair = r[:, (j // 2) * LANES:(j // 2 + 1) * LANES]
            if j % 2:
                pair = pltpu.roll(pair, HEAD_DIM, 1)
            ref[j] = jnp.where(lane < HEAD_DIM, pair * mult, upper(j)).astype(BF16)

    q_upper = jnp.where((lane == DECAY_LANES[0]) | (lane == DECAY_LANES[1]) | (lane == DECAY_LANES[2]), -1.0, 0.0)
    heads_wide(fq_ref, _SEG_FQ, FOX_HEADS, LOG2E, lambda j: q_upper)
    heads_wide(fk_ref, _SEG_FK, FOX_HEADS, 1.0, lambda j: 0.0)
    heads_wide(nq_ref, _SEG_NQ, NSA_HEADS, LOG2E, lambda j: 0.0)
    t = (pl.program_id(0) * tm + lax.broadcasted_iota(jnp.int32, (tm, 1), 0)) % T
    block_onehot = jnp.where(lane - HEAD_DIM == t // SLC_BLOCK, 1.0, 0.0)
    heads_wide(nk_ref, _SEG_NK, 2 * NSA_GROUPS, 1.0, lambda j: block_onehot if j < NSA_GROUPS else 0.0)
    fv_ref[...] = seg(_SEG_FV, FOX_W).astype(BF16).T
    nv_ref[...] = seg(_SEG_NV, 2 * KV_W).astype(BF16).T
    ckv = seg(_SEG_CKV, 2 * KV_W)
    for kind, sc in enumerate((ck_sc, cv_sc)):
        sc[...] = ckv[:, kind * LANES:(kind + 1) * LANES]
        for l in range(CMP_STRIDE):
            rows_l = sc[pl.ds(l, tm // CMP_STRIDE, stride=CMP_STRIDE), :]
            for g in range(NSA_GROUPS):
                ckv_ref[kind * NSA_GROUPS + g, :, l * HEAD_DIM:(l + 1) * HEAD_DIM] = (
                    rows_l[:, g * HEAD_DIM:(g + 1) * HEAD_DIM].astype(BF16))
    misc_ref[...] = seg(_SEG_MISC, LANES)


def _in_proj(x2, g, w_in, tm, T):
    M = x2.shape[0]
    assert T % tm == 0
    row = lambda i: (i, 0)
    hrow = lambda i: (0, i, 0)
    widths = (FOX_W, FOX_W, FOX_W, FOX_HEADS, NSA_W, KV_W, KV_W, KV_W, KV_W, KV_W, KV_W, 3 * NSA_HEADS)
    (fq, fk, fv, fl, nq, kc, vc, ksl, vsl, kwn, vwn, ng) = np.concatenate([[0], np.cumsum(widths)[:-1]]).tolist()
    order = [(fq, FOX_W, SCALE), (fk, FOX_W, 1.0), (nq, NSA_W, SCALE), (ksl, KV_W, 1.0), (kwn, KV_W, 1.0),
             (fv, FOX_W, 1.0), (vsl, KV_W, 1.0), (vwn, KV_W, 1.0), (kc, KV_W, 1.0), (vc, KV_W, 1.0),
             (fl, FOX_HEADS, 1.0), (ng, 3 * NSA_HEADS, 1.0)]
    segments, dst = [], 0
    for src, width, scale in order:
        segments.append((dst, src, width, scale))
        dst += width
    assert segments[2][0] == _SEG_NQ and segments[5][0] == _SEG_FV and segments[10][0] == _SEG_MISC
    return pl.pallas_call(
        functools.partial(_inproj_kernel, tm=tm, T=T, segments=tuple(segments)),
        grid=(M // tm,),
        in_specs=[pl.BlockSpec((tm, D_MODEL), row),
                  pl.BlockSpec((1, D_MODEL), lambda i: (0, 0)),
                  pl.BlockSpec((w_in.shape[1], D_MODEL), lambda i: (0, 0), pipeline_mode=pl.Buffered(1))],
        out_specs=[pl.BlockSpec((FOX_HEADS, tm, LANES), hrow),
                   pl.BlockSpec((FOX_HEADS, tm, LANES), hrow),
                   pl.BlockSpec((NSA_HEADS, tm, LANES), hrow),
                   pl.BlockSpec((2 * NSA_GROUPS, tm, LANES), hrow),
                   pl.BlockSpec((FOX_W, tm), lambda i: (0, i)),
                   pl.BlockSpec((2 * KV_W, tm), lambda i: (0, i)),
                   pl.BlockSpec((2 * NSA_GROUPS, tm // CMP_STRIDE, CMP_STRIDE * HEAD_DIM), hrow),
                   pl.BlockSpec((tm, LANES), row),
                   pl.BlockSpec((D_MODEL, D_MODEL), lambda i: (0, 0)),
                   pl.BlockSpec((D_MODEL, D_MODEL), lambda i: (0, 0))],
        out_shape=[jax.ShapeDtypeStruct((FOX_HEADS, M, LANES), BF16),
                   jax.ShapeDtypeStruct((FOX_HEADS, M, LANES), BF16),
                   jax.ShapeDtypeStruct((NSA_HEADS, M, LANES), BF16),
                   jax.ShapeDtypeStruct((2 * NSA_GROUPS, M, LANES), BF16),
                   jax.ShapeDtypeStruct((FOX_W, M), BF16),
                   jax.ShapeDtypeStruct((2 * KV_W, M), BF16),
                   jax.ShapeDtypeStruct((2 * NSA_GROUPS, M // CMP_STRIDE, CMP_STRIDE * HEAD_DIM), BF16),
                   jax.ShapeDtypeStruct((M, LANES), F32),
                   jax.ShapeDtypeStruct((D_MODEL, D_MODEL), BF16),
                   jax.ShapeDtypeStruct((D_MODEL, D_MODEL), BF16)],
        scratch_shapes=[pltpu.VMEM((tm, LANES), F32), pltpu.VMEM((tm, LANES), F32),
                        pltpu.VMEM((D_MODEL, _W_ALL), BF16)],
        compiler_params=pltpu.CompilerParams(dimension_semantics=("arbitrary",), vmem_limit_bytes=VMEM_LIMIT),
        name="in_proj",
    )(x2, g, w_in.T)


_SCAN_BLK = 256


DECAY_LANES = (HEAD_DIM, HEAD_DIM + FOX_HEADS, HEAD_DIM + 2 * FOX_HEADS)


def _decay_kernel(misc_ref, bf_ref, c3_ref, *, T):
    r = lax.broadcasted_iota(jnp.int32, (_SCAN_BLK, _SCAN_BLK), 0)
    c = lax.broadcasted_iota(jnp.int32, (_SCAN_BLK, _SCAN_BLK), 1)
    triu = jnp.where(r <= c, 1.0, 0.0).astype(BF16)
    x = misc_ref[...].T[0:FOX_HEADS, :] + bf_ref[...]
    lf = jnp.minimum(x, 0.0) - jnp.log1p(jnp.exp(-jnp.abs(x)))
    carry = jnp.zeros((FOX_HEADS, 1), F32)
    pieces = []
    for n in range(T // _SCAN_BLK):
        hi, mid, lo = _split3(lf[:, n * _SCAN_BLK:(n + 1) * _SCAN_BLK])
        cs = _dot(hi, triu) + _dot(mid, triu) + _dot(lo, triu) + carry
        carry = cs[:, _SCAN_BLK - 1:_SCAN_BLK]
        pieces.append(cs)
    hi, mid, lo = (t.astype(F32) for t in _split3(jnp.concatenate(pieces, axis=1) * LOG2E))
    rows = jnp.concatenate([hi, mid, lo, jnp.zeros((LANES - 3 * FOX_HEADS, T), F32)], axis=0)
    c3_ref[...] = rows.T


def _fox_decay(misc, bf_col, B, T):
    return pl.pallas_call(
        functools.partial(_decay_kernel, T=T),
        grid=(B,),
        in_specs=[pl.BlockSpec((T, LANES), lambda b: (b, 0)),
                  pl.BlockSpec((FOX_HEADS, 1), lambda b: (0, 0))],
        out_specs=pl.BlockSpec((T, LANES), lambda b: (b, 0)),
        out_shape=jax.ShapeDtypeStruct((B * T, LANES), F32),
        compiler_params=pltpu.CompilerParams(dimension_semantics=("parallel",), vmem_limit_bytes=VMEM_LIMIT),
        name="fox_decay",
    )(misc, bf_col)


_FOX_PREP_BLK = 512


SUM_ROWS = 16


def _with_sum_row(vt):
    r = lax.broadcasted_iota(jnp.int32, (SUM_ROWS, vt.shape[1]), 0)
    return jnp.concatenate([vt, jnp.where(r == 0, 1.0, 0.0).astype(vt.dtype)], axis=0)


def _stage_scores(slot_ref, s):
    slot_ref[...] = s
    return jnp.max(s, axis=0, keepdims=True)


def _softmax_step(slot_ref, smax, vt, carry):
    m, acc = carry
    m_new = jnp.maximum(m, smax)
    p = jnp.exp2(slot_ref[...] - m_new).astype(BF16)
    return m_new, jnp.exp2(m - m_new) * acc + _dot(_with_sum_row(vt), p)


def _fox_kernel(q_ref, k_ref, vt_ref, c3_ref, mask_ref, o_ref, kaug_sc, s_sc, qt_sc, *, tq, tk, T, nh):
    hg = pl.program_id(1)
    i = pl.program_id(2)
    lane = lax.broadcasted_iota(jnp.int32, (1, LANES), 1)
    decay_lane = (lane == DECAY_LANES[0]) | (lane == DECAY_LANES[1]) | (lane == DECAY_LANES[2])

    @pl.when(i == 0)
    def _():
        for hh in range(nh):
            def prep(n, _, hh=hh):
                r0 = pl.multiple_of(n * _FOX_PREP_BLK, _FOX_PREP_BLK)
                c3 = pltpu.roll(c3_ref[pl.ds(r0, _FOX_PREP_BLK), :], HEAD_DIM - (nh * hg + hh), 1)
                kaug_sc[hh, pl.ds(r0, _FOX_PREP_BLK), :] = jnp.where(
                    lane < HEAD_DIM, k_ref[hh, pl.ds(r0, _FOX_PREP_BLK), :],
                    jnp.where(decay_lane, c3, 0.0).astype(BF16))
                return 0

            lax.fori_loop(0, T // _FOX_PREP_BLK, prep, 0)

    for hh in range(nh):
        qt_sc[hh] = q_ref[hh].T

    n_diag = tq // tk
    n_far = i * n_diag
    no_mask = n_diag

    def stage(hh, j, slot, mask_idx):
        ks = pl.multiple_of(j * tk, tk)
        s = _dot(kaug_sc[hh, pl.ds(ks, tk), :], qt_sc[hh])
        if mask_idx is not None:
            s = s + mask_ref[mask_idx]
        return _stage_scores(s_sc.at[slot], s)

    def update(hh, j, slot, smax, carry):
        ks = pl.multiple_of(j * tk, tk)
        return _softmax_step(s_sc.at[slot], smax, vt_ref[hh * HEAD_DIM:(hh + 1) * HEAD_DIM, pl.ds(ks, tk)], carry)

    def far(j, carry):
        state, smax = list(carry[0]), carry[1]
        for hh in range(nh):
            if hh + 1 < nh:
                smax_next = stage(hh + 1, j, (hh + 1) % 2, None)
            else:
                smax_next = stage(0, j + 1, 0, jnp.where(j + 1 == n_far, 0, no_mask))
            state[hh] = update(hh, j, hh % 2, smax, state[hh])
            smax = smax_next
        return tuple(state), smax

    init = (jnp.full((1, tq), NEG_INF, F32), jnp.zeros((HEAD_DIM + SUM_ROWS, tq), F32))
    smax = stage(0, 0, 0, jnp.where(n_far == 0, 0, no_mask))
    state, smax = lax.fori_loop(0, n_far, far, ((init,) * nh, smax))
    state = list(state)
    steps = [(hh, d) for d in range(n_diag) for hh in range(nh)]
    for n, (hh, d) in enumerate(steps):
        if n + 1 < len(steps):
            hh2, d2 = steps[n + 1]
            smax_next = stage(hh2, n_far + d2, (n + 1) % 2, d2)
        state[hh] = update(hh, n_far + d, n % 2, smax, state[hh])
        smax = smax_next
    outs = [acc[0:HEAD_DIM] * (1.0 / acc[HEAD_DIM:HEAD_DIM + 1]) for _, acc in state]
    o_ref[...] = jnp.concatenate(outs, axis=0).T.astype(BF16)


def _fox_attention(fq, fk, fvt, c3, B, T, tq, tk, nh):
    assert nh % 2 == 0 and FOX_HEADS % nh == 0
    nT = T // tq
    n_diag = tq // tk
    kk = np.arange(tk)[None, :, None] + tk * np.arange(n_diag)[:, None, None]
    mask = np.where(kk <= np.arange(tq)[None, None, :], 0.0, NEG_INF).astype(np.float32)
    mask = jnp.asarray(np.concatenate([mask, np.zeros((1, tk, tq), np.float32)]))
    return pl.pallas_call(
        functools.partial(_fox_kernel, tq=tq, tk=tk, T=T, nh=nh),
        grid=(B, FOX_HEADS // nh, nT),
        in_specs=[pl.BlockSpec((nh, tq, LANES), lambda b, hp, i: (hp, b * nT + i, 0)),
                  pl.BlockSpec((nh, T, LANES), lambda b, hp, i: (hp, b, 0)),
                  pl.BlockSpec((nh * HEAD_DIM, T), lambda b, hp, i: (hp, b)),
                  pl.BlockSpec((T, LANES), lambda b, hp, i: (b, 0)),
                  pl.BlockSpec((n_diag + 1, tk, tq), lambda b, hp, i: (0, 0, 0))],
        out_specs=pl.BlockSpec((tq, nh * HEAD_DIM), lambda b, hp, i: (b * nT + i, hp)),
        out_shape=jax.ShapeDtypeStruct((B * T, FOX_W), BF16),
        scratch_shapes=[pltpu.VMEM((nh, T, LANES), BF16),
                        pltpu.VMEM((2, tk, tq), F32),
                        pltpu.VMEM((nh, LANES, tq), BF16)],
        compiler_params=pltpu.CompilerParams(dimension_semantics=("parallel", "parallel", "arbitrary"),
                                             vmem_limit_bytes=VMEM_LIMIT),
        name="fox_attn",
    )(fq, fk, fvt, c3, mask)


def _compress_kernel(r_ref, w1_ref, pe_ref, w2k_ref, w2v_ref, kc_ref, vc_ref, *, NR):
    half = CMP_STRIDE * HEAD_DIM
    vacc = jnp.zeros((NR, LANES), F32)
    for idx in range(2 * NSA_GROUPS):
        kind, g = divmod(idx, NSA_GROUPS)
        rm = r_ref[idx]
        a = _dot(rm, w1_ref[kind, 0:half, :])
        bm = _dot(rm, w1_ref[kind, half:2 * half, :])
        pe_term = _dot(pe_ref[kind], w1_ref[kind])[0:1, :]
        pre = a + pltpu.roll(bm, NR - 1, 0) + pe_term
        hid = (pre * jax.nn.sigmoid(pre)).astype(BF16)
        if kind == 0:
            kc_ref[0, g] = _dot(hid, w2k_ref[...]).astype(BF16)
        else:
            vacc = vacc + _dot(hid, w2v_ref[g])
    vc_ref[0] = vacc.astype(BF16)


def _compress(r4, w1, pe, w2k, w2v_pad, B, NR):
    return pl.pallas_call(
        functools.partial(_compress_kernel, NR=NR),
        grid=(B,),
        in_specs=[pl.BlockSpec((2 * NSA_GROUPS, NR, CMP_STRIDE * HEAD_DIM), lambda b: (0, b, 0)),
                  pl.BlockSpec((2, CMP_BLOCK * HEAD_DIM, CMP_HIDDEN), lambda b: (0, 0, 0)),
                  pl.BlockSpec((2, 16, CMP_BLOCK * HEAD_DIM), lambda b: (0, 0, 0)),
                  pl.BlockSpec((CMP_HIDDEN, LANES), lambda b: (0, 0)),
                  pl.BlockSpec((NSA_GROUPS, CMP_HIDDEN, LANES), lambda b: (0, 0, 0))],
        out_specs=[pl.BlockSpec((1, NSA_GROUPS, NR, LANES), lambda b: (b, 0, 0, 0)),
                   pl.BlockSpec((1, NR, LANES), lambda b: (b, 0, 0))],
        out_shape=[jax.ShapeDtypeStruct((B, NSA_GROUPS, NR, LANES), BF16),
                   jax.ShapeDtypeStruct((B, NR, LANES), BF16)],
        compiler_params=pltpu.CompilerParams(dimension_semantics=("parallel",), vmem_limit_bytes=VMEM_LIMIT),
        name="compress",
    )(r4, w1, pe, w2k, w2v_pad)


def _cmp_kernel(q_ref, kc_ref, vc_ref, bt_ref, misc_ref, ocmp_ref, qaug_ref, bias_sc, val_sc, *, tq, NR):
    i = pl.program_id(0)
    b = pl.program_id(1)
    t0 = i * tq
    n_slc = LANES // 2
    gates = jax.nn.sigmoid(misc_ref[...])

    @pl.when(b == 0)
    def _():
        f = lax.broadcasted_iota(jnp.int32, (LANES, NR), 0)
        c = lax.broadcasted_iota(jnp.int32, (LANES, NR), 1)
        place = (((f % CMP_BAND) == (c - t0 // CMP_STRIDE + CMP_BAND // 2)) & (f < 3 * CMP_BAND))
        place = jnp.where(place, 1.0, 0.0).astype(BF16)
        for h in range(NSA_HEADS):
            bias_sc[h] = _dot(bt_ref[h], place)

    lane = lax.broadcasted_iota(jnp.int32, (1, LANES), 1)

    def gate_col(h):
        c0 = FOX_HEADS + 3 * h
        return gates[:, c0:c0 + 1]

    def tile(nc, nb):
        t = t0 + lax.broadcasted_iota(jnp.int32, (tq, 1), 0)
        c = lax.broadcasted_iota(jnp.int32, (1, nc), 1)
        cmask = (c * CMP_STRIDE + (CMP_BLOCK - 1)) <= t

        jj = lax.broadcasted_iota(jnp.int32, (LANES, nc), 0)
        cc = lax.broadcasted_iota(jnp.int32, (LANES, nc), 1)
        ov = ((cc * CMP_STRIDE < jj * SLC_BLOCK + SLC_BLOCK) & (cc * CMP_STRIDE + CMP_BLOCK > jj * SLC_BLOCK)
              & (jj < n_slc) & (cc < NR - 1))
        ov = jnp.where(ov, 1.0, 0.0).astype(BF16)

        jrow = lax.broadcasted_iota(jnp.int32, (nb, tq), 0)
        tt = t0 + lax.broadcasted_iota(jnp.int32, (nb, tq), 1)
        cur = tt // SLC_BLOCK
        forced = (jrow == 0) | (jrow == cur) | (jrow == cur - 1)
        valid = jrow * SLC_BLOCK <= tt

        outs = []
        for g in range(NSA_GROUPS):
            qs = q_ref[g * NSA_REP:(g + 1) * NSA_REP].reshape(NSA_REP * tq, LANES)
            s = (_dot_nt(qs, kc_ref[0, g, 0:nc, :]).reshape(NSA_REP, tq, nc)
                 + bias_sc[g * NSA_REP:(g + 1) * NSA_REP, :, 0:nc])
            s = jnp.where(cmask, s, NEG_INF)
            m = jnp.max(s, axis=-1, keepdims=True)
            e = jnp.where(cmask, jnp.exp2(s - m), 0.0)
            l = jnp.sum(e, axis=-1, keepdims=True)
            p = e * (1.0 / jnp.where(l > 0.0, l, 1.0))
            o = _dot(p.reshape(NSA_REP * tq, nc).astype(BF16), vc_ref[0, 0:nc, :])
            outs.append(o.reshape(NSA_REP, tq, LANES))

            if nb > SLC_TOPK:
                hi, mid, lo = _split3(jnp.sum(p, axis=0))
                imp = (_dot_nt(ov, hi) + _dot_nt(ov, mid) + _dot_nt(ov, lo))[0:nb, :]
                val_sc[0:nb, :] = jnp.where(forced, FORCED_SCORE, jnp.where(valid, imp, -1.0))
                vals = [val_sc[8 * a:8 * a + 8, :] for a in range(nb // 8)]
                ranks = [jnp.zeros((8, tq), F32) for _ in vals]
                j8 = lax.broadcasted_iota(jnp.int32, (8, 1), 0)
                for k in range(nb):
                    vk = val_sc[k:k + 1, :]
                    for a, va in enumerate(vals):
                        if 8 * a > k:
                            ahead = jnp.where(vk >= va, 1.0, 0.0)
                        elif 8 * a + 7 < k:
                            ahead = jnp.where(vk > va, 1.0, 0.0)
                        else:
                            tie = jnp.where(j8 > k - 8 * a, 1.0, 0.0)
                            ahead = jnp.where(vk > va, 1.0, jnp.where(vk == va, tie, 0.0))
                        ranks[a] = ranks[a] + ahead
                selneg = jnp.where(jnp.concatenate(ranks, axis=0) < float(SLC_TOPK), 0.0, NEG_INF)
            else:
                selneg = jnp.zeros((nb, tq), F32)
            pieces = [jnp.zeros((LANES - n_slc, tq), F32), selneg]
            if nb < n_slc:
                pieces.append(jnp.full((n_slc - nb, tq), NEG_INF, F32))
            selneg = jnp.concatenate(pieces, axis=0).T.astype(BF16)
            for r in range(NSA_REP):
                h = g * NSA_REP + r
                qaug_ref[h] = jnp.where(lane < HEAD_DIM, q_ref[h], selneg)

        for r in range(NSA_REP):
            ocmp_ref[:, r * LANES:(r + 1) * LANES] = jnp.where(
                lane < HEAD_DIM, outs[0][r] * gate_col(r), outs[1][r] * gate_col(NSA_REP + r))

    per = SLC_TOPK * SLC_BLOCK // tq
    variants = []
    for v in range(n_slc // SLC_TOPK):
        nb = SLC_TOPK * (v + 1)
        nc = min(NR, -(-(nb * SLC_BLOCK // CMP_STRIDE) // LANES) * LANES)
        variants.append(functools.partial(tile, nc, nb))
    lax.switch(jnp.minimum(i // per, len(variants) - 1), variants)


def _nsa_cmp(nq, kc, vc, bt, misc, B, T, NR):
    tq = NSA_TILE
    nT = T // tq
    M = B * T
    return pl.pallas_call(
        functools.partial(_cmp_kernel, tq=tq, NR=NR),
        grid=(nT, B),
        in_specs=[pl.BlockSpec((NSA_HEADS, tq, LANES), lambda i, b: (0, b * nT + i, 0)),
                  pl.BlockSpec((1, NSA_GROUPS, NR, LANES), lambda i, b: (b, 0, 0, 0)),
                  pl.BlockSpec((1, NR, LANES), lambda i, b: (b, 0, 0)),
                  pl.BlockSpec((NSA_HEADS, tq, LANES), lambda i, b: (0, 0, 0)),
                  pl.BlockSpec((tq, LANES), lambda i, b: (b * nT + i, 0))],
        out_specs=[pl.BlockSpec((tq, NSA_REP * LANES), lambda i, b: (b * nT + i, 0)),
                   pl.BlockSpec((NSA_HEADS, tq, LANES), lambda i, b: (0, b * nT + i, 0))],
        out_shape=[jax.ShapeDtypeStruct((M, NSA_REP * LANES), F32),
                   jax.ShapeDtypeStruct((NSA_HEADS, M, LANES), BF16)],
        scratch_shapes=[pltpu.VMEM((NSA_HEADS, tq, NR), F32),
                        pltpu.VMEM((LANES // 2, tq), F32)],
        compiler_params=pltpu.CompilerParams(dimension_semantics=("arbitrary", "arbitrary"),
                                             vmem_limit_bytes=VMEM_LIMIT),
        name="nsa_cmp",
    )(nq, kc, vc, bt, misc)


def _nsa_kernel(q_ref, k_ref, vt_ref, ocmp_ref, misc_ref, dist_ref, o_ref, s_sc, qt_sc, bias_ref, wmask_ref, *,
                tq, T, nb):
    i = pl.program_id(1)
    tk = tq
    nq = NSA_REP * tq
    jp = jnp.maximum(i - 1, 0)
    jf = jnp.maximum(i - 2, 0)

    @pl.when((pl.program_id(0) == 0) & (i == 0))
    def _():
        kk = lax.broadcasted_iota(jnp.int32, (tk, tq), 0)
        qq = lax.broadcasted_iota(jnp.int32, (tk, tq), 1)
        for h in range(NSA_HEADS):
            g, r = divmod(h, NSA_REP)
            rows = jnp.broadcast_to(dist_ref[h:h + 1, :], (tk, 2 * tq))
            pair = pltpu.roll(rows, 0, 1, stride=1, stride_axis=0)
            bias_ref[g, 0, :, r * tq:(r + 1) * tq] = jnp.where(kk <= qq, pair[:, 0:tq], NEG_INF)
            bias_ref[g, 1, :, r * tq:(r + 1) * tq] = pair[:, tq:2 * tq]
        for g in range(NSA_GROUPS):
            bias_ref[g, 2] = jnp.full((tk, nq), NEG_INF, F32)
            bias_ref[g, 3] = jnp.zeros((tk, nq), F32)
        for r in range(NSA_REP):
            wmask_ref[0, :, r * tq:(r + 1) * tq] = jnp.where(kk > qq, 0.0, NEG_INF)
        wmask_ref[1] = jnp.full((tk, nq), NEG_INF, F32)

    streams = [(bb, g) for bb in range(nb) for g in range(NSA_GROUPS)]
    ns = len(streams)

    for st, (bb, g) in enumerate(streams):
        qt_sc[st] = q_ref[g * NSA_REP:(g + 1) * NSA_REP, bb].reshape(nq, LANES).T

    def stage(st, branch, j, slot, bias):
        bb, g = streams[st]
        ks = pl.multiple_of(j * tk, tk)
        s = _dot(k_ref[branch * NSA_GROUPS + g, bb, pl.ds(ks, tk), :], qt_sc[st])
        if bias is not None:
            s = s + bias
        return _stage_scores(s_sc.at[slot], s)

    def update(st, branch, j, slot, smax, carry):
        bb, g = streams[st]
        ks = pl.multiple_of(bb * T + j * tk, tk)
        r0 = (branch * NSA_GROUPS + g) * HEAD_DIM
        return _softmax_step(s_sc.at[slot], smax, vt_ref[r0:r0 + HEAD_DIM, pl.ds(ks, tk)], carry)

    init = (jnp.full((1, nq), NEG_INF, F32), jnp.zeros((HEAD_DIM + SUM_ROWS, nq), F32))
    prev_tile = jnp.where(i >= 1, 1, 2)

    def far(j, carry):
        state, smax = list(carry[0]), carry[1]
        for st in range(ns):
            if st + 1 < ns:
                smax_next = stage(st + 1, 0, j, (st + 1) % 2, None)
            else:
                smax_next = stage(0, 0, j + 1, 0, bias_ref[0, jnp.where(j + 1 == jp, 1, 3)])
            state[st] = update(st, 0, j, st % 2, smax, state[st])
            smax = smax_next
        return tuple(state), smax

    smax = stage(0, 0, 0, 0, bias_ref[0, jnp.where(i >= 2, 3, prev_tile)])
    c_slc, smax = lax.fori_loop(0, jp, far, ((init,) * ns, smax))
    c_slc = list(c_slc)
    c_win = [init] * ns

    near = [(0, jp, lambda g: bias_ref[g, prev_tile]),
            (0, i, lambda g: bias_ref[g, 0]),
            (1, jf, lambda g: wmask_ref[jnp.where(i >= 2, 0, 1)]),
            (1, jp, lambda g: bias_ref[g, prev_tile]),
            (1, i, lambda g: bias_ref[g, 0])]
    steps = [(st, branch, j, bias) for (branch, j, bias) in near for st in range(ns)]
    for n, (st, branch, j, _) in enumerate(steps):
        if n + 1 < len(steps):
            st2, branch2, j2, bias2 = steps[n + 1]
            smax_next = stage(st2, branch2, j2, (n + 1) % 2, bias2(streams[st2][1]))
        state = c_slc if branch == 0 else c_win
        state[st] = update(st, branch, j, n % 2, smax, state[st])
        smax = smax_next

    for bb in range(nb):
        gates = jax.nn.sigmoid(misc_ref[bb].T)
        for r in range(NSA_REP):
            halves = []
            for g in range(NSA_GROUPS):
                c0 = FOX_HEADS + 3 * (g * NSA_REP + r)
                sl = slice(r * tq, (r + 1) * tq)
                (_, accs), (_, accw) = c_slc[bb * NSA_GROUPS + g], c_win[bb * NSA_GROUPS + g]
                halves.append(
                    accs[0:HEAD_DIM, sl] * (gates[c0 + 1:c0 + 2, :] / accs[HEAD_DIM:HEAD_DIM + 1, sl])
                    + accw[0:HEAD_DIM, sl] * (gates[c0 + 2:c0 + 3, :] / accw[HEAD_DIM:HEAD_DIM + 1, sl]))
            o = jnp.concatenate(halves, axis=0).T + ocmp_ref[bb, :, r * LANES:(r + 1) * LANES]
            o_ref[bb, :, r * LANES:(r + 1) * LANES] = o.astype(BF16)


def _nsa_attention(qaug, nk, nvt, ocmp, misc, dist_bias, B, T, nb):
    tq = NSA_TILE
    assert WINDOW == 2 * tq and B % nb == 0
    nT = T // tq
    nq = NSA_REP * tq
    out = pl.pallas_call(
        functools.partial(_nsa_kernel, tq=tq, T=T, nb=nb),
        grid=(B // nb, nT),
        in_specs=[pl.BlockSpec((NSA_HEADS, nb, tq, LANES), lambda b, i: (0, b, i, 0)),
                  pl.BlockSpec((2 * NSA_GROUPS, nb, T, LANES), lambda b, i: (0, b, 0, 0)),
                  pl.BlockSpec((2 * KV_W, nb * T), lambda b, i: (0, b)),
                  pl.BlockSpec((nb, tq, NSA_REP * LANES), lambda b, i: (b, i, 0)),
                  pl.BlockSpec((nb, tq, LANES), lambda b, i: (b, i, 0)),
                  pl.BlockSpec((NSA_HEADS, 2 * tq), lambda b, i: (0, 0))],
        out_specs=pl.BlockSpec((nb, tq, NSA_REP * LANES), lambda b, i: (b, i, 0)),
        out_shape=jax.ShapeDtypeStruct((B, T, NSA_W), BF16),
        scratch_shapes=[pltpu.VMEM((2, tq, nq), F32),
                        pltpu.VMEM((nb * NSA_GROUPS, LANES, nq), BF16),
                        pltpu.VMEM((NSA_GROUPS, 4, tq, nq), F32),
                        pltpu.VMEM((2, tq, nq), F32)],
        compiler_params=pltpu.CompilerParams(dimension_semantics=("arbitrary", "arbitrary"),
                                             vmem_limit_bytes=VMEM_LIMIT),
        name="nsa_attn",
    )(qaug.reshape(NSA_HEADS, B, T, LANES), nk.reshape(2 * NSA_GROUPS, B, T, LANES), nvt,
      ocmp.reshape(B, T, NSA_REP * LANES), misc.reshape(B, T, LANES), dist_bias)
    return out.reshape(B * T, NSA_W)


def _merge_kernel(x_ref, g_ref, of_ref, on_ref, wga_ref, wgb_ref, wbf_ref, wbn_nat_ref, wo_ref, x1_ref, wbn_ref, *,
                  sub):
    @pl.when(pl.program_id(0) == 0)
    def _():
        for g in range(NSA_GROUPS):
            for r in range(NSA_REP):
                src = (g * NSA_REP + r) * HEAD_DIM
                dst = (r * NSA_GROUPS + g) * HEAD_DIM
                wbn_ref[dst:dst + HEAD_DIM, :] = wbn_nat_ref[src:src + HEAD_DIM, :]

    for r0 in range(0, x_ref.shape[0], sub):
        rows = slice(r0, r0 + sub)
        x = x_ref[rows, :]
        h = _rms(x, g_ref[...]).astype(BF16)
        ga = jax.nn.sigmoid(_dot(h, wga_ref[...]))
        gb = jax.nn.sigmoid(_dot(h, wgb_ref[...]))
        merged = ga * _dot(of_ref[rows, :], wbf_ref[...]) + gb * _dot(on_ref[rows, :], wbn_ref[...])
        x1_ref[rows, :] = x + _dot(merged.astype(BF16), wo_ref[...])


def _merge(x2, g, o_fox, o_nsa, wga, wgb, wbf, wbn, wo, tm):
    M = x2.shape[0]
    row = lambda i: (i, 0)
    full = lambda i: (0, 0)
    return pl.pallas_call(
        functools.partial(_merge_kernel, sub=tm // 2),
        grid=(M // tm,),
        in_specs=[pl.BlockSpec((tm, D_MODEL), row),
                  pl.BlockSpec((1, D_MODEL), full),
                  pl.BlockSpec((tm, FOX_W), row),
                  pl.BlockSpec((tm, NSA_W), row),
                  pl.BlockSpec((D_MODEL, D_MODEL), full),
                  pl.BlockSpec((D_MODEL, D_MODEL), full),
                  pl.BlockSpec((FOX_W, D_MODEL), full),
                  pl.BlockSpec((NSA_W, D_MODEL), full),
                  pl.BlockSpec((D_MODEL, D_MODEL), full)],
        out_specs=pl.BlockSpec((tm, D_MODEL), row),
        out_shape=jax.ShapeDtypeStruct((M, D_MODEL), F32),
        scratch_shapes=[pltpu.VMEM((NSA_W, D_MODEL), BF16)],
        compiler_params=pltpu.CompilerParams(dimension_semantics=("arbitrary",), vmem_limit_bytes=VMEM_LIMIT),
        name="merge",
    )(x2, g, o_fox, o_nsa, wga, wgb, wbf, wbn, wo)


def _mlp_kernel(x_ref, g_ref, w1_ref, w2_ref, gf_ref, o_ref, *, final, sub):
    for r0 in range(0, x_ref.shape[0], sub):
        x = x_ref[r0:r0 + sub, :]
        h = _rms(x, g_ref[...]).astype(BF16)
        u = jnp.maximum(_dot(h, w1_ref[...]), 0.0)
        y = x + _dot((u * u).astype(BF16), w2_ref[...])
        o_ref[r0:r0 + sub, :] = _rms(y, gf_ref[...]) if final else y


def _mlp(x1, g, w1, w2, gf, tm, sub, final):
    M = x1.shape[0]
    row = lambda i: (i, 0)
    full = lambda i: (0, 0)
    once = pl.Buffered(1)
    return pl.pallas_call(
        functools.partial(_mlp_kernel, final=final, sub=sub),
        grid=(M // tm,),
        in_specs=[pl.BlockSpec((tm, D_MODEL), row),
                  pl.BlockSpec((1, D_MODEL), full),
                  pl.BlockSpec((D_MODEL, D_FF), full, pipeline_mode=once),
                  pl.BlockSpec((D_FF, D_MODEL), full, pipeline_mode=once),
                  pl.BlockSpec((1, D_MODEL), full)],
        out_specs=pl.BlockSpec((tm, D_MODEL), row),
        out_shape=jax.ShapeDtypeStruct((M, D_MODEL), F32),
        compiler_params=pltpu.CompilerParams(dimension_semantics=("parallel",), vmem_limit_bytes=VMEM_LIMIT),
        name="mlp",
    )(x1, g, w1, w2, gf)


def _bias_tables(rel_tbl, tq):
    L = 2 * tq
    g = (rel_tbl - rel_tbl[REL_BUCKETS - 1]).T[:, _rel_bucket_np(np.arange(L))] * LOG2E
    base = CMP_STRIDE * (CMP_BAND // 2) - (CMP_BLOCK - 1)
    left = CMP_STRIDE * (CMP_BAND - 1) - base
    gp = jnp.concatenate([jnp.zeros((NSA_HEADS, left), F32), g], axis=1)
    n = left + L
    shifted = jnp.tile(gp, (1, CMP_BAND))[:, :CMP_BAND * (n - CMP_STRIDE)].reshape(NSA_HEADS, CMP_BAND, n - CMP_STRIDE)
    band = jnp.swapaxes(shifted[:, :, left + base:left + base + tq], 1, 2)
    return g, band


def _layer(x2, B, T, rel_tbl, g_attn, w_in, b_forget, pe_k, w1_k, w2_k, pe_v, w1_v, w2_v,
           w_br_fox, w_br_nsa, w_out, g_mlp, w_ff1, w_ff2, g_final, final):
    M = B * T
    NR = T // CMP_STRIDE
    fq, fk, nq, nk, fv, nv, r4, misc, w_ga, w_gb = _in_proj(x2, g_attn.reshape(1, D_MODEL), w_in, tm=512, T=T)

    c3 = _fox_decay(misc, b_forget.reshape(FOX_HEADS, 1), B, T)
    o_fox = _fox_attention(fq, fk, fv, c3, B, T, tq=512, tk=512, nh=8)

    w1 = jnp.stack([w1_k, w1_v]).astype(BF16)
    pe = jnp.stack([pe_k.reshape(1, -1), pe_v.reshape(1, -1)]).astype(BF16)
    pe = jnp.broadcast_to(pe, (2, 16, CMP_BLOCK * HEAD_DIM))
    zpad = jnp.zeros((CMP_HIDDEN, HEAD_DIM), F32)
    w2v_pad = jnp.stack([jnp.concatenate([w2_v, zpad], axis=1), jnp.concatenate([zpad, w2_v], axis=1)]).astype(BF16)
    w2k_pad = jnp.concatenate([w2_k, zpad], axis=1).astype(BF16)
    kc, vc = _compress(r4, w1, pe, w2k_pad, w2v_pad, B, NR)

    dist_bias, band = _bias_tables(rel_tbl, NSA_TILE)
    hi, mid, lo = _split3(band)
    bt = jnp.concatenate([hi, mid, lo, jnp.zeros_like(hi)], axis=-1)

    ocmp, qaug = _nsa_cmp(nq, kc, vc, bt, misc, B, T, NR)
    o_nsa = _nsa_attention(qaug, nk, nv, ocmp, misc, dist_bias, B, T, nb=2 if B % 2 == 0 else 1)

    x1 = _merge(x2, g_attn.reshape(1, D_MODEL), o_fox, o_nsa, w_ga, w_gb,
                w_br_fox.astype(BF16), w_br_nsa.astype(BF16), w_out.astype(BF16), tm=512)
    return _mlp(x1, g_mlp.reshape(1, D_MODEL), w_ff1.astype(BF16), w_ff2.astype(BF16),
                g_final.reshape(1, D_MODEL), tm=512, sub=256, final=final)


def kernel(x, rel_bias_table, g_attn, w_in, b_forget, cmp_pe_k, cmp_w1_k, cmp_w2_k, cmp_pe_v, cmp_w1_v, cmp_w2_v,
           w_br_fox, w_br_nsa, w_out, g_mlp, w_ff1, w_ff2, g_final):
    B, T, _ = x.shape
    depth = g_attn.shape[0]
    x2 = x.reshape(B * T, D_MODEL)
    for l in range(depth):
        x2 = _layer(x2, B, T, rel_bias_table, g_attn[l], w_in[l], b_forget[l], cmp_pe_k[l], cmp_w1_k[l],
                    cmp_w2_k[l], cmp_pe_v[l], cmp_w1_v[l], cmp_w2_v[l], w_br_fox[l], w_br_nsa[l], w_out[l],
                    g_mlp[l], w_ff1[l], w_ff2[l], g_final, final=(l == depth - 1))
    return x2.reshape(B, T, D_MODEL)
```
